```python
import jax, jax.numpy as jnp
from jax import lax
import numpy as np

D_MODEL = 1024
BATCH = 4
SEQ = 4096
DEPTH = 1

ROPE_THETA = 500000.0
BLOCK = 128
NEG = -1e30
RMS_EPS = 1e-6
LN_EPS = 1e-5

MLA_HEADS = 8
MLA_NOPE_DIM = 64
MLA_ROPE_DIM = 32
MLA_V_DIM = 64
Q_LORA_RANK = 384
KV_LORA_RANK = 256
MLA_WIDTH = MLA_HEADS * MLA_V_DIM

DIL_HEADS = 8
DIL_HEAD_DIM = 64
DIL_ROT_DIM = DIL_HEAD_DIM // 4
DIL_WIDTH = DIL_HEADS * DIL_HEAD_DIM
DIL_CONFIGS = ((128, 1), (512, 4), (2048, 16))

MIX_WIDTH = MLA_WIDTH + DIL_WIDTH
IN_SPLITS = (Q_LORA_RANK, KV_LORA_RANK, MLA_ROPE_DIM, MLA_WIDTH, DIL_WIDTH, DIL_WIDTH, DIL_WIDTH, DIL_WIDTH)
IN_WIDTH = sum(IN_SPLITS)

DEEPNORM_ALPHA = (2.0 * DEPTH) ** 0.25
DEEPNORM_BETA = (8.0 * DEPTH) ** -0.25

kernel_name = "hybrid_mla_dilated_deepnorm"


def rmsnorm(t, g):
    tf = t.astype(jnp.float32)
    tf = tf * lax.rsqrt(jnp.mean(tf * tf, axis=-1, keepdims=True) + RMS_EPS)
    return (tf * g.astype(jnp.float32)).astype(t.dtype)


def layernorm(t, g, b):
    tf = t.astype(jnp.float32)
    mu = jnp.mean(tf, axis=-1, keepdims=True)
    var = jnp.mean(jnp.square(tf - mu), axis=-1, keepdims=True)
    return ((tf - mu) * lax.rsqrt(var + LN_EPS) * g.astype(jnp.float32) + b.astype(jnp.float32)).astype(t.dtype)


def rope_tables(seq_len, dim):
    inv_freq = ROPE_THETA ** (-jnp.arange(0, dim, 2, dtype=jnp.float32) / dim)
    ang = jnp.arange(seq_len, dtype=jnp.float32)[:, None] * inv_freq[None, :]
    return jnp.cos(ang), jnp.sin(ang)


def apply_rope(t, cos, sin):
    t1, t2 = jnp.split(t.astype(jnp.float32), 2, axis=-1)
    c, s = cos[:, None, :], sin[:, None, :]
    return jnp.concatenate([t1 * c - t2 * s, t1 * s + t2 * c], axis=-1).astype(t.dtype)


def mla_attention(c_q, c_kv, k_rope, q_norm_g, kv_norm_g, w_uq, w_ukv):
    B, S, _ = c_q.shape
    H, DN, DR, DV = MLA_HEADS, MLA_NOPE_DIM, MLA_ROPE_DIM, MLA_V_DIM
    cos, sin = rope_tables(S, DR)
    q = (rmsnorm(c_q, q_norm_g) @ w_uq).reshape(B, S, H, DN + DR)
    q = jnp.concatenate([q[..., :DN], apply_rope(q[..., DN:], cos, sin)], axis=-1)
    kv = (rmsnorm(c_kv, kv_norm_g) @ w_ukv).reshape(B, S, H, DN + DV)
    k_nope, v = kv[..., :DN], kv[..., DN:]
    k_pe = apply_rope(k_rope[:, :, None, :], cos, sin)
    k = jnp.concatenate([k_nope, jnp.broadcast_to(k_pe, (B, S, H, DR))], axis=-1)
    scale = (DN + DR) ** -0.5
    nblk = S // BLOCK
    qb = q.reshape(B, nblk, BLOCK, H, DN + DR).transpose(1, 0, 3, 2, 4)
    kpos = jnp.arange(S)

    def one_block(args):
        q_blk, i = args
        s = jnp.einsum('bhqd,bkhd->bhqk', q_blk, k).astype(jnp.float32) * scale
        qpos = i * BLOCK + jnp.arange(BLOCK)
        s = jnp.where(kpos[None, :] <= qpos[:, None], s, NEG)
        p = jax.nn.softmax(s, axis=-1)
        return jnp.einsum('bhqk,bkhd->bqhd', p.astype(v.dtype), v)

    out = lax.map(one_block, (qb, jnp.arange(nblk)))
    return out.transpose(1, 0, 2, 3, 4).reshape(B, S, H * DV)


def dilated_branch(q, k, v, window, dilation):
    B, S, H, D = q.shape
    n_back = window // dilation
    seg = dilation * BLOCK
    S_pad = -(-S // seg) * seg
    pad = ((0, 0), (0, S_pad - S), (0, 0), (0, 0))
    L = S_pad // dilation
    nb = L // BLOCK

    def to_sub(t):
        t = jnp.pad(t, pad).reshape(B, L, dilation, H, D).transpose(0, 2, 3, 1, 4)
        return t.reshape(B, dilation, H, nb, BLOCK, D)

    def with_prev(t):
        prev = jnp.pad(t, ((0, 0), (0, 0), (0, 0), (1, 0), (0, 0), (0, 0)))[:, :, :, :-1]
        return jnp.concatenate([prev, t], axis=4)

    qs = to_sub(q)
    ks = with_prev(to_sub(k))
    vs = with_prev(to_sub(v))
    s = jnp.einsum('bdhnqe,bdhnke->bdhnqk', qs, ks).astype(jnp.float32)
    q_loc = jnp.arange(BLOCK)
    k_loc = jnp.arange(2 * BLOCK) - BLOCK
    dist = q_loc[:, None] - k_loc[None, :]
    valid = (jnp.arange(nb)[:, None, None] * BLOCK + k_loc[None, None, :]) >= 0
    mask = (dist >= 0) & (dist <= n_back) & valid
    s = jnp.where(mask, s, NEG)
    m = jnp.max(s, axis=-1, keepdims=True)
    p = jnp.exp(s - m)
    l = jnp.sum(p, axis=-1, keepdims=True)
    num = jnp.einsum('bdhnqk,bdhnke->bdhnqe', p, vs.astype(jnp.float32))

    def to_seq(t):
        c = t.shape[-1]
        t = t.reshape(B, dilation, H, L, c).transpose(0, 3, 1, 2, 4).reshape(B, S_pad, H, c)
        return t[:, :S]

    return to_seq(num), to_seq(m), to_seq(l)


def dilated_attention(q, k, v):
    B, S, _ = q.shape
    H, D = DIL_HEADS, DIL_HEAD_DIM
    cos, sin = rope_tables(S, DIL_ROT_DIM)

    def heads_rope(t):
        t = t.reshape(B, S, H, D)
        return jnp.concatenate([apply_rope(t[..., :DIL_ROT_DIM], cos, sin), t[..., DIL_ROT_DIM:]], axis=-1)

    qh = heads_rope(q) * (D ** -0.5)
    kh = heads_rope(k)
    vh = v.reshape(B, S, H, D)
    parts = [dilated_branch(qh, kh, vh, w, d) for (w, d) in DIL_CONFIGS]
    m_all = jnp.max(jnp.stack([pm for (_, pm, _) in parts], axis=0), axis=0)
    num = jnp.zeros((B, S, H, D), jnp.float32)
    den = jnp.zeros((B, S, H, 1), jnp.float32)
    for (pn, pm, pl) in parts:
        w = jnp.exp(pm - m_all)
        num = num + w * pn
        den = den + w * pl
    return (num / den).astype(q.dtype).reshape(B, S, H * D)


def setup_inputs(seed: int = 0) -> dict:
    key = jax.random.key(seed)
    ks = jax.random.split(key, 9)
    f32 = jnp.float32
    x = jax.random.normal(ks[0], (BATCH, SEQ, D_MODEL), f32)
    w_in = jax.random.normal(ks[1], (D_MODEL, IN_WIDTH), f32) * D_MODEL ** -0.5
    q_norm_g = 1.0 + 0.02 * jax.random.normal(ks[2], (Q_LORA_RANK,), f32)
    kv_norm_g = 1.0 + 0.02 * jax.random.normal(ks[3], (KV_LORA_RANK,), f32)
    w_uq = jax.random.normal(ks[4], (Q_LORA_RANK, MLA_HEADS * (MLA_NOPE_DIM + MLA_ROPE_DIM)), f32) * Q_LORA_RANK ** -0.5
    w_ukv = jax.random.normal(ks[5], (KV_LORA_RANK, MLA_HEADS * (MLA_NOPE_DIM + MLA_V_DIM)), f32) * KV_LORA_RANK ** -0.5
    w_out = jax.random.normal(ks[6], (MIX_WIDTH, D_MODEL), f32) * (MIX_WIDTH ** -0.5) * DEEPNORM_BETA
    ln_g = 1.0 + 0.02 * jax.random.normal(ks[7], (D_MODEL,), f32)
    ln_b = 0.02 * jax.random.normal(ks[8], (D_MODEL,), f32)
    return {"x": x, "w_in": w_in, "q_norm_g": q_norm_g, "kv_norm_g": kv_norm_g, "w_uq": w_uq,
            "w_ukv": w_ukv, "w_out": w_out, "ln_g": ln_g, "ln_b": ln_b}


def reference(x, w_in, q_norm_g, kv_norm_g, w_uq, w_ukv, w_out, ln_g, ln_b):
    offs = np.cumsum(IN_SPLITS)[:-1].tolist()
    for _ in range(DEPTH):
        h = x @ w_in
        c_q, c_kv, k_rope, g_a, q_b, k_b, v_b, g_b = jnp.split(h, offs, axis=-1)
        y_a = mla_attention(c_q, c_kv, k_rope, q_norm_g, kv_norm_g, w_uq, w_ukv) * jax.nn.silu(g_a)
        y_b = dilated_attention(q_b, k_b, v_b) * jax.nn.silu(g_b)
        mix = jnp.concatenate([y_a, y_b], axis=-1)
        x = layernorm(DEEPNORM_ALPHA * x + mix @ w_out, ln_g, ln_b)
    return x
```

```python
import functools

import jax
import jax.numpy as jnp
from jax import lax
from jax.experimental import pallas as pl
from jax.experimental.pallas import tpu as pltpu

D_MODEL = 1024
ROPE_THETA = 500000.0
BLOCK = 128
NEG = -1e30
RMS_EPS = 1e-6
LN_EPS = 1e-5

MLA_HEADS = 8
MLA_NOPE_DIM = 64
MLA_ROPE_DIM = 32
MLA_V_DIM = 64
Q_LORA_RANK = 384
KV_LORA_RANK = 256
MLA_WIDTH = MLA_HEADS * MLA_V_DIM

DIL_HEADS = 8
DIL_HEAD_DIM = 64
DIL_ROT_DIM = DIL_HEAD_DIM // 4
DIL_WIDTH = DIL_HEADS * DIL_HEAD_DIM
DILATIONS = (1, 4, 16)

DEPTH = 1
DEEPNORM_ALPHA = (2.0 * DEPTH) ** 0.25

LANES = 128
HEAD_PAIRS = MLA_HEADS // 2
VMEM_LIMIT_BYTES = 56 * 1024 * 1024

_OFF_CQ = 0
_OFF_CKV = _OFF_CQ + Q_LORA_RANK
_OFF_KPE = _OFF_CKV + KV_LORA_RANK
_OFF_GA = _OFF_KPE + LANES
_OFF_QB = _OFF_GA + MLA_WIDTH
_OFF_KB = _OFF_QB + DIL_WIDTH
_OFF_VB = _OFF_KB + DIL_WIDTH
_OFF_GB = _OFF_VB + DIL_WIDTH
_W_BIG = _OFF_GB + DIL_WIDTH

PROJ_ROWS = 256
MLA_TILE = 512
OUT_ROWS = 512
COMBINE_ROWS = 256


def _rope_lanes(x, cos, sin_fwd, sin_bwd, half):
    fwd = pltpu.roll(x, half, 1)
    bwd = pltpu.roll(x, LANES - half, 1)
    return x * cos + fwd * sin_fwd + bwd * sin_bwd


def _proj_kernel(x_ref, w_ref, wuq_ref, wuk_ref, wuv_ref, qg_ref, kvg_ref,
                 mcos_ref, msf_ref, msb_ref, dcos_ref, dsf_ref, dsb_ref,
                 qm_ref, km_ref, vm_ref, ga_ref, qd_ref, kd_ref, vd_ref, gb_ref):
    f32 = jnp.float32
    bf16 = jnp.bfloat16
    xb = x_ref[...].astype(bf16)

    def seg(lo, width):
        return jnp.dot(xb, w_ref[:, lo:lo + width], preferred_element_type=f32)

    def rms(t, g):
        return t * lax.rsqrt(jnp.mean(t * t, axis=-1, keepdims=True) + RMS_EPS) * g

    mcos, msf, msb = mcos_ref[...], msf_ref[...], msb_ref[...]
    dcos, dsf, dsb = dcos_ref[...], dsf_ref[...], dsb_ref[...]
    mla_scale = (MLA_NOPE_DIM + MLA_ROPE_DIM) ** -0.5
    dil_scale = DIL_HEAD_DIM ** -0.5

    cq = rms(seg(_OFF_CQ, Q_LORA_RANK), qg_ref[...]).astype(bf16)
    qf = jnp.dot(cq, wuq_ref[...], preferred_element_type=f32)
    for h in range(MLA_HEADS):
        blk = qf[:, h * LANES:(h + 1) * LANES]
        qm_ref[h] = (_rope_lanes(blk, mcos, msf, msb, MLA_ROPE_DIM // 2) * mla_scale).astype(bf16)

    ckv = rms(seg(_OFF_CKV, KV_LORA_RANK), kvg_ref[...]).astype(bf16)
    kf = jnp.dot(ckv, wuk_ref[...], preferred_element_type=f32)
    vf = jnp.dot(ckv, wuv_ref[...], preferred_element_type=f32)
    kpe = _rope_lanes(seg(_OFF_KPE, LANES), mcos, msf, msb, MLA_ROPE_DIM // 2)
    for h in range(MLA_HEADS):
        km_ref[h] = (kf[:, h * LANES:(h + 1) * LANES] + kpe).astype(bf16)
    for p in range(HEAD_PAIRS):
        vm_ref[p] = vf[:, p * LANES:(p + 1) * LANES].astype(bf16)

    ga = jax.nn.silu(seg(_OFF_GA, MLA_WIDTH))
    gb = jax.nn.silu(seg(_OFF_GB, DIL_WIDTH))
    qb = seg(_OFF_QB, DIL_WIDTH)
    kb = seg(_OFF_KB, DIL_WIDTH)
    vb = seg(_OFF_VB, DIL_WIDTH)
    for p in range(HEAD_PAIRS):
        sl = slice(p * LANES, (p + 1) * LANES)
        ga_ref[p] = ga[:, sl].astype(bf16)
        gb_ref[p] = gb[:, sl].astype(bf16)
        qd_ref[p] = (_rope_lanes(qb[:, sl], dcos, dsf, dsb, DIL_ROT_DIM // 2) * dil_scale).astype(bf16)
        kd_ref[p] = _rope_lanes(kb[:, sl], dcos, dsf, dsb, DIL_ROT_DIM // 2).astype(bf16)
        vd_ref[p] = vb[:, sl].astype(bf16)


def _mla_kernel(q_ref, k_ref, v_ref, g_ref, o_ref, acc_ref, m_ref, l_ref):
    f32 = jnp.float32
    t = MLA_TILE
    qi = pl.program_id(2)
    q = (q_ref[0], q_ref[1])

    acc_ref[...] = jnp.zeros_like(acc_ref)
    m_ref[...] = jnp.full_like(m_ref, NEG)
    l_ref[...] = jnp.zeros_like(l_ref)

    row = lax.broadcasted_iota(jnp.int32, (t, t), 0)
    col = lax.broadcasted_iota(jnp.int32, (t, t), 1)
    causal = col <= row

    def tile(j, masked):
        ks = pl.ds(pl.multiple_of(j * t, t), t)
        v = v_ref[0, ks, :]
        for h in range(2):
            s = lax.dot_general(q[h], k_ref[h, ks, :], (((1,), (1,)), ((), ())), preferred_element_type=f32)
            if masked:
                s = jnp.where(causal, s, NEG)
            m_old = m_ref[h]
            m_new = jnp.maximum(m_old, jnp.max(s, axis=1, keepdims=True))
            alpha = jnp.exp(m_old - m_new)
            p = jnp.exp(s - m_new)
            l_ref[h] = alpha * l_ref[h] + jnp.sum(p, axis=1, keepdims=True)
            acc_ref[h] = alpha * acc_ref[h] + jnp.dot(p.astype(v.dtype), v, preferred_element_type=f32)
            m_ref[h] = m_new

    def body(j, carry):
        tile(j, masked=False)
        return carry

    lax.fori_loop(0, qi, body, 0)
    tile(qi, masked=True)

    lane = lax.broadcasted_iota(jnp.int32, (t, LANES), 1)
    y = jnp.where(lane < MLA_V_DIM, acc_ref[0] / l_ref[0], acc_ref[1] / l_ref[1])
    o_ref[...] = (y * g_ref[0].astype(f32)).astype(o_ref.dtype)


def _dilated_kernel(q1_ref, k1_ref, v1_ref, q4_ref, k4_ref, v4_ref, q16_ref, k16_ref, v16_ref,
                    g_ref, o_ref, num_ref, max_ref, den_ref):
    f32 = jnp.float32
    r = pl.program_id(2)
    seq = num_ref.shape[1]

    lane = lax.broadcasted_iota(jnp.int32, (BLOCK, LANES), 1)
    low_half = lane < DIL_HEAD_DIM
    qi = lax.broadcasted_iota(jnp.int32, (BLOCK, 2 * BLOCK), 0)
    kj = lax.broadcasted_iota(jnp.int32, (BLOCK, 2 * BLOCK), 1)
    dist_first = qi - kj
    dist_later = qi + BLOCK - kj

    def branch(idx, dil, q_ref, k_ref, v_ref):
        n_blocks = q_ref.shape[2] // BLOCK

        def block(n, carry):
            q = q_ref[0, 0, pl.ds(pl.multiple_of(n * BLOCK, BLOCK), BLOCK), :]
            start = pl.multiple_of(jnp.maximum(n - 1, 0) * BLOCK, BLOCK)
            k = k_ref[0, 0, pl.ds(start, 2 * BLOCK), :]
            v = v_ref[0, 0, pl.ds(start, 2 * BLOCK), :]
            dist = jnp.where(n == 0, dist_first, dist_later)
            valid = (dist >= 0) & (dist <= BLOCK)
            outs = []
            for h in range(2):
                own = low_half if h == 0 else ~low_half
                qh = jnp.where(own, q, jnp.zeros_like(q))
                s = lax.dot_general(qh, k, (((1,), (1,)), ((), ())), preferred_element_type=f32)
                s = jnp.where(valid, s, NEG)
                m = jnp.max(s, axis=1, keepdims=True)
                p = jnp.exp(s - m)
                den = jnp.sum(p, axis=1, keepdims=True)
                num = jnp.dot(p.astype(v.dtype), v, preferred_element_type=f32)
                outs.append((num, m, den))
            num = jnp.where(low_half, outs[0][0], outs[1][0])
            m = jnp.where(low_half, outs[0][1], outs[1][1])
            den = jnp.where(low_half, outs[0][2], outs[1][2])
            if dil == 1:
                rows = pl.ds(pl.multiple_of(n * BLOCK, BLOCK), BLOCK)
            else:
                rows = pl.ds(r + n * (BLOCK * dil), BLOCK, stride=dil)
            num_ref[idx, rows, :] = num
            max_ref[idx, rows, :] = m
            den_ref[idx, rows, :] = den
            return carry

        lax.fori_loop(0, n_blocks, block, 0)

    @pl.when(r == 0)
    def _():
        branch(0, DILATIONS[0], q1_ref, k1_ref, v1_ref)

    @pl.when(r < DILATIONS[1])
    def _():
        branch(1, DILATIONS[1], q4_ref, k4_ref, v4_ref)

    branch(2, DILATIONS[2], q16_ref, k16_ref, v16_ref)

    @pl.when(r == DILATIONS[2] - 1)
    def _():
        def merge(c, carry):
            rows = pl.ds(pl.multiple_of(c * COMBINE_ROWS, COMBINE_ROWS), COMBINE_ROWS)
            ms = [max_ref[i, rows, :] for i in range(3)]
            m_all = jnp.maximum(jnp.maximum(ms[0], ms[1]), ms[2])
            num = jnp.zeros((COMBINE_ROWS, LANES), f32)
            den = jnp.zeros((COMBINE_ROWS, LANES), f32)
            for i in range(3):
                w = jnp.exp(ms[i] - m_all)
                num = num + w * num_ref[i, rows, :]
                den = den + w * den_ref[i, rows, :]
            o_ref[rows, :] = (num / den * g_ref[0, rows, :].astype(f32)).astype(o_ref.dtype)
            return carry

        lax.fori_loop(0, seq // COMBINE_ROWS, merge, 0)


def _out_kernel(x_ref, ya_ref, yb_ref, wa_ref, wb_ref, g_ref, b_ref, o_ref):
    f32 = jnp.float32
    h = DEEPNORM_ALPHA * x_ref[...]
    h = h + jnp.dot(ya_ref[...], wa_ref[...], preferred_element_type=f32)
    h = h + jnp.dot(yb_ref[...], wb_ref[...], preferred_element_type=f32)
    mu = jnp.mean(h, axis=-1, keepdims=True)
    c = h - mu
    var = jnp.mean(c * c, axis=-1, keepdims=True)
    o_ref[...] = c * lax.rsqrt(var + LN_EPS) * g_ref[...] + b_ref[...]


def _rope_tables(seq, rot_dim, period, rot_offset, pass_rest):
    half = rot_dim // 2
    inv_freq = ROPE_THETA ** (-jnp.arange(0, rot_dim, 2, dtype=jnp.float32) / rot_dim)
    ang = jnp.arange(seq, dtype=jnp.float32)[:, None] * inv_freq[None, :]
    cos, sin = jnp.cos(ang), jnp.sin(ang)
    zeros = jnp.zeros((seq, half), jnp.float32)
    rest = period - rot_offset - rot_dim
    fill = jnp.ones if pass_rest else jnp.zeros
    group_cos = jnp.concatenate([jnp.ones((seq, rot_offset), jnp.float32), cos, cos,
                                 fill((seq, rest), jnp.float32)], axis=1)
    group_fwd = jnp.concatenate([jnp.zeros((seq, rot_offset), jnp.float32), zeros, sin,
                                 jnp.zeros((seq, rest), jnp.float32)], axis=1)
    group_bwd = jnp.concatenate([jnp.zeros((seq, rot_offset), jnp.float32), -sin, zeros,
                                 jnp.zeros((seq, rest), jnp.float32)], axis=1)
    reps = LANES // period
    return tuple(jnp.tile(t, (1, reps)) for t in (group_cos, group_fwd, group_bwd))


def _params(*semantics):
    return pltpu.CompilerParams(dimension_semantics=semantics, vmem_limit_bytes=VMEM_LIMIT_BYTES)


def kernel(x, w_in, q_norm_g, kv_norm_g, w_uq, w_ukv, w_out, ln_g, ln_b):
    f32, bf16 = jnp.float32, jnp.bfloat16
    batch, seq, _ = x.shape
    rows = batch * seq
    x2 = x.reshape(rows, D_MODEL)

    offs = (0, Q_LORA_RANK, Q_LORA_RANK + KV_LORA_RANK, Q_LORA_RANK + KV_LORA_RANK + MLA_ROPE_DIM)
    kpe_w = jnp.zeros((D_MODEL, LANES), f32).at[:, MLA_NOPE_DIM:MLA_NOPE_DIM + MLA_ROPE_DIM].set(
        w_in[:, offs[2]:offs[3]])
    w_big = jnp.concatenate([w_in[:, :offs[2]], kpe_w, w_in[:, offs[3]:]], axis=1).astype(bf16)
    assert w_big.shape[1] == _W_BIG
    dk = MLA_NOPE_DIM + MLA_ROPE_DIM
    wuq = jnp.pad(w_uq.reshape(Q_LORA_RANK, MLA_HEADS, dk), ((0, 0), (0, 0), (0, LANES - dk)))
    wuq = wuq.reshape(Q_LORA_RANK, MLA_HEADS * LANES).astype(bf16)
    wukv = w_ukv.reshape(KV_LORA_RANK, MLA_HEADS, MLA_NOPE_DIM + MLA_V_DIM)
    wuk = jnp.pad(wukv[:, :, :MLA_NOPE_DIM], ((0, 0), (0, 0), (0, LANES - MLA_NOPE_DIM)))
    wuk = wuk.reshape(KV_LORA_RANK, MLA_HEADS * LANES).astype(bf16)
    wuv = wukv[:, :, MLA_NOPE_DIM:].reshape(KV_LORA_RANK, MLA_WIDTH).astype(bf16)
    wa = w_out[:MLA_WIDTH].astype(bf16)
    wb = w_out[MLA_WIDTH:].astype(bf16)

    mla_tabs = _rope_tables(seq, MLA_ROPE_DIM, LANES, MLA_NOPE_DIM, pass_rest=False)
    dil_tabs = _rope_tables(seq, DIL_ROT_DIM, DIL_HEAD_DIM, 0, pass_rest=True)

    tm = PROJ_ROWS
    seq_tiles = seq // tm
    full = lambda shape: pl.BlockSpec(shape, lambda i: (0,) * len(shape))
    tab = pl.BlockSpec((tm, LANES), lambda i: (i % seq_tiles, 0))
    slab = lambda n: pl.BlockSpec((n, tm, LANES), lambda i: (0, i, 0))
    slab_shape = lambda n: jax.ShapeDtypeStruct((n, rows, LANES), bf16)
    qm, km, vm, ga, qd, kd, vd, gb = pl.pallas_call(
        _proj_kernel,
        grid=(rows // tm,),
        in_specs=[pl.BlockSpec((tm, D_MODEL), lambda i: (i, 0)),
                  full((D_MODEL, _W_BIG)), full(wuq.shape), full(wuk.shape), full(wuv.shape),
                  full((1, Q_LORA_RANK)), full((1, KV_LORA_RANK)),
                  tab, tab, tab, tab, tab, tab],
        out_specs=[slab(MLA_HEADS), slab(MLA_HEADS), slab(HEAD_PAIRS), slab(HEAD_PAIRS),
                   slab(HEAD_PAIRS), slab(HEAD_PAIRS), slab(HEAD_PAIRS), slab(HEAD_PAIRS)],
        out_shape=[slab_shape(MLA_HEADS), slab_shape(MLA_HEADS), slab_shape(HEAD_PAIRS), slab_shape(HEAD_PAIRS),
                   slab_shape(HEAD_PAIRS), slab_shape(HEAD_PAIRS), slab_shape(HEAD_PAIRS), slab_shape(HEAD_PAIRS)],
        compiler_params=_params("parallel"),
        name="proj",
    )(x2, w_big, wuq, wuk, wuv, q_norm_g.reshape(1, -1), kv_norm_g.reshape(1, -1), *mla_tabs, *dil_tabs)

    t = MLA_TILE
    nq = seq // t
    ya = pl.pallas_call(
        _mla_kernel,
        grid=(batch, HEAD_PAIRS, nq),
        in_specs=[pl.BlockSpec((2, t, LANES), lambda b, p, i: (p, b * nq + i, 0)),
                  pl.BlockSpec((2, seq, LANES), lambda b, p, i: (p, b, 0)),
                  pl.BlockSpec((1, seq, LANES), lambda b, p, i: (p, b, 0)),
                  pl.BlockSpec((1, t, LANES), lambda b, p, i: (p, b * nq + i, 0))],
        out_specs=pl.BlockSpec((t, LANES), lambda b, p, i: (b * nq + i, p)),
        out_shape=jax.ShapeDtypeStruct((rows, MLA_WIDTH), bf16),
        scratch_shapes=[pltpu.VMEM((2, t, LANES), f32), pltpu.VMEM((2, t, 1), f32), pltpu.VMEM((2, t, 1), f32)],
        compiler_params=_params("parallel", "parallel", "arbitrary"),
        name="mla",
    )(qm, km, vm, ga)

    def strided_view(a, dil):
        return a.reshape(HEAD_PAIRS, batch, seq // dil, dil * LANES)

    def strided_spec(dil):
        return pl.BlockSpec((1, 1, seq // dil, LANES),
                            lambda b, p, r: (p, b, 0, jnp.minimum(r, dil - 1)))

    dil_in, dil_specs = [], []
    for dil in DILATIONS:
        for a in (qd, kd, vd):
            dil_in.append(strided_view(a, dil))
            dil_specs.append(strided_spec(dil))
    yb = pl.pallas_call(
        _dilated_kernel,
        grid=(batch, HEAD_PAIRS, DILATIONS[-1]),
        in_specs=dil_specs + [pl.BlockSpec((1, seq, LANES), lambda b, p, r: (p, b, 0))],
        out_specs=pl.BlockSpec((seq, LANES), lambda b, p, r: (b, p)),
        out_shape=jax.ShapeDtypeStruct((rows, DIL_WIDTH), bf16),
        scratch_shapes=[pltpu.VMEM((3, seq, LANES), f32)] * 3,
        compiler_params=_params("parallel", "parallel", "arbitrary"),
        name="dilated",
    )(*dil_in, gb)

    to = OUT_ROWS
    const = lambda shape: pl.BlockSpec(shape, lambda i: (0,) * len(shape))
    out = pl.pallas_call(
        _out_kernel,
        grid=(rows // to,),
        in_specs=[pl.BlockSpec((to, D_MODEL), lambda i: (i, 0)),
                  pl.BlockSpec((to, MLA_WIDTH), lambda i: (i, 0)),
                  pl.BlockSpec((to, DIL_WIDTH), lambda i: (i, 0)),
                  const((MLA_WIDTH, D_MODEL)), const((DIL_WIDTH, D_MODEL)),
                  const((1, D_MODEL)), const((1, D_MODEL))],
        out_specs=pl.BlockSpec((to, D_MODEL), lambda i: (i, 0)),
        out_shape=jax.ShapeDtypeStruct((rows, D_MODEL), f32),
        compiler_params=_params("parallel"),
        name="out",
    )(x2, ya, yb, wa, wb, ln_g.reshape(1, -1), ln_b.reshape(1, -1))
    return out.reshape(batch, seq, D_MODEL)
```

```python
import functools

import jax
import jax.numpy as jnp
from jax import lax
from jax.experimental import pallas as pl
from jax.experimental.pallas import tpu as pltpu

D_MODEL = 1024
ROPE_THETA = 500000.0
BLOCK = 128
NEG = -1e30
RMS_EPS = 1e-6
LN_EPS = 1e-5

MLA_HEADS = 8
MLA_NOPE_DIM = 64
MLA_ROPE_DIM = 32
MLA_V_DIM = 64
Q_LORA_RANK = 384
KV_LORA_RANK = 256
MLA_WIDTH = MLA_HEADS * MLA_V_DIM

DIL_HEADS = 8
DIL_HEAD_DIM = 64
DIL_ROT_DIM = DIL_HEAD_DIM // 4
DIL_WIDTH = DIL_HEADS * DIL_HEAD_DIM
DILATIONS = (1, 4, 16)

DEPTH = 1
DEEPNORM_ALPHA = (2.0 * DEPTH) ** 0.25
LOG2_E = 1.4426950408889634

LANES = 128
HEAD_PAIRS = MLA_HEADS // 2
VMEM_LIMIT_BYTES = 56 * 1024 * 1024

_OFF_CQ = 0
_OFF_CKV = _OFF_CQ + Q_LORA_RANK
_OFF_KPE = _OFF_CKV + KV_LORA_RANK
_OFF_GA = _OFF_KPE + LANES
_OFF_QB = _OFF_GA + MLA_WIDTH
_OFF_KB = _OFF_QB + DIL_WIDTH
_OFF_VB = _OFF_KB + DIL_WIDTH
_OFF_GB = _OFF_VB + DIL_WIDTH
_W_BIG = _OFF_GB + DIL_WIDTH

PROJ_ROWS = 256
MLA_TILE = 512
OUT_ROWS = 512
COMBINE_ROWS = 256
DIL_GROUP = 4


def _rope_lanes(x, cos, sin_fwd, sin_bwd, half):
    fwd = pltpu.roll(x, half, 1)
    bwd = pltpu.roll(x, LANES - half, 1)
    return x * cos + fwd * sin_fwd + bwd * sin_bwd


def _proj_kernel(x_ref, w_ref, wuq_ref, wuk_ref, wuvt_ref, qg_ref, kvg_ref,
                 mcos_ref, msf_ref, msb_ref, dcos_ref, dsf_ref, dsb_ref,
                 qm_ref, km_ref, vt_ref, ga_ref, qd_ref, kd_ref, vd_ref, gb_ref):
    f32 = jnp.float32
    bf16 = jnp.bfloat16
    xb = x_ref[...].astype(bf16)

    def seg(lo, width):
        return jnp.dot(xb, w_ref[:, lo:lo + width], preferred_element_type=f32)

    def rms(t, g):
        return t * lax.rsqrt(jnp.mean(t * t, axis=-1, keepdims=True) + RMS_EPS) * g

    mcos, msf, msb = mcos_ref[...], msf_ref[...], msb_ref[...]
    dcos, dsf, dsb = dcos_ref[...], dsf_ref[...], dsb_ref[...]
    mla_scale = (MLA_NOPE_DIM + MLA_ROPE_DIM) ** -0.5 * LOG2_E
    dil_scale = DIL_HEAD_DIM ** -0.5 * LOG2_E

    cq = rms(seg(_OFF_CQ, Q_LORA_RANK), qg_ref[...]).astype(bf16)
    qf = jnp.dot(cq, wuq_ref[...], preferred_element_type=f32)
    for h in range(MLA_HEADS):
        blk = qf[:, h * LANES:(h + 1) * LANES]
        qm_ref[h] = (_rope_lanes(blk, mcos, msf, msb, MLA_ROPE_DIM // 2) * mla_scale).astype(bf16)

    ckv = rms(seg(_OFF_CKV, KV_LORA_RANK), kvg_ref[...]).astype(bf16)
    kf = jnp.dot(ckv, wuk_ref[...], preferred_element_type=f32)
    kpe = _rope_lanes(seg(_OFF_KPE, LANES), mcos, msf, msb, MLA_ROPE_DIM // 2)
    for h in range(MLA_HEADS):
        km_ref[h] = (kf[:, h * LANES:(h + 1) * LANES] + kpe).astype(bf16)
    vt = lax.dot_general(wuvt_ref[...], ckv, (((1,), (1,)), ((), ())), preferred_element_type=f32)
    vt_ref[0] = vt.astype(bf16)

    ga = jax.nn.silu(seg(_OFF_GA, MLA_WIDTH))
    gb = jax.nn.silu(seg(_OFF_GB, DIL_WIDTH))
    qb = seg(_OFF_QB, DIL_WIDTH)
    kb = seg(_OFF_KB, DIL_WIDTH)
    vb = seg(_OFF_VB, DIL_WIDTH)
    for p in range(HEAD_PAIRS):
        sl = slice(p * LANES, (p + 1) * LANES)
        ga_ref[p] = ga[:, sl].astype(bf16)
        gb_ref[p] = gb[:, sl].astype(bf16)
        qd_ref[p] = (_rope_lanes(qb[:, sl], dcos, dsf, dsb, DIL_ROT_DIM // 2) * dil_scale).astype(bf16)
        kd_ref[p] = _rope_lanes(kb[:, sl], dcos, dsf, dsb, DIL_ROT_DIM // 2).astype(bf16)
        vd_ref[p] = vb[:, sl].astype(bf16)


def _mla_kernel(q_ref, k_ref, vt_ref, g_ref, o_ref, acc_ref, s_ref):
    f32 = jnp.float32
    tq, tk = MLA_TILE, PROJ_ROWS
    per_q = tq // tk
    assert per_q == 2
    qi = pl.program_id(2)
    q = (q_ref[0], q_ref[1])

    acc_ref[...] = jnp.zeros_like(acc_ref)

    def scores(j, slot):
        ks = pl.ds(pl.multiple_of(j * tk, tk), tk)
        tile_max = []
        for h in range(2):
            s = lax.dot_general(k_ref[h, ks, :], q[h], (((1,), (1,)), ((), ())), preferred_element_type=f32)
            s_ref[slot, h] = s
            tile_max.append(jnp.max(s, axis=0, keepdims=True))
        return tuple(tile_max)

    def absorb(j, slot, tile_max, stats, diag):
        new = []
        for h in range(2):
            m_old, l_old = stats[h]
            s = s_ref[slot, h]
            if diag is None:
                m_tile = tile_max[h]
            else:
                key = lax.broadcasted_iota(jnp.int32, (tk, tq), 0) + diag * tk
                qry = lax.broadcasted_iota(jnp.int32, (tk, tq), 1)
                s = jnp.where(key <= qry, s, NEG)
                m_tile = jnp.max(s, axis=0, keepdims=True)
            m_new = jnp.maximum(m_old, m_tile)
            alpha = jnp.exp2(m_old - m_new)
            p = jnp.exp2(s - m_new)
            l_new = alpha * l_old + jnp.sum(p, axis=0, keepdims=True)
            vt = vt_ref[j, h * MLA_V_DIM:(h + 1) * MLA_V_DIM, :]
            pv = jnp.dot(vt, p.astype(jnp.bfloat16), preferred_element_type=f32)
            acc_ref[h] = alpha * acc_ref[h] + pv
            new.append((m_new, l_new))
        return tuple(new)

    def body(jj, carry):
        max0, stats = carry
        j = per_q * jj
        max1 = scores(j + 1, 1)
        stats = absorb(j, 0, max0, stats, None)
        max0 = scores(j + 2, 0)
        stats = absorb(j + 1, 1, max1, stats, None)
        return max0, stats

    stat = (jnp.full((1, tq), NEG, f32), jnp.zeros((1, tq), f32))
    _, stats = lax.fori_loop(0, qi, body, (scores(0, 0), (stat, stat)))
    j = per_q * qi
    scores(j + 1, 1)
    stats = absorb(j, 0, None, stats, 0)
    (_, l0), (_, l1) = absorb(j + 1, 1, None, stats, 1)

    yt = jnp.concatenate([acc_ref[0] / l0, acc_ref[1] / l1], axis=0)
    o_ref[...] = (yt.T * g_ref[0].astype(f32)).astype(o_ref.dtype)


def _dilated_kernel(q1_ref, k1_ref, v1_ref, q4_ref, k4_ref, v4_ref, q16_ref, k16_ref, v16_ref,
                    g_ref, o_ref, num_ref, max_ref, den_ref):
    f32 = jnp.float32
    r = pl.program_id(2)
    seq = num_ref.shape[1]

    lane = lax.broadcasted_iota(jnp.int32, (BLOCK, LANES), 1)
    low_half = lane < DIL_HEAD_DIM
    qi = lax.broadcasted_iota(jnp.int32, (2 * BLOCK, 2 * BLOCK), 0) % BLOCK
    kj = lax.broadcasted_iota(jnp.int32, (2 * BLOCK, 2 * BLOCK), 1)
    dist_first = qi - kj
    dist_later = dist_first + BLOCK

    def branch(idx, dil, q_ref, k_ref, v_ref):
        n_blocks = q_ref.shape[2] // BLOCK
        group = min(n_blocks, DIL_GROUP)

        def key_rows(n):
            return pl.ds(pl.multiple_of(jnp.maximum(n - 1, 0) * BLOCK, BLOCK), 2 * BLOCK)

        def scores(n):
            q = q_ref[0, 0, pl.ds(pl.multiple_of(n * BLOCK, BLOCK), BLOCK), :]
            k = k_ref[0, 0, key_rows(n), :]
            zero = jnp.zeros_like(q)
            q2 = jnp.concatenate([jnp.where(low_half, q, zero), jnp.where(low_half, zero, q)], axis=0)
            return lax.dot_general(q2, k, (((1,), (1,)), ((), ())), preferred_element_type=f32)

        def softmax(n, s):
            dist = jnp.where(n == 0, dist_first, dist_later)
            s = jnp.where((dist >= 0) & (dist <= BLOCK), s, NEG)
            m = jnp.max(s, axis=1, keepdims=True)
            p = jnp.exp2(s - m)
            return p.astype(jnp.bfloat16), m, jnp.sum(p, axis=1, keepdims=True)

        def emit(n, p, m, den):
            num = jnp.dot(p, v_ref[0, 0, key_rows(n), :], preferred_element_type=f32)
            if dil == 1:
                rows = pl.ds(pl.multiple_of(n * BLOCK, BLOCK), BLOCK)
            else:
                rows = pl.ds(r + n * (BLOCK * dil), BLOCK, stride=dil)
            num_ref[idx, rows, :] = jnp.where(low_half, num[:BLOCK], num[BLOCK:])
            max_ref[idx, rows, :] = jnp.where(low_half, m[:BLOCK], m[BLOCK:])
            den_ref[idx, rows, :] = jnp.where(low_half, den[:BLOCK], den[BLOCK:])

        def trip(g, carry):
            blocks = [g * group + i for i in range(group)]
            stage = [scores(n) for n in blocks]
            stage = [softmax(n, s) for n, s in zip(blocks, stage)]
            for n, (p, m, den) in zip(blocks, stage):
                emit(n, p, m, den)
            return carry

        lax.fori_loop(0, n_blocks // group, trip, 0)

    @pl.when(r == 0)
    def _():
        branch(0, DILATIONS[0], q1_ref, k1_ref, v1_ref)

    @pl.when(r < DILATIONS[1])
    def _():
        branch(1, DILATIONS[1], q4_ref, k4_ref, v4_ref)

    branch(2, DILATIONS[2], q16_ref, k16_ref, v16_ref)

    @pl.when(r == DILATIONS[2] - 1)
    def _():
        def merge(c, carry):
            rows = pl.ds(pl.multiple_of(c * COMBINE_ROWS, COMBINE_ROWS), COMBINE_ROWS)
            ms = [max_ref[i, rows, :] for i in range(3)]
            m_all = jnp.maximum(jnp.maximum(ms[0], ms[1]), ms[2])
            num = jnp.zeros((COMBINE_ROWS, LANES), f32)
            den = jnp.zeros((COMBINE_ROWS, LANES), f32)
            for i in range(3):
                w = jnp.exp2(ms[i] - m_all)
                num = num + w * num_ref[i, rows, :]
                den = den + w * den_ref[i, rows, :]
            o_ref[rows, :] = (num / den * g_ref[0, rows, :].astype(f32)).astype(o_ref.dtype)
            return carry

        lax.fori_loop(0, seq // COMBINE_ROWS, merge, 0)


def _out_kernel(x_ref, ya_ref, yb_ref, wa_ref, wb_ref, g_ref, b_ref, o_ref):
    f32 = jnp.float32
    h = DEEPNORM_ALPHA * x_ref[...]
    h = h + jnp.dot(ya_ref[...], wa_ref[...], preferred_element_type=f32)
    h = h + jnp.dot(yb_ref[...], wb_ref[...], preferred_element_type=f32)
    mu = jnp.mean(h, axis=-1, keepdims=True)
    c = h - mu
    var = jnp.mean(c * c, axis=-1, keepdims=True)
    o_ref[...] = c * lax.rsqrt(var + LN_EPS) * g_ref[...] + b_ref[...]


def _rope_tables(seq, rot_dim, period, rot_offset, pass_rest):
    half = rot_dim // 2
    inv_freq = ROPE_THETA ** (-jnp.arange(0, rot_dim, 2, dtype=jnp.float32) / rot_dim)
    ang = jnp.arange(seq, dtype=jnp.float32)[:, None] * inv_freq[None, :]
    cos, sin = jnp.cos(ang), jnp.sin(ang)
    zeros = jnp.zeros((seq, half), jnp.float32)
    rest = period - rot_offset - rot_dim
    fill = jnp.ones if pass_rest else jnp.zeros
    group_cos = jnp.concatenate([jnp.ones((seq, rot_offset), jnp.float32), cos, cos,
                                 fill((seq, rest), jnp.float32)], axis=1)
    group_fwd = jnp.concatenate([jnp.zeros((seq, rot_offset), jnp.float32), zeros, sin,
                                 jnp.zeros((seq, rest), jnp.float32)], axis=1)
    group_bwd = jnp.concatenate([jnp.zeros((seq, rot_offset), jnp.float32), -sin, zeros,
                                 jnp.zeros((seq, rest), jnp.float32)], axis=1)
    reps = LANES // period
    return tuple(jnp.tile(t, (1, reps)) for t in (group_cos, group_fwd, group_bwd))


def _params(*semantics):
    return pltpu.CompilerParams(dimension_semantics=semantics, vmem_limit_bytes=VMEM_LIMIT_BYTES)


def kernel(x, w_in, q_norm_g, kv_norm_g, w_uq, w_ukv, w_out, ln_g, ln_b):
    f32, bf16 = jnp.float32, jnp.bfloat16
    batch, seq, _ = x.shape
    rows = batch * seq
    x2 = x.reshape(rows, D_MODEL)

    offs = (0, Q_LORA_RANK, Q_LORA_RANK + KV_LORA_RANK, Q_LORA_RANK + KV_LORA_RANK + MLA_ROPE_DIM)
    kpe_w = jnp.zeros((D_MODEL, LANES), f32).at[:, MLA_NOPE_DIM:MLA_NOPE_DIM + MLA_ROPE_DIM].set(
        w_in[:, offs[2]:offs[3]])
    w_big = jnp.concatenate([w_in[:, :offs[2]], kpe_w, w_in[:, offs[3]:]], axis=1).astype(bf16)
    assert w_big.shape[1] == _W_BIG
    dk = MLA_NOPE_DIM + MLA_ROPE_DIM
    wuq = jnp.pad(w_uq.reshape(Q_LORA_RANK, MLA_HEADS, dk), ((0, 0), (0, 0), (0, LANES - dk)))
    wuq = wuq.reshape(Q_LORA_RANK, MLA_HEADS * LANES).astype(bf16)
    wukv = w_ukv.reshape(KV_LORA_RANK, MLA_HEADS, MLA_NOPE_DIM + MLA_V_DIM)
    wuk = jnp.pad(wukv[:, :, :MLA_NOPE_DIM], ((0, 0), (0, 0), (0, LANES - MLA_NOPE_DIM)))
    wuk = wuk.reshape(KV_LORA_RANK, MLA_HEADS * LANES).astype(bf16)
    wuvt = wukv[:, :, MLA_NOPE_DIM:].reshape(KV_LORA_RANK, MLA_WIDTH).T.astype(bf16)
    wa = w_out[:MLA_WIDTH].astype(bf16)
    wb = w_out[MLA_WIDTH:].astype(bf16)

    mla_tabs = _rope_tables(seq, MLA_ROPE_DIM, LANES, MLA_NOPE_DIM, pass_rest=False)
    dil_tabs = _rope_tables(seq, DIL_ROT_DIM, DIL_HEAD_DIM, 0, pass_rest=True)

    tm = PROJ_ROWS
    seq_tiles = seq // tm
    full = lambda shape: pl.BlockSpec(shape, lambda i: (0,) * len(shape))
    tab = pl.BlockSpec((tm, LANES), lambda i: (i % seq_tiles, 0))
    slab = lambda n: pl.BlockSpec((n, tm, LANES), lambda i: (0, i, 0))
    slab_shape = lambda n: jax.ShapeDtypeStruct((n, rows, LANES), bf16)
    vt_spec = pl.BlockSpec((1, MLA_WIDTH, tm), lambda i: (i, 0, 0))
    vt_shape = jax.ShapeDtypeStruct((rows // tm, MLA_WIDTH, tm), bf16)
    qm, km, vt, ga, qd, kd, vd, gb = pl.pallas_call(
        _proj_kernel,
        grid=(rows // tm,),
        in_specs=[pl.BlockSpec((tm, D_MODEL), lambda i: (i, 0)),
                  full((D_MODEL, _W_BIG)), full(wuq.shape), full(wuk.shape), full(wuvt.shape),
                  full((1, Q_LORA_RANK)), full((1, KV_LORA_RANK)),
                  tab, tab, tab, tab, tab, tab],
        out_specs=[slab(MLA_HEADS), slab(MLA_HEADS), vt_spec, slab(HEAD_PAIRS),
                   slab(HEAD_PAIRS), slab(HEAD_PAIRS), slab(HEAD_PAIRS), slab(HEAD_PAIRS)],
        out_shape=[slab_shape(MLA_HEADS), slab_shape(MLA_HEADS), vt_shape, slab_shape(HEAD_PAIRS),
                   slab_shape(HEAD_PAIRS), slab_shape(HEAD_PAIRS), slab_shape(HEAD_PAIRS), slab_shape(HEAD_PAIRS)],
        compiler_params=_params("parallel"),
        name="proj",
    )(x2, w_big, wuq, wuk, wuvt, q_norm_g.reshape(1, -1), kv_norm_g.reshape(1, -1), *mla_tabs, *dil_tabs)

    t = MLA_TILE
    nq = seq // t
    ya = pl.pallas_call(
        _mla_kernel,
        grid=(batch, HEAD_PAIRS, nq),
        in_specs=[pl.BlockSpec((2, t, LANES), lambda b, p, i: (p, b * nq + i, 0)),
                  pl.BlockSpec((2, seq, LANES), lambda b, p, i: (p, b, 0)),
                  pl.BlockSpec((seq // tm, LANES, tm), lambda b, p, i: (b, p, 0)),
                  pl.BlockSpec((1, t, LANES), lambda b, p, i: (p, b * nq + i, 0))],
        out_specs=pl.BlockSpec((t, LANES), lambda b, p, i: (b * nq + i, p)),
        out_shape=jax.ShapeDtypeStruct((rows, MLA_WIDTH), bf16),
        scratch_shapes=[pltpu.VMEM((2, MLA_V_DIM, t), f32), pltpu.VMEM((2, 2, tm, t), f32)],
        compiler_params=_params("parallel", "parallel", "arbitrary"),
        name="mla",
    )(qm, km, vt, ga)

    def strided_view(a, dil):
        return a.reshape(HEAD_PAIRS, batch, seq // dil, dil * LANES)

    def strided_spec(dil):
        return pl.BlockSpec((1, 1, seq // dil, LANES),
                            lambda b, p, r: (p, b, 0, jnp.minimum(r, dil - 1)))

    dil_in, dil_specs = [], []
    for dil in DILATIONS:
        for a in (qd, kd, vd):
            dil_in.append(strided_view(a, dil))
            dil_specs.append(strided_spec(dil))
    yb = pl.pallas_call(
        _dilated_kernel,
        grid=(batch, HEAD_PAIRS, DILATIONS[-1]),
        in_specs=dil_specs + [pl.BlockSpec((1, seq, LANES), lambda b, p, r: (p, b, 0))],
        out_specs=pl.BlockSpec((seq, LANES), lambda b, p, r: (b, p)),
        out_shape=jax.ShapeDtypeStruct((rows, DIL_WIDTH), bf16),
        scratch_shapes=[pltpu.VMEM((3, seq, LANES), f32)] * 3,
        compiler_params=_params("parallel", "parallel", "arbitrary"),
        name="dilated",
    )(*dil_in, gb)

    to = OUT_ROWS
    const = lambda shape: pl.BlockSpec(shape, lambda i: (0,) * len(shape))
    out = pl.pallas_call(
        _out_kernel,
        grid=(rows // to,),
        in_specs=[pl.BlockSpec((to, D_MODEL), lambda i: (i, 0)),
                  pl.BlockSpec((to, MLA_WIDTH), lambda i: (i, 0)),
                  pl.BlockSpec((to, DIL_WIDTH), lambda i: (i, 0)),
                  const((MLA_WIDTH, D_MODEL)), const((DIL_WIDTH, D_MODEL)),
                  const((1, D_MODEL)), const((1, D_MODEL))],
        out_specs=pl.BlockSpec((to, D_MODEL), lambda i: (i, 0)),
        out_shape=jax.ShapeDtypeStruct((rows, D_MODEL), f32),
        compiler_params=_params("parallel"),
        name="out",
    )(x2, ya, yb, wa, wb, ln_g.reshape(1, -1), ln_b.reshape(1, -1))
    return out.reshape(batch, seq, D_MODEL)
```

```python
import functools

import jax
import jax.numpy as jnp
from jax import lax
from jax.experimental import pallas as pl
from jax.experimental.pallas import tpu as pltpu

D_MODEL = 1024
ROPE_THETA = 500000.0
BLOCK = 128
NEG = -1e30
RMS_EPS = 1e-6
LN_EPS = 1e-5

MLA_HEADS = 8
MLA_NOPE_DIM = 64
MLA_ROPE_DIM = 32
MLA_V_DIM = 64
Q_LORA_RANK = 384
KV_LORA_RANK = 256
MLA_WIDTH = MLA_HEADS * MLA_V_DIM

DIL_HEADS = 8
DIL_HEAD_DIM = 64
DIL_ROT_DIM = DIL_HEAD_DIM // 4
DIL_WIDTH = DIL_HEADS * DIL_HEAD_DIM
DILATIONS = (1, 4, 16)

DEPTH = 1
DEEPNORM_ALPHA = (2.0 * DEPTH) ** 0.25
LOG2_E = 1.4426950408889634

LANES = 128
HEAD_PAIRS = MLA_HEADS // 2
VMEM_LIMIT_BYTES = 56 * 1024 * 1024

_OFF_CQ = 0
_OFF_CKV = _OFF_CQ + Q_LORA_RANK
_OFF_KPE = _OFF_CKV + KV_LORA_RANK
_OFF_GA = _OFF_KPE + LANES
_OFF_QB = _OFF_GA + MLA_WIDTH
_OFF_KB = _OFF_QB + DIL_WIDTH
_OFF_VB = _OFF_KB + DIL_WIDTH
_OFF_GB = _OFF_VB + DIL_WIDTH
_W_BIG = _OFF_GB + DIL_WIDTH

PROJ_ROWS = 256
MLA_TILE = 512
OUT_ROWS = 512
COMBINE_ROWS = 256
DIL_GROUP = 4


def _rope_lanes(x, cos, sin_fwd, sin_bwd, half):
    fwd = pltpu.roll(x, half, 1)
    bwd = pltpu.roll(x, LANES - half, 1)
    return x * cos + fwd * sin_fwd + bwd * sin_bwd


def _proj_kernel(x_ref, w_ref, wuq_ref, wuk_ref, wuvt_ref, qg_ref, kvg_ref,
                 mcos_ref, msf_ref, msb_ref, dcos_ref, dsf_ref, dsb_ref,
                 qm_ref, km_ref, vt_ref, ga_ref, gb_ref, *dil_and_scratch):
    f32 = jnp.float32
    bf16 = jnp.bfloat16
    xb = x_ref[...].astype(bf16)
    dil_refs = [dil_and_scratch[3 * i:3 * i + 3] for i in range(len(DILATIONS))]
    stage_ref = dil_and_scratch[-1]
    tm = x_ref.shape[0]

    def emit_residues(which, p, val):
        slab = which * HEAD_PAIRS + p
        stage_ref[slab] = val
        for refs, dil in zip(dil_refs, DILATIONS):
            for r in range(dil):
                rows = val if dil == 1 else stage_ref[slab, pl.ds(r, tm // dil, stride=dil), :]
                refs[which][p, 0, r] = rows.astype(bf16)

    def seg(lo, width):
        return jnp.dot(xb, w_ref[:, lo:lo + width], preferred_element_type=f32)

    def rms(t, g):
        return t * lax.rsqrt(jnp.mean(t * t, axis=-1, keepdims=True) + RMS_EPS) * g

    mcos, msf, msb = mcos_ref[...], msf_ref[...], msb_ref[...]
    dcos, dsf, dsb = dcos_ref[...], dsf_ref[...], dsb_ref[...]
    mla_scale = (MLA_NOPE_DIM + MLA_ROPE_DIM) ** -0.5 * LOG2_E
    dil_scale = DIL_HEAD_DIM ** -0.5 * LOG2_E

    cq = rms(seg(_OFF_CQ, Q_LORA_RANK), qg_ref[...]).astype(bf16)
    qf = jnp.dot(cq, wuq_ref[...], preferred_element_type=f32)
    for h in range(MLA_HEADS):
        blk = qf[:, h * LANES:(h + 1) * LANES]
        qm_ref[h] = (_rope_lanes(blk, mcos, msf, msb, MLA_ROPE_DIM // 2) * mla_scale).astype(bf16)

    ckv = rms(seg(_OFF_CKV, KV_LORA_RANK), kvg_ref[...]).astype(bf16)
    kf = jnp.dot(ckv, wuk_ref[...], preferred_element_type=f32)
    kpe = _rope_lanes(seg(_OFF_KPE, LANES), mcos, msf, msb, MLA_ROPE_DIM // 2)
    for h in range(MLA_HEADS):
        km_ref[h] = (kf[:, h * LANES:(h + 1) * LANES] + kpe).astype(bf16)
    vt = lax.dot_general(wuvt_ref[...], ckv, (((1,), (1,)), ((), ())), preferred_element_type=f32)
    vt_ref[0] = vt.astype(bf16)

    ga = jax.nn.silu(seg(_OFF_GA, MLA_WIDTH))
    gb = jax.nn.silu(seg(_OFF_GB, DIL_WIDTH))
    qb = seg(_OFF_QB, DIL_WIDTH)
    kb = seg(_OFF_KB, DIL_WIDTH)
    vb = seg(_OFF_VB, DIL_WIDTH)
    for p in range(HEAD_PAIRS):
        sl = slice(p * LANES, (p + 1) * LANES)
        ga_ref[p] = ga[:, sl].astype(bf16)
        gb_ref[p] = gb[:, sl].astype(bf16)
        emit_residues(0, p, _rope_lanes(qb[:, sl], dcos, dsf, dsb, DIL_ROT_DIM // 2) * dil_scale)
        emit_residues(1, p, _rope_lanes(kb[:, sl], dcos, dsf, dsb, DIL_ROT_DIM // 2))
        emit_residues(2, p, vb[:, sl])


def _mla_kernel(q_ref, k_ref, vt_ref, g_ref, o_ref, acc_ref, s_ref):
    f32 = jnp.float32
    tq, tk = MLA_TILE, PROJ_ROWS
    per_q = tq // tk
    assert per_q == 2
    qi = pl.program_id(2)
    q = (q_ref[0], q_ref[1])

    acc_ref[...] = jnp.zeros_like(acc_ref)

    def scores(j, slot):
        ks = pl.ds(pl.multiple_of(j * tk, tk), tk)
        tile_max = []
        for h in range(2):
            s = lax.dot_general(k_ref[h, ks, :], q[h], (((1,), (1,)), ((), ())), preferred_element_type=f32)
            s_ref[slot, h] = s
            tile_max.append(jnp.max(s, axis=0, keepdims=True))
        return tuple(tile_max)

    def absorb(j, slot, tile_max, stats, diag):
        new = []
        for h in range(2):
            m_old, l_old = stats[h]
            s = s_ref[slot, h]
            if diag is None:
                m_tile = tile_max[h]
            else:
                key = lax.broadcasted_iota(jnp.int32, (tk, tq), 0) + diag * tk
                qry = lax.broadcasted_iota(jnp.int32, (tk, tq), 1)
                s = jnp.where(key <= qry, s, NEG)
                m_tile = jnp.max(s, axis=0, keepdims=True)
            m_new = jnp.maximum(m_old, m_tile)
            alpha = jnp.exp2(m_old - m_new)
            p = jnp.exp2(s - m_new)
            l_new = alpha * l_old + jnp.sum(p, axis=0, keepdims=True)
            vt = vt_ref[j, h * MLA_V_DIM:(h + 1) * MLA_V_DIM, :]
            pv = jnp.dot(vt, p.astype(jnp.bfloat16), preferred_element_type=f32)
            acc_ref[h] = alpha * acc_ref[h] + pv
            new.append((m_new, l_new))
        return tuple(new)

    def body(jj, carry):
        max0, stats = carry
        j = per_q * jj
        max1 = scores(j + 1, 1)
        stats = absorb(j, 0, max0, stats, None)
        max0 = scores(j + 2, 0)
        stats = absorb(j + 1, 1, max1, stats, None)
        return max0, stats

    stat = (jnp.full((1, tq), NEG, f32), jnp.zeros((1, tq), f32))
    _, stats = lax.fori_loop(0, qi, body, (scores(0, 0), (stat, stat)))
    j = per_q * qi
    scores(j + 1, 1)
    stats = absorb(j, 0, None, stats, 0)
    (_, l0), (_, l1) = absorb(j + 1, 1, None, stats, 1)

    yt = jnp.concatenate([acc_ref[0] / l0, acc_ref[1] / l1], axis=0)
    o_ref[...] = (yt.T * g_ref[0].astype(f32)).astype(o_ref.dtype)


def _dilated_kernel(q1_ref, k1_ref, v1_ref, q4_ref, k4_ref, v4_ref, q16_ref, k16_ref, v16_ref,
                    g_ref, o_ref, num_ref, max_ref, den_ref, s_ref, p_ref, bias_ref):
    f32 = jnp.float32
    seq = num_ref.shape[1]

    lane = lax.broadcasted_iota(jnp.int32, (BLOCK, LANES), 1)
    low_half = lane < DIL_HEAD_DIM
    qi = lax.broadcasted_iota(jnp.int32, (2 * BLOCK, 2 * BLOCK), 0) % BLOCK
    kj = lax.broadcasted_iota(jnp.int32, (2 * BLOCK, 2 * BLOCK), 1)
    dist_first = qi - kj
    dist_later = dist_first + BLOCK
    for kind, dist in enumerate((dist_first, dist_later)):
        bias_ref[kind] = jnp.where((dist >= 0) & (dist <= BLOCK), 0.0, NEG).astype(f32)

    n_groups = seq // BLOCK // DIL_GROUP
    assert n_groups % 2 == 0 and n_groups >= 4

    def branch(idx, dil, q_ref, k_ref, v_ref):
        per_residue = q_ref.shape[3] // BLOCK

        def locate(b):
            return b // per_residue, b % per_residue

        def key_rows(n):
            return pl.ds(pl.multiple_of(jnp.maximum(n - 1, 0) * BLOCK, BLOCK), 2 * BLOCK)

        def out_rows(r, n):
            if dil == 1:
                return pl.ds(pl.multiple_of(n * BLOCK, BLOCK), BLOCK)
            return pl.ds(r + n * (BLOCK * dil), BLOCK, stride=dil)

        def scores(g, slot):
            for i in range(DIL_GROUP):
                r, n = locate(g * DIL_GROUP + i)
                q = q_ref[0, 0, r, pl.ds(pl.multiple_of(n * BLOCK, BLOCK), BLOCK), :]
                k = k_ref[0, 0, r, key_rows(n), :]
                zero = jnp.zeros_like(q)
                q2 = jnp.concatenate([jnp.where(low_half, q, zero), jnp.where(low_half, zero, q)], axis=0)
                s_ref[slot, i] = lax.dot_general(q2, k, (((1,), (1,)), ((), ())), preferred_element_type=f32)

        def softmax(g, slot):
            for i in range(DIL_GROUP):
                r, n = locate(g * DIL_GROUP + i)
                s = s_ref[slot, i] + bias_ref[jnp.minimum(n, 1)]
                m = jnp.max(s, axis=1, keepdims=True)
                p = jnp.exp2(s - m)
                den = jnp.sum(p, axis=1, keepdims=True)
                p_ref[slot, i] = p.astype(jnp.bfloat16)
                rows = out_rows(r, n)
                max_ref[idx, rows, :] = jnp.where(low_half, m[:BLOCK], m[BLOCK:])
                den_ref[idx, rows, :] = jnp.where(low_half, den[:BLOCK], den[BLOCK:])

        def values(g, slot):
            for i in range(DIL_GROUP):
                r, n = locate(g * DIL_GROUP + i)
                num = jnp.dot(p_ref[slot, i], v_ref[0, 0, r, key_rows(n), :], preferred_element_type=f32)
                num_ref[idx, out_rows(r, n), :] = jnp.where(low_half, num[:BLOCK], num[BLOCK:])

        return scores, softmax, values

    stages = [branch(0, DILATIONS[0], q1_ref, k1_ref, v1_ref),
              branch(1, DILATIONS[1], q4_ref, k4_ref, v4_ref),
              branch(2, DILATIONS[2], q16_ref, k16_ref, v16_ref)]
    last = n_groups - 1

    for b, (scores, softmax, values) in enumerate(stages):
        if b == 0:
            scores(0, 0)
            scores(1, 1)
            softmax(0, 0)
        else:
            prev_values = stages[b - 1][2]
            prev_values(last, 1)
            scores(1, 1)
            softmax(0, 0)

        def two_trips(t, carry, scores=scores, softmax=softmax, values=values):
            values(2 * t, 0)
            scores(2 * t + 2, 0)
            softmax(2 * t + 1, 1)
            values(2 * t + 1, 1)
            scores(2 * t + 3, 1)
            softmax(2 * t + 2, 0)
            return carry

        lax.fori_loop(0, (n_groups - 2) // 2, two_trips, 0)
        values(last - 1, 0)
        if b + 1 < len(stages):
            stages[b + 1][0](0, 0)
        softmax(last, 1)
    stages[-1][2](last, 1)

    def merge(c, carry):
        rows = pl.ds(pl.multiple_of(c * COMBINE_ROWS, COMBINE_ROWS), COMBINE_ROWS)
        ms = [max_ref[i, rows, :] for i in range(3)]
        m_all = jnp.maximum(jnp.maximum(ms[0], ms[1]), ms[2])
        num = jnp.zeros((COMBINE_ROWS, LANES), f32)
        den = jnp.zeros((COMBINE_ROWS, LANES), f32)
        for i in range(3):
            w = jnp.exp2(ms[i] - m_all)
            num = num + w * num_ref[i, rows, :]
            den = den + w * den_ref[i, rows, :]
        o_ref[rows, :] = (num / den * g_ref[0, rows, :].astype(f32)).astype(o_ref.dtype)
        return carry

    lax.fori_loop(0, seq // COMBINE_ROWS, merge, 0)


def _out_kernel(x_ref, ya_ref, yb_ref, wa_ref, wb_ref, g_ref, b_ref, o_ref):
    f32 = jnp.float32
    h = DEEPNORM_ALPHA * x_ref[...]
    h = h + jnp.dot(ya_ref[...], wa_ref[...], preferred_element_type=f32)
    h = h + jnp.dot(yb_ref[...], wb_ref[...], preferred_element_type=f32)
    mu = jnp.mean(h, axis=-1, keepdims=True)
    c = h - mu
    var = jnp.mean(c * c, axis=-1, keepdims=True)
    o_ref[...] = c * lax.rsqrt(var + LN_EPS) * g_ref[...] + b_ref[...]


def _rope_tables(seq, rot_dim, period, rot_offset, pass_rest):
    half = rot_dim // 2
    inv_freq = ROPE_THETA ** (-jnp.arange(0, rot_dim, 2, dtype=jnp.float32) / rot_dim)
    ang = jnp.arange(seq, dtype=jnp.float32)[:, None] * inv_freq[None, :]
    cos, sin = jnp.cos(ang), jnp.sin(ang)
    zeros = jnp.zeros((seq, half), jnp.float32)
    rest = period - rot_offset - rot_dim
    fill = jnp.ones if pass_rest else jnp.zeros
    group_cos = jnp.concatenate([jnp.ones((seq, rot_offset), jnp.float32), cos, cos,
                                 fill((seq, rest), jnp.float32)], axis=1)
    group_fwd = jnp.concatenate([jnp.zeros((seq, rot_offset), jnp.float32), zeros, sin,
                                 jnp.zeros((seq, rest), jnp.float32)], axis=1)
    group_bwd = jnp.concatenate([jnp.zeros((seq, rot_offset), jnp.float32), -sin, zeros,
                                 jnp.zeros((seq, rest), jnp.float32)], axis=1)
    reps = LANES // period
    return tuple(jnp.tile(t, (1, reps)) for t in (group_cos, group_fwd, group_bwd))


def _params(*semantics):
    return pltpu.CompilerParams(dimension_semantics=semantics, vmem_limit_bytes=VMEM_LIMIT_BYTES)


def kernel(x, w_in, q_norm_g, kv_norm_g, w_uq, w_ukv, w_out, ln_g, ln_b):
    f32, bf16 = jnp.float32, jnp.bfloat16
    batch, seq, _ = x.shape
    rows = batch * seq
    x2 = x.reshape(rows, D_MODEL)

    offs = (0, Q_LORA_RANK, Q_LORA_RANK + KV_LORA_RANK, Q_LORA_RANK + KV_LORA_RANK + MLA_ROPE_DIM)
    kpe_w = jnp.zeros((D_MODEL, LANES), f32).at[:, MLA_NOPE_DIM:MLA_NOPE_DIM + MLA_ROPE_DIM].set(
        w_in[:, offs[2]:offs[3]])
    w_big = jnp.concatenate([w_in[:, :offs[2]], kpe_w, w_in[:, offs[3]:]], axis=1).astype(bf16)
    assert w_big.shape[1] == _W_BIG
    dk = MLA_NOPE_DIM + MLA_ROPE_DIM
    wuq = jnp.pad(w_uq.reshape(Q_LORA_RANK, MLA_HEADS, dk), ((0, 0), (0, 0), (0, LANES - dk)))
    wuq = wuq.reshape(Q_LORA_RANK, MLA_HEADS * LANES).astype(bf16)
    wukv = w_ukv.reshape(KV_LORA_RANK, MLA_HEADS, MLA_NOPE_DIM + MLA_V_DIM)
    wuk = jnp.pad(wukv[:, :, :MLA_NOPE_DIM], ((0, 0), (0, 0), (0, LANES - MLA_NOPE_DIM)))
    wuk = wuk.reshape(KV_LORA_RANK, MLA_HEADS * LANES).astype(bf16)
    wuvt = wukv[:, :, MLA_NOPE_DIM:].reshape(KV_LORA_RANK, MLA_WIDTH).T.astype(bf16)
    wa = w_out[:MLA_WIDTH].astype(bf16)
    wb = w_out[MLA_WIDTH:].astype(bf16)

    mla_tabs = _rope_tables(seq, MLA_ROPE_DIM, LANES, MLA_NOPE_DIM, pass_rest=False)
    dil_tabs = _rope_tables(seq, DIL_ROT_DIM, DIL_HEAD_DIM, 0, pass_rest=True)

    tm = PROJ_ROWS
    seq_tiles = seq // tm
    full = lambda shape: pl.BlockSpec(shape, lambda i: (0,) * len(shape))
    tab = pl.BlockSpec((tm, LANES), lambda i: (i % seq_tiles, 0))
    slab = lambda n: pl.BlockSpec((n, tm, LANES), lambda i: (0, i, 0))
    slab_shape = lambda n: jax.ShapeDtypeStruct((n, rows, LANES), bf16)
    vt_spec = pl.BlockSpec((1, MLA_WIDTH, tm), lambda i: (i, 0, 0))
    vt_shape = jax.ShapeDtypeStruct((rows // tm, MLA_WIDTH, tm), bf16)
    res_specs, res_shapes = [], []
    for dil in DILATIONS:
        spec = pl.BlockSpec((HEAD_PAIRS, 1, dil, tm // dil, LANES),
                            lambda i: (0, i // seq_tiles, 0, i % seq_tiles, 0))
        shape = jax.ShapeDtypeStruct((HEAD_PAIRS, batch, dil, seq // dil, LANES), bf16)
        res_specs += [spec] * 3
        res_shapes += [shape] * 3
    qm, km, vt, ga, gb, *dil_in = pl.pallas_call(
        _proj_kernel,
        grid=(rows // tm,),
        in_specs=[pl.BlockSpec((tm, D_MODEL), lambda i: (i, 0)),
                  full((D_MODEL, _W_BIG)), full(wuq.shape), full(wuk.shape), full(wuvt.shape),
                  full((1, Q_LORA_RANK)), full((1, KV_LORA_RANK)),
                  tab, tab, tab, tab, tab, tab],
        out_specs=[slab(MLA_HEADS), slab(MLA_HEADS), vt_spec, slab(HEAD_PAIRS), slab(HEAD_PAIRS)] + res_specs,
        out_shape=[slab_shape(MLA_HEADS), slab_shape(MLA_HEADS), vt_shape, slab_shape(HEAD_PAIRS),
                   slab_shape(HEAD_PAIRS)] + res_shapes,
        scratch_shapes=[pltpu.VMEM((3 * HEAD_PAIRS, tm, LANES), f32)],
        compiler_params=_params("parallel"),
        name="proj",
    )(x2, w_big, wuq, wuk, wuvt, q_norm_g.reshape(1, -1), kv_norm_g.reshape(1, -1), *mla_tabs, *dil_tabs)

    t = MLA_TILE
    nq = seq // t
    ya = pl.pallas_call(
        _mla_kernel,
        grid=(batch, HEAD_PAIRS, nq),
        in_specs=[pl.BlockSpec((2, t, LANES), lambda b, p, i: (p, b * nq + i, 0)),
                  pl.BlockSpec((2, seq, LANES), lambda b, p, i: (p, b, 0)),
                  pl.BlockSpec((seq // tm, LANES, tm), lambda b, p, i: (b, p, 0)),
                  pl.BlockSpec((1, t, LANES), lambda b, p, i: (p, b * nq + i, 0))],
        out_specs=pl.BlockSpec((t, LANES), lambda b, p, i: (b * nq + i, p)),
        out_shape=jax.ShapeDtypeStruct((rows, MLA_WIDTH), bf16),
        scratch_shapes=[pltpu.VMEM((2, MLA_V_DIM, t), f32), pltpu.VMEM((2, 2, tm, t), f32)],
        compiler_params=_params("parallel", "parallel", "arbitrary"),
        name="mla",
    )(qm, km, vt, ga)

    dil_specs = [pl.BlockSpec((1, 1, dil, seq // dil, LANES), lambda b, p: (p, b, 0, 0, 0))
                 for dil in DILATIONS for _ in range(3)]
    yb = pl.pallas_call(
        _dilated_kernel,
        grid=(batch, HEAD_PAIRS),
        in_specs=dil_specs + [pl.BlockSpec((1, seq, LANES), lambda b, p: (p, b, 0))],
        out_specs=pl.BlockSpec((seq, LANES), lambda b, p: (b, p)),
        out_shape=jax.ShapeDtypeStruct((rows, DIL_WIDTH), bf16),
        scratch_shapes=[pltpu.VMEM((3, seq, LANES), f32)] * 3 + [
            pltpu.VMEM((2, DIL_GROUP, 2 * BLOCK, 2 * BLOCK), f32),
            pltpu.VMEM((2, DIL_GROUP, 2 * BLOCK, 2 * BLOCK), bf16),
            pltpu.VMEM((2, 2 * BLOCK, 2 * BLOCK), f32)],
        compiler_params=_params("parallel", "parallel"),
        name="dilated",
    )(*dil_in, gb)

    to = OUT_ROWS
    const = lambda shape: pl.BlockSpec(shape, lambda i: (0,) * len(shape))
    out = pl.pallas_call(
        _out_kernel,
        grid=(rows // to,),
        in_specs=[pl.BlockSpec((to, D_MODEL), lambda i: (i, 0)),
                  pl.BlockSpec((to, MLA_WIDTH), lambda i: (i, 0)),
                  pl.BlockSpec((to, DIL_WIDTH), lambda i: (i, 0)),
                  const((MLA_WIDTH, D_MODEL)), const((DIL_WIDTH, D_MODEL)),
                  const((1, D_MODEL)), const((1, D_MODEL))],
        out_specs=pl.BlockSpec((to, D_MODEL), lambda i: (i, 0)),
        out_shape=jax.ShapeDtypeStruct((rows, D_MODEL), f32),
        compiler_params=_params("parallel"),
        name="out",
    )(x2, ya, yb, wa, wb, ln_g.reshape(1, -1), ln_b.reshape(1, -1))
    return out.reshape(batch, seq, D_MODEL)
```

```python
import functools

import jax
import jax.numpy as jnp
from jax import lax
from jax.experimental import pallas as pl
from jax.experimental.pallas import tpu as pltpu

D_MODEL = 1024
ROPE_THETA = 500000.0
BLOCK = 128
NEG = -1e30
RMS_EPS = 1e-6
LN_EPS = 1e-5

MLA_HEADS = 8
MLA_NOPE_DIM = 64
MLA_ROPE_DIM = 32
MLA_V_DIM = 64
Q_LORA_RANK = 384
KV_LORA_RANK = 256
MLA_WIDTH = MLA_HEADS * MLA_V_DIM

DIL_HEADS = 8
DIL_HEAD_DIM = 64
DIL_ROT_DIM = DIL_HEAD_DIM // 4
DIL_WIDTH = DIL_HEADS * DIL_HEAD_DIM
DILATIONS = (1, 4, 16)

DEPTH = 1
DEEPNORM_ALPHA = (2.0 * DEPTH) ** 0.25
LOG2_E = 1.4426950408889634

LANES = 128
HEAD_PAIRS = MLA_HEADS // 2
VMEM_LIMIT_BYTES = 56 * 1024 * 1024

_OFF_CQ = 0
_OFF_CKV = _OFF_CQ + Q_LORA_RANK
_OFF_KPE = _OFF_CKV + KV_LORA_RANK
_OFF_GA = _OFF_KPE + LANES
_OFF_QB = _OFF_GA + MLA_WIDTH
_OFF_KB = _OFF_QB + DIL_WIDTH
_OFF_VB = _OFF_KB + DIL_WIDTH
_OFF_GB = _OFF_VB + DIL_WIDTH
_W_BIG = _OFF_GB + DIL_WIDTH

PROJ_ROWS = 256
MLA_TILE = 512
MLA_ONES_ROWS = 16
OUT_ROWS = 512
COMBINE_ROWS = 256
DIL_GROUP = 4


def _rope_lanes(x, cos, sin_fwd, sin_bwd, half):
    fwd = pltpu.roll(x, half, 1)
    bwd = pltpu.roll(x, LANES - half, 1)
    return x * cos + fwd * sin_fwd + bwd * sin_bwd


def _proj_kernel(x_ref, w_ref, wuq_ref, wuk_ref, wuvt_ref, qg_ref, kvg_ref,
                 mcos_ref, msf_ref, msb_ref, dcos_ref, dsf_ref, dsb_ref,
                 qm_ref, km_ref, vt_ref, ga_ref, gb_ref, *dil_and_scratch):
    f32 = jnp.float32
    bf16 = jnp.bfloat16
    xb = x_ref[...].astype(bf16)
    dil_refs = [dil_and_scratch[3 * i:3 * i + 3] for i in range(len(DILATIONS))]
    stage_ref = dil_and_scratch[-1]
    tm = x_ref.shape[0]

    def emit_residues(which, p, val):
        slab = which * HEAD_PAIRS + p
        stage_ref[slab] = val
        for refs, dil in zip(dil_refs, DILATIONS):
            for r in range(dil):
                rows = val if dil == 1 else stage_ref[slab, pl.ds(r, tm // dil, stride=dil), :]
                refs[which][p, 0, r] = rows.astype(bf16)

    def seg(lo, width):
        return jnp.dot(xb, w_ref[:, lo:lo + width], preferred_element_type=f32)

    def rms(t, g):
        return t * lax.rsqrt(jnp.mean(t * t, axis=-1, keepdims=True) + RMS_EPS) * g

    mcos, msf, msb = mcos_ref[...], msf_ref[...], msb_ref[...]
    dcos, dsf, dsb = dcos_ref[...], dsf_ref[...], dsb_ref[...]
    mla_scale = (MLA_NOPE_DIM + MLA_ROPE_DIM) ** -0.5 * LOG2_E
    dil_scale = DIL_HEAD_DIM ** -0.5 * LOG2_E

    cq = rms(seg(_OFF_CQ, Q_LORA_RANK), qg_ref[...]).astype(bf16)
    qf = jnp.dot(cq, wuq_ref[...], preferred_element_type=f32)
    for h in range(MLA_HEADS):
        blk = qf[:, h * LANES:(h + 1) * LANES]
        qm_ref[h] = (_rope_lanes(blk, mcos, msf, msb, MLA_ROPE_DIM // 2) * mla_scale).astype(bf16)

    ckv = rms(seg(_OFF_CKV, KV_LORA_RANK), kvg_ref[...]).astype(bf16)
    kf = jnp.dot(ckv, wuk_ref[...], preferred_element_type=f32)
    kpe = _rope_lanes(seg(_OFF_KPE, LANES), mcos, msf, msb, MLA_ROPE_DIM // 2)
    for h in range(MLA_HEADS):
        km_ref[h] = (kf[:, h * LANES:(h + 1) * LANES] + kpe).astype(bf16)
    vt = lax.dot_general(wuvt_ref[...], ckv, (((1,), (1,)), ((), ())), preferred_element_type=f32)
    vt_ref[0] = vt.astype(bf16)

    ga = jax.nn.silu(seg(_OFF_GA, MLA_WIDTH))
    gb = jax.nn.silu(seg(_OFF_GB, DIL_WIDTH))
    qb = seg(_OFF_QB, DIL_WIDTH)
    kb = seg(_OFF_KB, DIL_WIDTH)
    vb = seg(_OFF_VB, DIL_WIDTH)
    for p in range(HEAD_PAIRS):
        sl = slice(p * LANES, (p + 1) * LANES)
        ga_ref[p] = ga[:, sl].astype(bf16)
        gb_ref[p] = gb[:, sl].astype(bf16)
        emit_residues(0, p, _rope_lanes(qb[:, sl], dcos, dsf, dsb, DIL_ROT_DIM // 2) * dil_scale)
        emit_residues(1, p, _rope_lanes(kb[:, sl], dcos, dsf, dsb, DIL_ROT_DIM // 2))
        emit_residues(2, p, vb[:, sl])


def _mla_kernel(q_ref, k_ref, vt_ref, g_ref, o_ref, acc_ref, s_ref):
    f32 = jnp.float32
    bf16 = jnp.bfloat16
    tq, tk = MLA_TILE, PROJ_ROWS
    assert tq == 2 * tk
    nq = q_ref.shape[1] // tq
    ones = jnp.ones((acc_ref.shape[1] - MLA_V_DIM, tk), bf16)

    def q_tile(qi):
        rows = pl.ds(pl.multiple_of(qi * tq, tq), tq)
        return q_ref[0, rows, :], q_ref[1, rows, :]

    def scores(q, j, slot):
        ks = pl.ds(pl.multiple_of(j * tk, tk), tk)
        tile_max = []
        for h in range(2):
            s = lax.dot_general(k_ref[h, ks, :], q[h], (((1,), (1,)), ((), ())), preferred_element_type=f32)
            s_ref[slot, h] = s
            tile_max.append(jnp.max(s, axis=0, keepdims=True))
        return tuple(tile_max)

    def accumulate(h, j, lanes, m_old, m_tile, s):
        m_new = jnp.maximum(m_old, m_tile)
        alpha = jnp.exp2(m_old - m_new)
        p = jnp.exp2(s - m_new).astype(bf16)
        vt = jnp.concatenate([vt_ref[j, h * MLA_V_DIM:(h + 1) * MLA_V_DIM, :], ones], axis=0)
        acc_ref[h, :, lanes] = alpha * acc_ref[h, :, lanes] + jnp.dot(vt, p, preferred_element_type=f32)
        return m_new

    def absorb(j, slot, tile_max, m_run):
        return tuple(accumulate(h, j, slice(None), m_run[h], tile_max[h], s_ref[slot, h]) for h in range(2))

    key = lax.broadcasted_iota(jnp.int32, (tk, tk), 0)
    qry = lax.broadcasted_iota(jnp.int32, (tk, tk), 1)
    lower = key <= qry

    def per_query_tile(qi, max0):
        q = q_tile(qi)
        acc_ref[...] = jnp.zeros_like(acc_ref)

        def body(jj, carry):
            max0, m_run = carry
            j = 2 * jj
            max1 = scores(q, j + 1, 1)
            m_run = absorb(j, 0, max0, m_run)
            max0 = scores(q, j + 2, 0)
            m_run = absorb(j + 1, 1, max1, m_run)
            return max0, m_run

        start = jnp.full((1, tq), NEG, f32)
        _, m_run = lax.fori_loop(0, qi, body, (max0, (start, start)))

        j = 2 * qi
        ks = pl.ds(pl.multiple_of((j + 1) * tk, tk), tk)
        late = [lax.dot_general(k_ref[h, ks, :], q[h][tk:, :], (((1,), (1,)), ((), ())),
                                preferred_element_type=f32) for h in range(2)]
        m_mid = []
        for h in range(2):
            s = s_ref[0, h]
            s = jnp.concatenate([jnp.where(lower, s[:, :tk], NEG), s[:, tk:]], axis=1)
            m_mid.append(accumulate(h, j, slice(None), m_run[h], jnp.max(s, axis=0, keepdims=True), s))
        max0_next = scores(q_tile(jnp.minimum(qi + 1, nq - 1)), 0, 0)
        for h in range(2):
            s = jnp.where(lower, late[h], NEG)
            accumulate(h, j + 1, slice(tk, None), m_mid[h][:, tk:], jnp.max(s, axis=0, keepdims=True), s)

        yt = jnp.concatenate([acc_ref[h, :MLA_V_DIM, :] / acc_ref[h, MLA_V_DIM:MLA_V_DIM + 1, :] for h in range(2)],
                             axis=0)
        rows = pl.ds(pl.multiple_of(qi * tq, tq), tq)
        o_ref[rows, :] = (yt.T * g_ref[0, rows, :].astype(f32)).astype(o_ref.dtype)
        return max0_next

    lax.fori_loop(0, nq, per_query_tile, scores(q_tile(0), 0, 0))


def _dilated_kernel(q1_ref, k1_ref, v1_ref, q4_ref, k4_ref, v4_ref, q16_ref, k16_ref, v16_ref,
                    g_ref, o_ref, num_ref, max_ref, den_ref, s_ref, p_ref, bias_ref):
    f32 = jnp.float32
    seq = num_ref.shape[1]

    lane = lax.broadcasted_iota(jnp.int32, (BLOCK, LANES), 1)
    low_half = lane < DIL_HEAD_DIM
    qi = lax.broadcasted_iota(jnp.int32, (2 * BLOCK, 2 * BLOCK), 0) % BLOCK
    kj = lax.broadcasted_iota(jnp.int32, (2 * BLOCK, 2 * BLOCK), 1)
    dist_first = qi - kj
    dist_later = dist_first + BLOCK
    for kind, dist in enumerate((dist_first, dist_later)):
        bias_ref[kind] = jnp.where((dist >= 0) & (dist <= BLOCK), 0.0, NEG).astype(f32)

    n_groups = seq // BLOCK // DIL_GROUP
    assert n_groups % 2 == 0 and n_groups >= 4

    def branch(idx, dil, q_ref, k_ref, v_ref):
        per_residue = q_ref.shape[3] // BLOCK

        def locate(b):
            return b // per_residue, b % per_residue

        def key_rows(n):
            return pl.ds(pl.multiple_of(jnp.maximum(n - 1, 0) * BLOCK, BLOCK), 2 * BLOCK)

        def out_rows(r, n):
            if dil == 1:
                return pl.ds(pl.multiple_of(n * BLOCK, BLOCK), BLOCK)
            return pl.ds(r + n * (BLOCK * dil), BLOCK, stride=dil)

        def scores(g, slot):
            for i in range(DIL_GROUP):
                r, n = locate(g * DIL_GROUP + i)
                q = q_ref[0, 0, r, pl.ds(pl.multiple_of(n * BLOCK, BLOCK), BLOCK), :]
                k = k_ref[0, 0, r, key_rows(n), :]
                zero = jnp.zeros_like(q)
                q2 = jnp.concatenate([jnp.where(low_half, q, zero), jnp.where(low_half, zero, q)], axis=0)
                s_ref[slot, i] = lax.dot_general(q2, k, (((1,), (1,)), ((), ())), preferred_element_type=f32)

        def softmax(g, slot):
            for i in range(DIL_GROUP):
                r, n = locate(g * DIL_GROUP + i)
                s = s_ref[slot, i] + bias_ref[jnp.minimum(n, 1)]
                m = jnp.max(s, axis=1, keepdims=True)
                p = jnp.exp2(s - m)
                den = jnp.sum(p, axis=1, keepdims=True)
                p_ref[slot, i] = p.astype(jnp.bfloat16)
                rows = out_rows(r, n)
                max_ref[idx, rows, :] = jnp.where(low_half, m[:BLOCK], m[BLOCK:])
                den_ref[idx, rows, :] = jnp.where(low_half, den[:BLOCK], den[BLOCK:])

        def values(g, slot):
            for i in range(DIL_GROUP):
                r, n = locate(g * DIL_GROUP + i)
                num = jnp.dot(p_ref[slot, i], v_ref[0, 0, r, key_rows(n), :], preferred_element_type=f32)
                num_ref[idx, out_rows(r, n), :] = jnp.where(low_half, num[:BLOCK], num[BLOCK:])

        return scores, softmax, values

    stages = [branch(0, DILATIONS[0], q1_ref, k1_ref, v1_ref),
              branch(1, DILATIONS[1], q4_ref, k4_ref, v4_ref),
              branch(2, DILATIONS[2], q16_ref, k16_ref, v16_ref)]
    last = n_groups - 1

    for b, (scores, softmax, values) in enumerate(stages):
        if b == 0:
            scores(0, 0)
            scores(1, 1)
            softmax(0, 0)
        else:
            prev_values = stages[b - 1][2]
            prev_values(last, 1)
            scores(1, 1)
            softmax(0, 0)

        def two_trips(t, carry, scores=scores, softmax=softmax, values=values):
            values(2 * t, 0)
            scores(2 * t + 2, 0)
            softmax(2 * t + 1, 1)
            values(2 * t + 1, 1)
            scores(2 * t + 3, 1)
            softmax(2 * t + 2, 0)
            return carry

        lax.fori_loop(0, (n_groups - 2) // 2, two_trips, 0)
        values(last - 1, 0)
        if b + 1 < len(stages):
            stages[b + 1][0](0, 0)
        softmax(last, 1)
    stages[-1][2](last, 1)

    def merge(c, carry):
        rows = pl.ds(pl.multiple_of(c * COMBINE_ROWS, COMBINE_ROWS), COMBINE_ROWS)
        ms = [max_ref[i, rows, :] for i in range(3)]
        m_all = jnp.maximum(jnp.maximum(ms[0], ms[1]), ms[2])
        num = jnp.zeros((COMBINE_ROWS, LANES), f32)
        den = jnp.zeros((COMBINE_ROWS, LANES), f32)
        for i in range(3):
            w = jnp.exp2(ms[i] - m_all)
            num = num + w * num_ref[i, rows, :]
            den = den + w * den_ref[i, rows, :]
        o_ref[rows, :] = (num / den * g_ref[0, rows, :].astype(f32)).astype(o_ref.dtype)
        return carry

    lax.fori_loop(0, seq // COMBINE_ROWS, merge, 0)


def _out_kernel(x_ref, ya_ref, yb_ref, wa_ref, wb_ref, g_ref, b_ref, o_ref):
    f32 = jnp.float32
    h = DEEPNORM_ALPHA * x_ref[...]
    h = h + jnp.dot(ya_ref[...], wa_ref[...], preferred_element_type=f32)
    h = h + jnp.dot(yb_ref[...], wb_ref[...], preferred_element_type=f32)
    mu = jnp.mean(h, axis=-1, keepdims=True)
    c = h - mu
    var = jnp.mean(c * c, axis=-1, keepdims=True)
    o_ref[...] = c * lax.rsqrt(var + LN_EPS) * g_ref[...] + b_ref[...]


def _rope_tables(seq, rot_dim, period, rot_offset, pass_rest):
    half = rot_dim // 2
    inv_freq = ROPE_THETA ** (-jnp.arange(0, rot_dim, 2, dtype=jnp.float32) / rot_dim)
    ang = jnp.arange(seq, dtype=jnp.float32)[:, None] * inv_freq[None, :]
    cos, sin = jnp.cos(ang), jnp.sin(ang)
    zeros = jnp.zeros((seq, half), jnp.float32)
    rest = period - rot_offset - rot_dim
    fill = jnp.ones if pass_rest else jnp.zeros
    group_cos = jnp.concatenate([jnp.ones((seq, rot_offset), jnp.float32), cos, cos,
                                 fill((seq, rest), jnp.float32)], axis=1)
    group_fwd = jnp.concatenate([jnp.zeros((seq, rot_offset), jnp.float32), zeros, sin,
                                 jnp.zeros((seq, rest), jnp.float32)], axis=1)
    group_bwd = jnp.concatenate([jnp.zeros((seq, rot_offset), jnp.float32), -sin, zeros,
                                 jnp.zeros((seq, rest), jnp.float32)], axis=1)
    reps = LANES // period
    return tuple(jnp.tile(t, (1, reps)) for t in (group_cos, group_fwd, group_bwd))


def _params(*semantics):
    return pltpu.CompilerParams(dimension_semantics=semantics, vmem_limit_bytes=VMEM_LIMIT_BYTES)


def kernel(x, w_in, q_norm_g, kv_norm_g, w_uq, w_ukv, w_out, ln_g, ln_b):
    f32, bf16 = jnp.float32, jnp.bfloat16
    batch, seq, _ = x.shape
    rows = batch * seq
    x2 = x.reshape(rows, D_MODEL)

    offs = (0, Q_LORA_RANK, Q_LORA_RANK + KV_LORA_RANK, Q_LORA_RANK + KV_LORA_RANK + MLA_ROPE_DIM)
    kpe_w = jnp.zeros((D_MODEL, LANES), f32).at[:, MLA_NOPE_DIM:MLA_NOPE_DIM + MLA_ROPE_DIM].set(
        w_in[:, offs[2]:offs[3]])
    w_big = jnp.concatenate([w_in[:, :offs[2]], kpe_w, w_in[:, offs[3]:]], axis=1).astype(bf16)
    assert w_big.shape[1] == _W_BIG
    dk = MLA_NOPE_DIM + MLA_ROPE_DIM
    wuq = jnp.pad(w_uq.reshape(Q_LORA_RANK, MLA_HEADS, dk), ((0, 0), (0, 0), (0, LANES - dk)))
    wuq = wuq.reshape(Q_LORA_RANK, MLA_HEADS * LANES).astype(bf16)
    wukv = w_ukv.reshape(KV_LORA_RANK, MLA_HEADS, MLA_NOPE_DIM + MLA_V_DIM)
    wuk = jnp.pad(wukv[:, :, :MLA_NOPE_DIM], ((0, 0), (0, 0), (0, LANES - MLA_NOPE_DIM)))
    wuk = wuk.reshape(KV_LORA_RANK, MLA_HEADS * LANES).astype(bf16)
    wuvt = wukv[:, :, MLA_NOPE_DIM:].reshape(KV_LORA_RANK, MLA_WIDTH).T.astype(bf16)
    wa = w_out[:MLA_WIDTH].astype(bf16)
    wb = w_out[MLA_WIDTH:].astype(bf16)

    mla_tabs = _rope_tables(seq, MLA_ROPE_DIM, LANES, MLA_NOPE_DIM, pass_rest=False)
    dil_tabs = _rope_tables(seq, DIL_ROT_DIM, DIL_HEAD_DIM, 0, pass_rest=True)

    tm = PROJ_ROWS
    seq_tiles = seq // tm
    full = lambda shape: pl.BlockSpec(shape, lambda i: (0,) * len(shape))
    tab = pl.BlockSpec((tm, LANES), lambda i: (i % seq_tiles, 0))
    slab = lambda n: pl.BlockSpec((n, tm, LANES), lambda i: (0, i, 0))
    slab_shape = lambda n: jax.ShapeDtypeStruct((n, rows, LANES), bf16)
    vt_spec = pl.BlockSpec((1, MLA_WIDTH, tm), lambda i: (i, 0, 0))
    vt_shape = jax.ShapeDtypeStruct((rows // tm, MLA_WIDTH, tm), bf16)
    res_specs, res_shapes = [], []
    for dil in DILATIONS:
        spec = pl.BlockSpec((HEAD_PAIRS, 1, dil, tm // dil, LANES),
                            lambda i: (0, i // seq_tiles, 0, i % seq_tiles, 0))
        shape = jax.ShapeDtypeStruct((HEAD_PAIRS, batch, dil, seq // dil, LANES), bf16)
        res_specs += [spec] * 3
        res_shapes += [shape] * 3
    qm, km, vt, ga, gb, *dil_in = pl.pallas_call(
        _proj_kernel,
        grid=(rows // tm,),
        in_specs=[pl.BlockSpec((tm, D_MODEL), lambda i: (i, 0)),
                  full((D_MODEL, _W_BIG)), full(wuq.shape), full(wuk.shape), full(wuvt.shape),
                  full((1, Q_LORA_RANK)), full((1, KV_LORA_RANK)),
                  tab, tab, tab, tab, tab, tab],
        out_specs=[slab(MLA_HEADS), slab(MLA_HEADS), vt_spec, slab(HEAD_PAIRS), slab(HEAD_PAIRS)] + res_specs,
        out_shape=[slab_shape(MLA_HEADS), slab_shape(MLA_HEADS), vt_shape, slab_shape(HEAD_PAIRS),
                   slab_shape(HEAD_PAIRS)] + res_shapes,
        scratch_shapes=[pltpu.VMEM((3 * HEAD_PAIRS, tm, LANES), f32)],
        compiler_params=_params("parallel"),
        name="proj",
    )(x2, w_big, wuq, wuk, wuvt, q_norm_g.reshape(1, -1), kv_norm_g.reshape(1, -1), *mla_tabs, *dil_tabs)

    t = MLA_TILE
    ya = pl.pallas_call(
        _mla_kernel,
        grid=(batch, HEAD_PAIRS),
        in_specs=[pl.BlockSpec((2, seq, LANES), lambda b, p: (p, b, 0)),
                  pl.BlockSpec((2, seq, LANES), lambda b, p: (p, b, 0)),
                  pl.BlockSpec((seq // tm, LANES, tm), lambda b, p: (b, p, 0)),
                  pl.BlockSpec((1, seq, LANES), lambda b, p: (p, b, 0))],
        out_specs=pl.BlockSpec((seq, LANES), lambda b, p: (b, p)),
        out_shape=jax.ShapeDtypeStruct((rows, MLA_WIDTH), bf16),
        scratch_shapes=[pltpu.VMEM((2, MLA_V_DIM + MLA_ONES_ROWS, t), f32), pltpu.VMEM((2, 2, tm, t), f32)],
        compiler_params=_params("parallel", "parallel"),
        name="mla",
    )(qm, km, vt, ga)

    dil_specs = [pl.BlockSpec((1, 1, dil, seq // dil, LANES), lambda b, p: (p, b, 0, 0, 0))
                 for dil in DILATIONS for _ in range(3)]
    yb = pl.pallas_call(
        _dilated_kernel,
        grid=(batch, HEAD_PAIRS),
        in_specs=dil_specs + [pl.BlockSpec((1, seq, LANES), lambda b, p: (p, b, 0))],
        out_specs=pl.BlockSpec((seq, LANES), lambda b, p: (b, p)),
        out_shape=jax.ShapeDtypeStruct((rows, DIL_WIDTH), bf16),
        scratch_shapes=[pltpu.VMEM((3, seq, LANES), f32)] * 3 + [
            pltpu.VMEM((2, DIL_GROUP, 2 * BLOCK, 2 * BLOCK), f32),
            pltpu.VMEM((2, DIL_GROUP, 2 * BLOCK, 2 * BLOCK), bf16),
            pltpu.VMEM((2, 2 * BLOCK, 2 * BLOCK), f32)],
        compiler_params=_params("parallel", "parallel"),
        name="dilated",
    )(*dil_in, gb)

    to = OUT_ROWS
    const = lambda shape: pl.BlockSpec(shape, lambda i: (0,) * len(shape))
    out = pl.pallas_call(
        _out_kernel,
        grid=(rows // to,),
        in_specs=[pl.BlockSpec((to, D_MODEL), lambda i: (i, 0)),
                  pl.BlockSpec((to, MLA_WIDTH), lambda i: (i, 0)),
                  pl.BlockSpec((to, DIL_WIDTH), lambda i: (i, 0)),
                  const((MLA_WIDTH, D_MODEL)), const((DIL_WIDTH, D_MODEL)),
                  const((1, D_MODEL)), const((1, D_MODEL))],
        out_specs=pl.BlockSpec((to, D_MODEL), lambda i: (i, 0)),
        out_shape=jax.ShapeDtypeStruct((rows, D_MODEL), f32),
        compiler_params=_params("parallel"),
        name="out",
    )(x2, ya, yb, wa, wb, ln_g.reshape(1, -1), ln_b.reshape(1, -1))
    return out.reshape(batch, seq, D_MODEL)
```

```python
import functools

import jax
import jax.numpy as jnp
from jax import lax
from jax.experimental import pallas as pl
from jax.experimental.pallas import tpu as pltpu

D_MODEL = 1024
ROPE_THETA = 500000.0
BLOCK = 128
NEG = -1e30
RMS_EPS = 1e-6
LN_EPS = 1e-5

MLA_HEADS = 8
MLA_NOPE_DIM = 64
MLA_ROPE_DIM = 32
MLA_V_DIM = 64
Q_LORA_RANK = 384
KV_LORA_RANK = 256
MLA_WIDTH = MLA_HEADS * MLA_V_DIM

DIL_HEADS = 8
DIL_HEAD_DIM = 64
DIL_ROT_DIM = DIL_HEAD_DIM // 4
DIL_WIDTH = DIL_HEADS * DIL_HEAD_DIM
DILATIONS = (1, 4, 16)

DEPTH = 1
DEEPNORM_ALPHA = (2.0 * DEPTH) ** 0.25
LOG2_E = 1.4426950408889634

LANES = 128
HEAD_PAIRS = MLA_HEADS // 2
VMEM_LIMIT_BYTES = 56 * 1024 * 1024

_OFF_CQ = 0
_OFF_CKV = _OFF_CQ + Q_LORA_RANK
_OFF_KPE = _OFF_CKV + KV_LORA_RANK
_OFF_GA = _OFF_KPE + LANES
_OFF_QB = _OFF_GA + MLA_WIDTH
_OFF_KB = _OFF_QB + DIL_WIDTH
_OFF_VB = _OFF_KB + DIL_WIDTH
_OFF_GB = _OFF_VB + DIL_WIDTH
_W_BIG = _OFF_GB + DIL_WIDTH

PROJ_ROWS = 256
MLA_TILE = 1024
MLA_ONES_ROWS = 16
OUT_ROWS = 512
COMBINE_ROWS = 256
DIL_GROUP = 4


def _rope_lanes(x, cos, sin_fwd, sin_bwd, half):
    fwd = pltpu.roll(x, half, 1)
    bwd = pltpu.roll(x, LANES - half, 1)
    return x * cos + fwd * sin_fwd + bwd * sin_bwd


def _proj_kernel(x_ref, w_ref, wuq_ref, wuk_ref, wuvt_ref, qg_ref, kvg_ref,
                 mcos_ref, msf_ref, msb_ref, dcos_ref, dsf_ref, dsb_ref,
                 qm_ref, km_ref, vt_ref, ga_ref, gb_ref, *dil_and_scratch):
    f32 = jnp.float32
    bf16 = jnp.bfloat16
    xb = x_ref[...].astype(bf16)
    dil_refs = [dil_and_scratch[3 * i:3 * i + 3] for i in range(len(DILATIONS))]
    stage_ref = dil_and_scratch[-1]
    tm = x_ref.shape[0]

    def emit_residues(which, p, val):
        slab = which * HEAD_PAIRS + p
        stage_ref[slab] = val
        for refs, dil in zip(dil_refs, DILATIONS):
            for r in range(dil):
                rows = val if dil == 1 else stage_ref[slab, pl.ds(r, tm // dil, stride=dil), :]
                refs[which][p, 0, r] = rows.astype(bf16)

    def seg(lo, width):
        return jnp.dot(xb, w_ref[:, lo:lo + width], preferred_element_type=f32)

    def rms(t, g):
        return t * lax.rsqrt(jnp.mean(t * t, axis=-1, keepdims=True) + RMS_EPS) * g

    mcos, msf, msb = mcos_ref[...], msf_ref[...], msb_ref[...]
    dcos, dsf, dsb = dcos_ref[...], dsf_ref[...], dsb_ref[...]
    mla_scale = (MLA_NOPE_DIM + MLA_ROPE_DIM) ** -0.5 * LOG2_E
    dil_scale = DIL_HEAD_DIM ** -0.5 * LOG2_E

    cq = rms(seg(_OFF_CQ, Q_LORA_RANK), qg_ref[...]).astype(bf16)
    qf = jnp.dot(cq, wuq_ref[...], preferred_element_type=f32)
    for h in range(MLA_HEADS):
        blk = qf[:, h * LANES:(h + 1) * LANES]
        qm_ref[h] = (_rope_lanes(blk, mcos, msf, msb, MLA_ROPE_DIM // 2) * mla_scale).astype(bf16)

    ckv = rms(seg(_OFF_CKV, KV_LORA_RANK), kvg_ref[...]).astype(bf16)
    kf = jnp.dot(ckv, wuk_ref[...], preferred_element_type=f32)
    kpe = _rope_lanes(seg(_OFF_KPE, LANES), mcos, msf, msb, MLA_ROPE_DIM // 2)
    for h in range(MLA_HEADS):
        km_ref[h] = (kf[:, h * LANES:(h + 1) * LANES] + kpe).astype(bf16)
    vt = lax.dot_general(wuvt_ref[...], ckv, (((1,), (1,)), ((), ())), preferred_element_type=f32)
    vt_ref[0] = vt.astype(bf16)

    ga = jax.nn.silu(seg(_OFF_GA, MLA_WIDTH))
    gb = jax.nn.silu(seg(_OFF_GB, DIL_WIDTH))
    qb = seg(_OFF_QB, DIL_WIDTH)
    kb = seg(_OFF_KB, DIL_WIDTH)
    vb = seg(_OFF_VB, DIL_WIDTH)
    for p in range(HEAD_PAIRS):
        sl = slice(p * LANES, (p + 1) * LANES)
        ga_ref[p] = ga[:, sl].astype(bf16)
        gb_ref[p] = gb[:, sl].astype(bf16)
        emit_residues(0, p, _rope_lanes(qb[:, sl], dcos, dsf, dsb, DIL_ROT_DIM // 2) * dil_scale)
        emit_residues(1, p, _rope_lanes(kb[:, sl], dcos, dsf, dsb, DIL_ROT_DIM // 2))
        emit_residues(2, p, vb[:, sl])


def _mla_kernel(q_ref, k_ref, vt_ref, g_ref, o_ref, acc_ref, s_ref):
    f32 = jnp.float32
    bf16 = jnp.bfloat16
    tq, tk = MLA_TILE, PROJ_ROWS
    n_chunks = tq // tk
    assert n_chunks % 2 == 0
    nq = q_ref.shape[1] // tq
    units = [(h, c) for c in range(n_chunks) for h in range(2)]
    ones = jnp.ones((acc_ref.shape[1] - MLA_V_DIM, tk), bf16)
    key = lax.broadcasted_iota(jnp.int32, (tk, tk), 0)
    qry = lax.broadcasted_iota(jnp.int32, (tk, tk), 1)
    lower = key <= qry

    def lanes(c):
        return slice(c * tk, (c + 1) * tk)

    def scores(qi, j, slot, h, c):
        q = q_ref[h, pl.ds(pl.multiple_of(qi * tq + c * tk, tk), tk), :]
        k = k_ref[h, pl.ds(pl.multiple_of(j * tk, tk), tk), :]
        s = lax.dot_general(k, q, (((1,), (1,)), ((), ())), preferred_element_type=f32)
        s_ref[slot, h, :, lanes(c)] = s
        return jnp.max(s, axis=0, keepdims=True)

    def absorb(j, slot, h, c, m_tile, m_old, masked=False):
        s = s_ref[slot, h, :, lanes(c)]
        if masked:
            s = jnp.where(lower, s, NEG)
            m_tile = jnp.max(s, axis=0, keepdims=True)
        m_new = jnp.maximum(m_old, m_tile)
        alpha = jnp.exp2(m_old - m_new)
        p = jnp.exp2(s - m_new).astype(bf16)
        vt = jnp.concatenate([vt_ref[j, h * MLA_V_DIM:(h + 1) * MLA_V_DIM, :], ones], axis=0)
        acc_ref[h, :, lanes(c)] = alpha * acc_ref[h, :, lanes(c)] + jnp.dot(vt, p, preferred_element_type=f32)
        return m_new

    def per_query_tile(qi, carry):
        rows = pl.ds(pl.multiple_of(qi * tq, tq), tq)
        acc_ref[...] = jnp.zeros_like(acc_ref)

        def step(j, slot, tile_max, m_run):
            next_max, m_new = {}, {}
            for u in units:
                next_max[u] = scores(qi, j + 1, 1 - slot, *u)
                m_new[u] = absorb(j, slot, *u, tile_max[u], m_run[u])
            return next_max, m_new

        def body(jj, carry):
            tile_max, m_run = carry
            tile_max, m_run = step(2 * jj, 0, tile_max, m_run)
            return step(2 * jj + 1, 1, tile_max, m_run)

        start = {u: jnp.full((1, tk), NEG, f32) for u in units}
        first = {u: scores(qi, 0, 0, *u) for u in units}
        tile_max, m_run = lax.fori_loop(0, qi * (n_chunks // 2), body, (first, start))

        base = qi * n_chunks
        for d in range(n_chunks):
            next_max = {}
            for h, c in units:
                if c > d:
                    next_max[h, c] = scores(qi, base + d + 1, (d + 1) % 2, h, c)
                if c >= d:
                    m_run[h, c] = absorb(base + d, d % 2, h, c, tile_max[h, c], m_run[h, c], masked=(c == d))
            tile_max = next_max

        yt = jnp.concatenate([acc_ref[h, :MLA_V_DIM, :] / acc_ref[h, MLA_V_DIM:MLA_V_DIM + 1, :] for h in range(2)],
                             axis=0)
        o_ref[rows, :] = (yt.T * g_ref[0, rows, :].astype(f32)).astype(o_ref.dtype)
        return carry

    lax.fori_loop(0, nq, per_query_tile, 0)


def _dilated_kernel(q1_ref, k1_ref, v1_ref, q4_ref, k4_ref, v4_ref, q16_ref, k16_ref, v16_ref,
                    g_ref, o_ref, num_ref, max_ref, den_ref, s_ref, p_ref, bias_ref):
    f32 = jnp.float32
    seq = num_ref.shape[1]

    lane = lax.broadcasted_iota(jnp.int32, (BLOCK, LANES), 1)
    low_half = lane < DIL_HEAD_DIM
    qi = lax.broadcasted_iota(jnp.int32, (2 * BLOCK, 2 * BLOCK), 0) % BLOCK
    kj = lax.broadcasted_iota(jnp.int32, (2 * BLOCK, 2 * BLOCK), 1)
    dist_first = qi - kj
    dist_later = dist_first + BLOCK
    for kind, dist in enumerate((dist_first, dist_later)):
        bias_ref[kind] = jnp.where((dist >= 0) & (dist <= BLOCK), 0.0, NEG).astype(f32)

    n_groups = seq // BLOCK // DIL_GROUP
    assert n_groups % 2 == 0 and n_groups >= 4

    def branch(idx, dil, q_ref, k_ref, v_ref):
        per_residue = q_ref.shape[3] // BLOCK

        def locate(b):
            return b // per_residue, b % per_residue

        def key_rows(n):
            return pl.ds(pl.multiple_of(jnp.maximum(n - 1, 0) * BLOCK, BLOCK), 2 * BLOCK)

        def out_rows(r, n):
            if dil == 1:
                return pl.ds(pl.multiple_of(n * BLOCK, BLOCK), BLOCK)
            return pl.ds(r + n * (BLOCK * dil), BLOCK, stride=dil)

        def scores(g, slot):
            for i in range(DIL_GROUP):
                r, n = locate(g * DIL_GROUP + i)
                q = q_ref[0, 0, r, pl.ds(pl.multiple_of(n * BLOCK, BLOCK), BLOCK), :]
                k = k_ref[0, 0, r, key_rows(n), :]
                zero = jnp.zeros_like(q)
                q2 = jnp.concatenate([jnp.where(low_half, q, zero), jnp.where(low_half, zero, q)], axis=0)
                s_ref[slot, i] = lax.dot_general(q2, k, (((1,), (1,)), ((), ())), preferred_element_type=f32)

        def softmax(g, slot):
            for i in range(DIL_GROUP):
                r, n = locate(g * DIL_GROUP + i)
                s = s_ref[slot, i] + bias_ref[jnp.minimum(n, 1)]
                m = jnp.max(s, axis=1, keepdims=True)
                p = jnp.exp2(s - m)
                den = jnp.sum(p, axis=1, keepdims=True)
                p_ref[slot, i] = p.astype(jnp.bfloat16)
                rows = out_rows(r, n)
                max_ref[idx, rows, :] = jnp.where(low_half, m[:BLOCK], m[BLOCK:])
                den_ref[idx, rows, :] = jnp.where(low_half, den[:BLOCK], den[BLOCK:])

        def values(g, slot):
            for i in range(DIL_GROUP):
                r, n = locate(g * DIL_GROUP + i)
                num = jnp.dot(p_ref[slot, i], v_ref[0, 0, r, key_rows(n), :], preferred_element_type=f32)
                num_ref[idx, out_rows(r, n), :] = jnp.where(low_half, num[:BLOCK], num[BLOCK:])

        return scores, softmax, values

    stages = [branch(0, DILATIONS[0], q1_ref, k1_ref, v1_ref),
              branch(1, DILATIONS[1], q4_ref, k4_ref, v4_ref),
              branch(2, DILATIONS[2], q16_ref, k16_ref, v16_ref)]
    last = n_groups - 1

    for b, (scores, softmax, values) in enumerate(stages):
        if b == 0:
            scores(0, 0)
            scores(1, 1)
            softmax(0, 0)
        else:
            prev_values = stages[b - 1][2]
            prev_values(last, 1)
            scores(1, 1)
            softmax(0, 0)

        def two_trips(t, carry, scores=scores, softmax=softmax, values=values):
            values(2 * t, 0)
            scores(2 * t + 2, 0)
            softmax(2 * t + 1, 1)
            values(2 * t + 1, 1)
            scores(2 * t + 3, 1)
            softmax(2 * t + 2, 0)
            return carry

        lax.fori_loop(0, (n_groups - 2) // 2, two_trips, 0)
        values(last - 1, 0)
        if b + 1 < len(stages):
            stages[b + 1][0](0, 0)
        softmax(last, 1)
    stages[-1][2](last, 1)

    def merge(c, carry):
        rows = pl.ds(pl.multiple_of(c * COMBINE_ROWS, COMBINE_ROWS), COMBINE_ROWS)
        ms = [max_ref[i, rows, :] for i in range(3)]
        m_all = jnp.maximum(jnp.maximum(ms[0], ms[1]), ms[2])
        num = jnp.zeros((COMBINE_ROWS, LANES), f32)
        den = jnp.zeros((COMBINE_ROWS, LANES), f32)
        for i in range(3):
            w = jnp.exp2(ms[i] - m_all)
            num = num + w * num_ref[i, rows, :]
            den = den + w * den_ref[i, rows, :]
        o_ref[rows, :] = (num / den * g_ref[0, rows, :].astype(f32)).astype(o_ref.dtype)
        return carry

    lax.fori_loop(0, seq // COMBINE_ROWS, merge, 0)


def _out_kernel(x_ref, ya_ref, yb_ref, wa_ref, wb_ref, g_ref, b_ref, o_ref):
    f32 = jnp.float32
    h = DEEPNORM_ALPHA * x_ref[...]
    h = h + jnp.dot(ya_ref[...], wa_ref[...], preferred_element_type=f32)
    h = h + jnp.dot(yb_ref[...], wb_ref[...], preferred_element_type=f32)
    mu = jnp.mean(h, axis=-1, keepdims=True)
    c = h - mu
    var = jnp.mean(c * c, axis=-1, keepdims=True)
    o_ref[...] = c * lax.rsqrt(var + LN_EPS) * g_ref[...] + b_ref[...]


def _rope_tables(seq, rot_dim, period, rot_offset, pass_rest):
    half = rot_dim // 2
    inv_freq = ROPE_THETA ** (-jnp.arange(0, rot_dim, 2, dtype=jnp.float32) / rot_dim)
    ang = jnp.arange(seq, dtype=jnp.float32)[:, None] * inv_freq[None, :]
    cos, sin = jnp.cos(ang), jnp.sin(ang)
    zeros = jnp.zeros((seq, half), jnp.float32)
    rest = period - rot_offset - rot_dim
    fill = jnp.ones if pass_rest else jnp.zeros
    group_cos = jnp.concatenate([jnp.ones((seq, rot_offset), jnp.float32), cos, cos,
                                 fill((seq, rest), jnp.float32)], axis=1)
    group_fwd = jnp.concatenate([jnp.zeros((seq, rot_offset), jnp.float32), zeros, sin,
                                 jnp.zeros((seq, rest), jnp.float32)], axis=1)
    group_bwd = jnp.concatenate([jnp.zeros((seq, rot_offset), jnp.float32), -sin, zeros,
                                 jnp.zeros((seq, rest), jnp.float32)], axis=1)
    reps = LANES // period
    return tuple(jnp.tile(t, (1, reps)) for t in (group_cos, group_fwd, group_bwd))


def _params(*semantics, flags=None):
    return pltpu.CompilerParams(dimension_semantics=semantics, vmem_limit_bytes=VMEM_LIMIT_BYTES, flags=flags)


def kernel(x, w_in, q_norm_g, kv_norm_g, w_uq, w_ukv, w_out, ln_g, ln_b):
    f32, bf16 = jnp.float32, jnp.bfloat16
    batch, seq, _ = x.shape
    rows = batch * seq
    x2 = x.reshape(rows, D_MODEL)

    offs = (0, Q_LORA_RANK, Q_LORA_RANK + KV_LORA_RANK, Q_LORA_RANK + KV_LORA_RANK + MLA_ROPE_DIM)
    kpe_w = jnp.zeros((D_MODEL, LANES), f32).at[:, MLA_NOPE_DIM:MLA_NOPE_DIM + MLA_ROPE_DIM].set(
        w_in[:, offs[2]:offs[3]])
    w_big = jnp.concatenate([w_in[:, :offs[2]], kpe_w, w_in[:, offs[3]:]], axis=1).astype(bf16)
    assert w_big.shape[1] == _W_BIG
    dk = MLA_NOPE_DIM + MLA_ROPE_DIM
    wuq = jnp.pad(w_uq.reshape(Q_LORA_RANK, MLA_HEADS, dk), ((0, 0), (0, 0), (0, LANES - dk)))
    wuq = wuq.reshape(Q_LORA_RANK, MLA_HEADS * LANES).astype(bf16)
    wukv = w_ukv.reshape(KV_LORA_RANK, MLA_HEADS, MLA_NOPE_DIM + MLA_V_DIM)
    wuk = jnp.pad(wukv[:, :, :MLA_NOPE_DIM], ((0, 0), (0, 0), (0, LANES - MLA_NOPE_DIM)))
    wuk = wuk.reshape(KV_LORA_RANK, MLA_HEADS * LANES).astype(bf16)
    wuvt = wukv[:, :, MLA_NOPE_DIM:].reshape(KV_LORA_RANK, MLA_WIDTH).T.astype(bf16)
    wa = w_out[:MLA_WIDTH].astype(bf16)
    wb = w_out[MLA_WIDTH:].astype(bf16)

    mla_tabs = _rope_tables(seq, MLA_ROPE_DIM, LANES, MLA_NOPE_DIM, pass_rest=False)
    dil_tabs = _rope_tables(seq, DIL_ROT_DIM, DIL_HEAD_DIM, 0, pass_rest=True)

    tm = PROJ_ROWS
    seq_tiles = seq // tm
    full = lambda shape: pl.BlockSpec(shape, lambda i: (0,) * len(shape))
    tab = pl.BlockSpec((tm, LANES), lambda i: (i % seq_tiles, 0))
    slab = lambda n: pl.BlockSpec((n, tm, LANES), lambda i: (0, i, 0))
    slab_shape = lambda n: jax.ShapeDtypeStruct((n, rows, LANES), bf16)
    vt_spec = pl.BlockSpec((1, MLA_WIDTH, tm), lambda i: (i, 0, 0))
    vt_shape = jax.ShapeDtypeStruct((rows // tm, MLA_WIDTH, tm), bf16)
    res_specs, res_shapes = [], []
    for dil in DILATIONS:
        spec = pl.BlockSpec((HEAD_PAIRS, 1, dil, tm // dil, LANES),
                            lambda i: (0, i // seq_tiles, 0, i % seq_tiles, 0))
        shape = jax.ShapeDtypeStruct((HEAD_PAIRS, batch, dil, seq // dil, LANES), bf16)
        res_specs += [spec] * 3
        res_shapes += [shape] * 3
    qm, km, vt, ga, gb, *dil_in = pl.pallas_call(
        _proj_kernel,
        grid=(rows // tm,),
        in_specs=[pl.BlockSpec((tm, D_MODEL), lambda i: (i, 0)),
                  full((D_MODEL, _W_BIG)), full(wuq.shape), full(wuk.shape), full(wuvt.shape),
                  full((1, Q_LORA_RANK)), full((1, KV_LORA_RANK)),
                  tab, tab, tab, tab, tab, tab],
        out_specs=[slab(MLA_HEADS), slab(MLA_HEADS), vt_spec, slab(HEAD_PAIRS), slab(HEAD_PAIRS)] + res_specs,
        out_shape=[slab_shape(MLA_HEADS), slab_shape(MLA_HEADS), vt_shape, slab_shape(HEAD_PAIRS),
                   slab_shape(HEAD_PAIRS)] + res_shapes,
        scratch_shapes=[pltpu.VMEM((3 * HEAD_PAIRS, tm, LANES), f32)],
        compiler_params=_params("parallel"),
        name="proj",
    )(x2, w_big, wuq, wuk, wuvt, q_norm_g.reshape(1, -1), kv_norm_g.reshape(1, -1), *mla_tabs, *dil_tabs)

    t = MLA_TILE
    ya = pl.pallas_call(
        _mla_kernel,
        grid=(batch, HEAD_PAIRS),
        in_specs=[pl.BlockSpec((2, seq, LANES), lambda b, p: (p, b, 0)),
                  pl.BlockSpec((2, seq, LANES), lambda b, p: (p, b, 0)),
                  pl.BlockSpec((seq // tm, LANES, tm), lambda b, p: (b, p, 0)),
                  pl.BlockSpec((1, seq, LANES), lambda b, p: (p, b, 0))],
        out_specs=pl.BlockSpec((seq, LANES), lambda b, p: (b, p)),
        out_shape=jax.ShapeDtypeStruct((rows, MLA_WIDTH), bf16),
        scratch_shapes=[pltpu.VMEM((2, MLA_V_DIM + MLA_ONES_ROWS, t), f32), pltpu.VMEM((2, 2, tm, t), f32)],
        compiler_params=_params("parallel", "parallel"),
        name="mla",
    )(qm, km, vt, ga)

    dil_specs = [pl.BlockSpec((1, 1, dil, seq // dil, LANES), lambda b, p: (p, b, 0, 0, 0))
                 for dil in DILATIONS for _ in range(3)]
    yb = pl.pallas_call(
        _dilated_kernel,
        grid=(batch, HEAD_PAIRS),
        in_specs=dil_specs + [pl.BlockSpec((1, seq, LANES), lambda b, p: (p, b, 0))],
        out_specs=pl.BlockSpec((seq, LANES), lambda b, p: (b, p)),
        out_shape=jax.ShapeDtypeStruct((rows, DIL_WIDTH), bf16),
        scratch_shapes=[pltpu.VMEM((3, seq, LANES), f32)] * 3 + [
            pltpu.VMEM((2, DIL_GROUP, 2 * BLOCK, 2 * BLOCK), f32),
            pltpu.VMEM((2, DIL_GROUP, 2 * BLOCK, 2 * BLOCK), bf16),
            pltpu.VMEM((2, 2 * BLOCK, 2 * BLOCK), f32)],
        compiler_params=_params("parallel", "parallel"),
        name="dilated",
    )(*dil_in, gb)

    to = OUT_ROWS
    const = lambda shape: pl.BlockSpec(shape, lambda i: (0,) * len(shape))
    out = pl.pallas_call(
        _out_kernel,
        grid=(rows // to,),
        in_specs=[pl.BlockSpec((to, D_MODEL), lambda i: (i, 0)),
                  pl.BlockSpec((to, MLA_WIDTH), lambda i: (i, 0)),
                  pl.BlockSpec((to, DIL_WIDTH), lambda i: (i, 0)),
                  const((MLA_WIDTH, D_MODEL)), const((DIL_WIDTH, D_MODEL)),
                  const((1, D_MODEL)), const((1, D_MODEL))],
        out_specs=pl.BlockSpec((to, D_MODEL), lambda i: (i, 0)),
        out_shape=jax.ShapeDtypeStruct((rows, D_MODEL), f32),
        compiler_params=_params("parallel"),
        name="out",
    )(x2, ya, yb, wa, wb, ln_g.reshape(1, -1), ln_b.reshape(1, -1))
    return out.reshape(batch, seq, D_MODEL)
```

```python
import functools

import jax
import jax.numpy as jnp
from jax import lax
from jax.experimental import pallas as pl
from jax.experimental.pallas import tpu as pltpu

D_MODEL = 1024
ROPE_THETA = 500000.0
BLOCK = 128
NEG = -1e30
RMS_EPS = 1e-6
LN_EPS = 1e-5

MLA_HEADS = 8
MLA_NOPE_DIM = 64
MLA_ROPE_DIM = 32
MLA_V_DIM = 64
Q_LORA_RANK = 384
KV_LORA_RANK = 256
MLA_WIDTH = MLA_HEADS * MLA_V_DIM

DIL_HEADS = 8
DIL_HEAD_DIM = 64
DIL_ROT_DIM = DIL_HEAD_DIM // 4
DIL_WIDTH = DIL_HEADS * DIL_HEAD_DIM
DILATIONS = (1, 4, 16)

DEPTH = 1
DEEPNORM_ALPHA = (2.0 * DEPTH) ** 0.25
LOG2_E = 1.4426950408889634

LANES = 128
HEAD_PAIRS = MLA_HEADS // 2
VMEM_LIMIT_BYTES = 56 * 1024 * 1024

_OFF_CQ = 0
_OFF_CKV = _OFF_CQ + Q_LORA_RANK
_OFF_KPE = _OFF_CKV + KV_LORA_RANK
_OFF_GA = _OFF_KPE + LANES
_OFF_QB = _OFF_GA + MLA_WIDTH
_OFF_KB = _OFF_QB + DIL_WIDTH
_OFF_VB = _OFF_KB + DIL_WIDTH
_OFF_GB = _OFF_VB + DIL_WIDTH
_W_BIG = _OFF_GB + DIL_WIDTH

PROJ_ROWS = 256
MLA_TILE = 1024
MLA_ONES_ROWS = 16
OUT_ROWS = 512
COMBINE_ROWS = 256
DIL_GROUP = 8


def _rope_lanes(x, cos, sin_fwd, sin_bwd, half):
    fwd = pltpu.roll(x, half, 1)
    bwd = pltpu.roll(x, LANES - half, 1)
    return x * cos + fwd * sin_fwd + bwd * sin_bwd


def _proj_kernel(x_ref, w_ref, wuq_ref, wuk_ref, wuvt_ref, qg_ref, kvg_ref,
                 mcos_ref, msf_ref, msb_ref, dcos_ref, dsf_ref, dsb_ref,
                 qm_ref, km_ref, vt_ref, ga_ref, gb_ref, *dil_and_scratch):
    f32 = jnp.float32
    bf16 = jnp.bfloat16
    xb = x_ref[...].astype(bf16)
    dil_refs = [dil_and_scratch[3 * i:3 * i + 3] for i in range(len(DILATIONS))]
    stage_ref = dil_and_scratch[-1]
    tm = x_ref.shape[0]

    def emit_residues(which, p, val):
        slab = which * HEAD_PAIRS + p
        stage_ref[slab] = val
        for refs, dil in zip(dil_refs, DILATIONS):
            for r in range(dil):
                rows = val if dil == 1 else stage_ref[slab, pl.ds(r, tm // dil, stride=dil), :]
                refs[which][p, 0, r] = rows.astype(bf16)

    def seg(lo, width):
        return jnp.dot(xb, w_ref[:, lo:lo + width], preferred_element_type=f32)

    def rms(t, g):
        return t * lax.rsqrt(jnp.mean(t * t, axis=-1, keepdims=True) + RMS_EPS) * g

    mcos, msf, msb = mcos_ref[...], msf_ref[...], msb_ref[...]
    dcos, dsf, dsb = dcos_ref[...], dsf_ref[...], dsb_ref[...]
    mla_scale = (MLA_NOPE_DIM + MLA_ROPE_DIM) ** -0.5 * LOG2_E
    dil_scale = DIL_HEAD_DIM ** -0.5 * LOG2_E

    qb = seg(_OFF_QB, DIL_WIDTH)
    kb = seg(_OFF_KB, DIL_WIDTH)
    vb = seg(_OFF_VB, DIL_WIDTH)
    for p in range(HEAD_PAIRS):
        sl = slice(p * LANES, (p + 1) * LANES)
        emit_residues(0, p, _rope_lanes(qb[:, sl], dcos, dsf, dsb, DIL_ROT_DIM // 2) * dil_scale)
        emit_residues(1, p, _rope_lanes(kb[:, sl], dcos, dsf, dsb, DIL_ROT_DIM // 2))
        emit_residues(2, p, vb[:, sl])

    ga = jax.nn.silu(seg(_OFF_GA, MLA_WIDTH))
    gb = jax.nn.silu(seg(_OFF_GB, DIL_WIDTH))
    for p in range(HEAD_PAIRS):
        sl = slice(p * LANES, (p + 1) * LANES)
        ga_ref[p] = ga[:, sl].astype(bf16)
        gb_ref[p] = gb[:, sl].astype(bf16)

    cq = rms(seg(_OFF_CQ, Q_LORA_RANK), qg_ref[...]).astype(bf16)
    qf = jnp.dot(cq, wuq_ref[...], preferred_element_type=f32)
    for h in range(MLA_HEADS):
        blk = qf[:, h * LANES:(h + 1) * LANES]
        qm_ref[h] = (_rope_lanes(blk, mcos, msf, msb, MLA_ROPE_DIM // 2) * mla_scale).astype(bf16)

    ckv = rms(seg(_OFF_CKV, KV_LORA_RANK), kvg_ref[...]).astype(bf16)
    kf = jnp.dot(ckv, wuk_ref[...], preferred_element_type=f32)
    kpe = _rope_lanes(seg(_OFF_KPE, LANES), mcos, msf, msb, MLA_ROPE_DIM // 2)
    for h in range(MLA_HEADS):
        km_ref[h] = (kf[:, h * LANES:(h + 1) * LANES] + kpe).astype(bf16)
    vt = lax.dot_general(wuvt_ref[...], ckv, (((1,), (1,)), ((), ())), preferred_element_type=f32)
    vt_ref[0] = vt.astype(bf16)


def _mla_kernel(q_ref, k_ref, vt_ref, g_ref, o_ref, accs_ref, s_ref):
    f32 = jnp.float32
    bf16 = jnp.bfloat16
    tq, tk = MLA_TILE, PROJ_ROWS
    n_chunks = tq // tk
    assert n_chunks % 2 == 0
    nq = q_ref.shape[1] // tq
    units = [(h, c) for c in range(n_chunks) for h in range(2)]
    ones = jnp.ones((MLA_ONES_ROWS, tk), bf16)
    key = lax.broadcasted_iota(jnp.int32, (tk, tk), 0)
    qry = lax.broadcasted_iota(jnp.int32, (tk, tk), 1)
    lower = key <= qry

    def lanes(c):
        return slice(c * tk, (c + 1) * tk)

    def scores(qi, j, slot, h, c):
        q = q_ref[h, pl.ds(qi * tq + c * tk, tk), :]
        k = k_ref[h, pl.ds(j * tk if isinstance(j, int) else pl.multiple_of(j * tk, tk), tk), :]
        s = lax.dot_general(k, q, (((1,), (1,)), ((), ())), preferred_element_type=f32)
        s_ref[slot, h, :, lanes(c)] = s
        return jnp.max(s, axis=0, keepdims=True)

    def absorb(acc_ref, j, slot, h, c, m_tile, m_old, masked=False):
        s = s_ref[slot, h, :, lanes(c)]
        if masked:
            s = jnp.where(lower, s, NEG)
            m_tile = jnp.max(s, axis=0, keepdims=True)
        m_new = jnp.maximum(m_old, m_tile)
        alpha = jnp.exp2(m_old - m_new)
        p = jnp.exp2(s - m_new).astype(bf16)
        vt = jnp.concatenate([vt_ref[j, h * MLA_V_DIM:(h + 1) * MLA_V_DIM, :], ones], axis=0)
        acc_ref[h, :, lanes(c)] = alpha * acc_ref[h, :, lanes(c)] + jnp.dot(vt, p, preferred_element_type=f32)
        return m_new

    def per_query_tile(qi, acc_ref):
        rows = pl.ds(qi * tq, tq)
        acc_ref[...] = jnp.zeros_like(acc_ref)

        def step(j, slot, tile_max, m_run):
            next_max, m_new = {}, {}
            for u in units:
                next_max[u] = scores(qi, j + 1, 1 - slot, *u)
                m_new[u] = absorb(acc_ref, j, slot, *u, tile_max[u], m_run[u])
            return next_max, m_new

        def body(jj, carry):
            tile_max, m_run = carry
            for i in range(n_chunks):
                tile_max, m_run = step(n_chunks * jj + i, i % 2, tile_max, m_run)
            return tile_max, m_run

        start = {u: jnp.full((1, tk), NEG, f32) for u in units}
        first = {u: scores(qi, 0, 0, *u) for u in units}
        tile_max, m_run = lax.fori_loop(0, qi, body, (first, start))

        base = qi * n_chunks
        for d in range(n_chunks):
            next_max = {}
            for h, c in units:
                if c > d:
                    next_max[h, c] = scores(qi, base + d + 1, (d + 1) % 2, h, c)
                if c >= d:
                    m_run[h, c] = absorb(acc_ref, base + d, d % 2, h, c, tile_max[h, c], m_run[h, c],
                                         masked=(c == d))
            tile_max = next_max

        yt = jnp.concatenate([acc_ref[h, :MLA_V_DIM, :] / acc_ref[h, MLA_V_DIM:MLA_V_DIM + 1, :] for h in range(2)],
                             axis=0)
        o_ref[rows, :] = (yt.T * g_ref[0, rows, :].astype(f32)).astype(o_ref.dtype)

    for qi in range(nq):
        per_query_tile(qi, accs_ref.at[qi % 2])


def _dilated_kernel(q1_ref, k1_ref, v1_ref, q4_ref, k4_ref, v4_ref, q16_ref, k16_ref, v16_ref,
                    g_ref, o_ref, num_ref, max_ref, den_ref, s_ref, p_ref, bias_ref):
    f32 = jnp.float32
    seq = num_ref.shape[1]

    lane = lax.broadcasted_iota(jnp.int32, (BLOCK, LANES), 1)
    low_half = lane < DIL_HEAD_DIM
    ones = jnp.ones((2 * BLOCK, LANES), jnp.bfloat16)
    qi = lax.broadcasted_iota(jnp.int32, (2 * BLOCK, 2 * BLOCK), 0) % BLOCK
    kj = lax.broadcasted_iota(jnp.int32, (2 * BLOCK, 2 * BLOCK), 1)
    dist_first = qi - kj
    dist_later = dist_first + BLOCK
    for kind, dist in enumerate((dist_first, dist_later)):
        bias_ref[kind] = jnp.where((dist >= 0) & (dist <= BLOCK), 0.0, NEG).astype(f32)

    n_groups = seq // BLOCK // DIL_GROUP
    assert n_groups % 2 == 0 and n_groups >= 4

    def branch(idx, dil, q_ref, k_ref, v_ref):
        per_residue = q_ref.shape[3] // BLOCK

        def locate(b):
            return b // per_residue, b % per_residue

        def key_rows(n):
            return pl.ds(pl.multiple_of(jnp.maximum(n - 1, 0) * BLOCK, BLOCK), 2 * BLOCK)

        def out_rows(r, n):
            if dil == 1:
                return pl.ds(pl.multiple_of(n * BLOCK, BLOCK), BLOCK)
            return pl.ds(r + n * (BLOCK * dil), BLOCK, stride=dil)

        def scores(g, slot, i):
            r, n = locate(g * DIL_GROUP + i)
            q = q_ref[0, 0, r, pl.ds(pl.multiple_of(n * BLOCK, BLOCK), BLOCK), :]
            k = k_ref[0, 0, r, key_rows(n), :]
            zero = jnp.zeros_like(q)
            q2 = jnp.concatenate([jnp.where(low_half, q, zero), jnp.where(low_half, zero, q)], axis=0)
            s_ref[slot, i] = lax.dot_general(q2, k, (((1,), (1,)), ((), ())), preferred_element_type=f32)

        def softmax(g, slot, i):
            r, n = locate(g * DIL_GROUP + i)
            s = s_ref[slot, i] + bias_ref[jnp.minimum(n, 1)]
            m = jnp.max(s, axis=1, keepdims=True)
            p_ref[slot, i] = jnp.exp2(s - m).astype(jnp.bfloat16)
            max_ref[idx, out_rows(r, n), :] = jnp.where(low_half, m[:BLOCK], m[BLOCK:])

        def values(g, slot, i):
            r, n = locate(g * DIL_GROUP + i)
            v = jnp.concatenate([v_ref[0, 0, r, key_rows(n), :], ones], axis=1)
            both = jnp.dot(p_ref[slot, i], v, preferred_element_type=f32)
            num, den = both[:, :LANES], both[:, LANES:]
            rows = out_rows(r, n)
            num_ref[idx, rows, :] = jnp.where(low_half, num[:BLOCK], num[BLOCK:])
            den_ref[idx, rows, :] = jnp.where(low_half, den[:BLOCK], den[BLOCK:])

        return scores, softmax, values

    stages = [branch(0, DILATIONS[0], q1_ref, k1_ref, v1_ref),
              branch(1, DILATIONS[1], q4_ref, k4_ref, v4_ref),
              branch(2, DILATIONS[2], q16_ref, k16_ref, v16_ref)]
    last = n_groups - 1

    def trip(*work):
        for stage, g, slot in work:
            for i in range(DIL_GROUP):
                stage(g, slot, i)

    for b, (scores, softmax, values) in enumerate(stages):
        if b == 0:
            trip((scores, 0, 0))
            trip((scores, 1, 1), (softmax, 0, 0))
        else:
            trip((stages[b - 1][2], last, 1), (scores, 1, 1), (softmax, 0, 0))

        def two_trips(t, carry, scores=scores, softmax=softmax, values=values):
            trip((values, 2 * t, 0), (scores, 2 * t + 2, 0), (softmax, 2 * t + 1, 1))
            trip((values, 2 * t + 1, 1), (scores, 2 * t + 3, 1), (softmax, 2 * t + 2, 0))
            return carry

        lax.fori_loop(0, (n_groups - 2) // 2, two_trips, 0)
        if b + 1 < len(stages):
            trip((values, last - 1, 0), (stages[b + 1][0], 0, 0), (softmax, last, 1))
        else:
            trip((values, last - 1, 0), (softmax, last, 1))
    trip((stages[-1][2], last, 1))

    def merge(c, carry):
        rows = pl.ds(pl.multiple_of(c * COMBINE_ROWS, COMBINE_ROWS), COMBINE_ROWS)
        ms = [max_ref[i, rows, :] for i in range(3)]
        m_all = jnp.maximum(jnp.maximum(ms[0], ms[1]), ms[2])
        num = jnp.zeros((COMBINE_ROWS, LANES), f32)
        den = jnp.zeros((COMBINE_ROWS, LANES), f32)
        for i in range(3):
            w = jnp.exp2(ms[i] - m_all)
            num = num + w * num_ref[i, rows, :]
            den = den + w * den_ref[i, rows, :]
        o_ref[rows, :] = (num / den * g_ref[0, rows, :].astype(f32)).astype(o_ref.dtype)
        return carry

    lax.fori_loop(0, seq // COMBINE_ROWS, merge, 0)


def _out_kernel(x_ref, ya_ref, yb_ref, wa_ref, wb_ref, g_ref, b_ref, o_ref):
    f32 = jnp.float32
    h = DEEPNORM_ALPHA * x_ref[...]
    h = h + jnp.dot(ya_ref[...], wa_ref[...], preferred_element_type=f32)
    h = h + jnp.dot(yb_ref[...], wb_ref[...], preferred_element_type=f32)
    mu = jnp.mean(h, axis=-1, keepdims=True)
    c = h - mu
    var = jnp.mean(c * c, axis=-1, keepdims=True)
    o_ref[...] = c * lax.rsqrt(var + LN_EPS) * g_ref[...] + b_ref[...]


def _rope_tables(seq, rot_dim, period, rot_offset, pass_rest):
    half = rot_dim // 2
    inv_freq = ROPE_THETA ** (-jnp.arange(0, rot_dim, 2, dtype=jnp.float32) / rot_dim)
    ang = jnp.arange(seq, dtype=jnp.float32)[:, None] * inv_freq[None, :]
    cos, sin = jnp.cos(ang), jnp.sin(ang)
    zeros = jnp.zeros((seq, half), jnp.float32)
    rest = period - rot_offset - rot_dim
    fill = jnp.ones if pass_rest else jnp.zeros
    group_cos = jnp.concatenate([jnp.ones((seq, rot_offset), jnp.float32), cos, cos,
                                 fill((seq, rest), jnp.float32)], axis=1)
    group_fwd = jnp.concatenate([jnp.zeros((seq, rot_offset), jnp.float32), zeros, sin,
                                 jnp.zeros((seq, rest), jnp.float32)], axis=1)
    group_bwd = jnp.concatenate([jnp.zeros((seq, rot_offset), jnp.float32), -sin, zeros,
                                 jnp.zeros((seq, rest), jnp.float32)], axis=1)
    reps = LANES // period
    return tuple(jnp.tile(t, (1, reps)) for t in (group_cos, group_fwd, group_bwd))


def _params(*semantics, flags=None):
    return pltpu.CompilerParams(dimension_semantics=semantics, vmem_limit_bytes=VMEM_LIMIT_BYTES, flags=flags)


def kernel(x, w_in, q_norm_g, kv_norm_g, w_uq, w_ukv, w_out, ln_g, ln_b):
    f32, bf16 = jnp.float32, jnp.bfloat16
    batch, seq, _ = x.shape
    rows = batch * seq
    x2 = x.reshape(rows, D_MODEL)

    offs = (0, Q_LORA_RANK, Q_LORA_RANK + KV_LORA_RANK, Q_LORA_RANK + KV_LORA_RANK + MLA_ROPE_DIM)
    kpe_w = jnp.zeros((D_MODEL, LANES), f32).at[:, MLA_NOPE_DIM:MLA_NOPE_DIM + MLA_ROPE_DIM].set(
        w_in[:, offs[2]:offs[3]])
    w_big = jnp.concatenate([w_in[:, :offs[2]], kpe_w, w_in[:, offs[3]:]], axis=1).astype(bf16)
    assert w_big.shape[1] == _W_BIG
    dk = MLA_NOPE_DIM + MLA_ROPE_DIM
    wuq = jnp.pad(w_uq.reshape(Q_LORA_RANK, MLA_HEADS, dk), ((0, 0), (0, 0), (0, LANES - dk)))
    wuq = wuq.reshape(Q_LORA_RANK, MLA_HEADS * LANES).astype(bf16)
    wukv = w_ukv.reshape(KV_LORA_RANK, MLA_HEADS, MLA_NOPE_DIM + MLA_V_DIM)
    wuk = jnp.pad(wukv[:, :, :MLA_NOPE_DIM], ((0, 0), (0, 0), (0, LANES - MLA_NOPE_DIM)))
    wuk = wuk.reshape(KV_LORA_RANK, MLA_HEADS * LANES).astype(bf16)
    wuvt = wukv[:, :, MLA_NOPE_DIM:].reshape(KV_LORA_RANK, MLA_WIDTH).T.astype(bf16)
    wa = w_out[:MLA_WIDTH].astype(bf16)
    wb = w_out[MLA_WIDTH:].astype(bf16)

    mla_tabs = _rope_tables(seq, MLA_ROPE_DIM, LANES, MLA_NOPE_DIM, pass_rest=False)
    dil_tabs = _rope_tables(seq, DIL_ROT_DIM, DIL_HEAD_DIM, 0, pass_rest=True)

    tm = PROJ_ROWS
    seq_tiles = seq // tm
    full = lambda shape: pl.BlockSpec(shape, lambda i: (0,) * len(shape))
    tab = pl.BlockSpec((tm, LANES), lambda i: (i % seq_tiles, 0))
    slab = lambda n: pl.BlockSpec((n, tm, LANES), lambda i: (0, i, 0))
    slab_shape = lambda n: jax.ShapeDtypeStruct((n, rows, LANES), bf16)
    vt_spec = pl.BlockSpec((1, MLA_WIDTH, tm), lambda i: (i, 0, 0))
    vt_shape = jax.ShapeDtypeStruct((rows // tm, MLA_WIDTH, tm), bf16)
    res_specs, res_shapes = [], []
    for dil in DILATIONS:
        spec = pl.BlockSpec((HEAD_PAIRS, 1, dil, tm // dil, LANES),
                            lambda i: (0, i // seq_tiles, 0, i % seq_tiles, 0))
        shape = jax.ShapeDtypeStruct((HEAD_PAIRS, batch, dil, seq // dil, LANES), bf16)
        res_specs += [spec] * 3
        res_shapes += [shape] * 3
    qm, km, vt, ga, gb, *dil_in = pl.pallas_call(
        _proj_kernel,
        grid=(rows // tm,),
        in_specs=[pl.BlockSpec((tm, D_MODEL), lambda i: (i, 0)),
                  full((D_MODEL, _W_BIG)), full(wuq.shape), full(wuk.shape), full(wuvt.shape),
                  full((1, Q_LORA_RANK)), full((1, KV_LORA_RANK)),
                  tab, tab, tab, tab, tab, tab],
        out_specs=[slab(MLA_HEADS), slab(MLA_HEADS), vt_spec, slab(HEAD_PAIRS), slab(HEAD_PAIRS)] + res_specs,
        out_shape=[slab_shape(MLA_HEADS), slab_shape(MLA_HEADS), vt_shape, slab_shape(HEAD_PAIRS),
                   slab_shape(HEAD_PAIRS)] + res_shapes,
        scratch_shapes=[pltpu.VMEM((3 * HEAD_PAIRS, tm, LANES), f32)],
        compiler_params=_params("parallel"),
        name="proj",
    )(x2, w_big, wuq, wuk, wuvt, q_norm_g.reshape(1, -1), kv_norm_g.reshape(1, -1), *mla_tabs, *dil_tabs)

    t = MLA_TILE
    ya = pl.pallas_call(
        _mla_kernel,
        grid=(batch, HEAD_PAIRS),
        in_specs=[pl.BlockSpec((2, seq, LANES), lambda b, p: (p, b, 0)),
                  pl.BlockSpec((2, seq, LANES), lambda b, p: (p, b, 0)),
                  pl.BlockSpec((seq // tm, LANES, tm), lambda b, p: (b, p, 0)),
                  pl.BlockSpec((1, seq, LANES), lambda b, p: (p, b, 0))],
        out_specs=pl.BlockSpec((seq, LANES), lambda b, p: (b, p)),
        out_shape=jax.ShapeDtypeStruct((rows, MLA_WIDTH), bf16),
        scratch_shapes=[pltpu.VMEM((2, 2, MLA_V_DIM + MLA_ONES_ROWS, t), f32), pltpu.VMEM((2, 2, tm, t), f32)],
        compiler_params=_params("parallel", "parallel"),
        name="mla",
    )(qm, km, vt, ga)

    dil_specs = [pl.BlockSpec((1, 1, dil, seq // dil, LANES), lambda b, p: (p, b, 0, 0, 0))
                 for dil in DILATIONS for _ in range(3)]
    yb = pl.pallas_call(
        _dilated_kernel,
        grid=(batch, HEAD_PAIRS),
        in_specs=dil_specs + [pl.BlockSpec((1, seq, LANES), lambda b, p: (p, b, 0))],
        out_specs=pl.BlockSpec((seq, LANES), lambda b, p: (b, p)),
        out_shape=jax.ShapeDtypeStruct((rows, DIL_WIDTH), bf16),
        scratch_shapes=[pltpu.VMEM((3, seq, LANES), f32)] * 3 + [
            pltpu.VMEM((2, DIL_GROUP, 2 * BLOCK, 2 * BLOCK), f32),
            pltpu.VMEM((2, DIL_GROUP, 2 * BLOCK, 2 * BLOCK), bf16),
            pltpu.VMEM((2, 2 * BLOCK, 2 * BLOCK), f32)],
        compiler_params=_params("parallel", "parallel"),
        name="dilated",
    )(*dil_in, gb)

    to = OUT_ROWS
    const = lambda shape: pl.BlockSpec(shape, lambda i: (0,) * len(shape))
    out = pl.pallas_call(
        _out_kernel,
        grid=(rows // to,),
        in_specs=[pl.BlockSpec((to, D_MODEL), lambda i: (i, 0)),
                  pl.BlockSpec((to, MLA_WIDTH), lambda i: (i, 0)),
                  pl.BlockSpec((to, DIL_WIDTH), lambda i: (i, 0)),
                  const((MLA_WIDTH, D_MODEL)), const((DIL_WIDTH, D_MODEL)),
                  const((1, D_MODEL)), const((1, D_MODEL))],
        out_specs=pl.BlockSpec((to, D_MODEL), lambda i: (i, 0)),
        out_shape=jax.ShapeDtypeStruct((rows, D_MODEL), f32),
        compiler_params=_params("parallel"),
        name="out",
    )(x2, ya, yb, wa, wb, ln_g.reshape(1, -1), ln_b.reshape(1, -1))
    return out.reshape(batch, seq, D_MODEL)
```

```python
import functools

import jax
import jax.numpy as jnp
from jax import lax
from jax.experimental import pallas as pl
from jax.experimental.pallas import tpu as pltpu

D_MODEL = 1024
ROPE_THETA = 500000.0
BLOCK = 128
NEG = -1e30
RMS_EPS = 1e-6
LN_EPS = 1e-5

MLA_HEADS = 8
MLA_NOPE_DIM = 64
MLA_ROPE_DIM = 32
MLA_V_DIM = 64
Q_LORA_RANK = 384
KV_LORA_RANK = 256
MLA_WIDTH = MLA_HEADS * MLA_V_DIM

DIL_HEADS = 8
DIL_HEAD_DIM = 64
DIL_ROT_DIM = DIL_HEAD_DIM // 4
DIL_WIDTH = DIL_HEADS * DIL_HEAD_DIM
DILATIONS = (1, 4, 16)

DEPTH = 1
DEEPNORM_ALPHA = (2.0 * DEPTH) ** 0.25
LOG2_E = 1.4426950408889634

LANES = 128
HEAD_PAIRS = MLA_HEADS // 2
VMEM_LIMIT_BYTES = 56 * 1024 * 1024

_OFF_CQ = 0
_OFF_CKV = _OFF_CQ + Q_LORA_RANK
_OFF_KPE = _OFF_CKV + KV_LORA_RANK
_OFF_GA = _OFF_KPE + LANES
_OFF_QB = _OFF_GA + MLA_WIDTH
_OFF_KB = _OFF_QB + DIL_WIDTH
_OFF_VB = _OFF_KB + DIL_WIDTH
_OFF_GB = _OFF_VB + DIL_WIDTH
_W_BIG = _OFF_GB + DIL_WIDTH

PROJ_ROWS = 512
MLA_KEY_TILE = 256
MLA_TILE = 1024
MLA_ONES_ROWS = 16
OUT_ROWS = 1024
OUT_CHUNK = 256
COMBINE_ROWS = 256
DIL_GROUP = 8


def _rope_lanes(x, cos, sin_fwd, sin_bwd, half):
    fwd = pltpu.roll(x, half, 1)
    bwd = pltpu.roll(x, LANES - half, 1)
    return x * cos + fwd * sin_fwd + bwd * sin_bwd


def _proj_kernel(x_ref, w_ref, wuq_ref, wuk_ref, wuvt_ref, qg_ref, kvg_ref,
                 mcos_ref, msf_ref, msb_ref, dcos_ref, dsf_ref, dsb_ref,
                 qm_ref, km_ref, vt_ref, ga_ref, gb_ref, *dil_and_scratch):
    f32 = jnp.float32
    bf16 = jnp.bfloat16
    xb = x_ref[...].astype(bf16)
    dil_refs = [dil_and_scratch[3 * i:3 * i + 3] for i in range(len(DILATIONS))]
    stage_ref = dil_and_scratch[-1]
    tm = x_ref.shape[0]

    def emit_residues(which, p, val):
        slab = which * HEAD_PAIRS + p
        stage_ref[slab] = val
        for refs, dil in zip(dil_refs, DILATIONS):
            for r in range(dil):
                rows = val if dil == 1 else stage_ref[slab, pl.ds(r, tm // dil, stride=dil), :]
                refs[which][p, 0, r] = rows.astype(bf16)

    def seg(lo, width):
        return jnp.dot(xb, w_ref[:, lo:lo + width], preferred_element_type=f32)

    def rms(t, g):
        return t * lax.rsqrt(jnp.mean(t * t, axis=-1, keepdims=True) + RMS_EPS) * g

    mcos, msf, msb = mcos_ref[...], msf_ref[...], msb_ref[...]
    dcos, dsf, dsb = dcos_ref[...], dsf_ref[...], dsb_ref[...]
    mla_scale = (MLA_NOPE_DIM + MLA_ROPE_DIM) ** -0.5 * LOG2_E
    dil_scale = DIL_HEAD_DIM ** -0.5 * LOG2_E

    def pairs():
        return [slice(p * LANES, (p + 1) * LANES) for p in range(HEAD_PAIRS)]

    t_cq = seg(_OFF_CQ, Q_LORA_RANK)
    t_ckv = seg(_OFF_CKV, KV_LORA_RANK)
    t_qb = seg(_OFF_QB, DIL_WIDTH)
    cq = rms(t_cq, qg_ref[...]).astype(bf16)
    t_qf = jnp.dot(cq, wuq_ref[...], preferred_element_type=f32)
    ckv = rms(t_ckv, kvg_ref[...]).astype(bf16)
    t_kf = jnp.dot(ckv, wuk_ref[...], preferred_element_type=f32)
    for p, sl in enumerate(pairs()):
        emit_residues(0, p, _rope_lanes(t_qb[:, sl], dcos, dsf, dsb, DIL_ROT_DIM // 2) * dil_scale)
    t_kb = seg(_OFF_KB, DIL_WIDTH)
    for h in range(MLA_HEADS):
        blk = t_qf[:, h * LANES:(h + 1) * LANES]
        qm_ref[h] = (_rope_lanes(blk, mcos, msf, msb, MLA_ROPE_DIM // 2) * mla_scale).astype(bf16)
    t_kpe = seg(_OFF_KPE, LANES)
    t_vt = lax.dot_general(wuvt_ref[...], ckv, (((1,), (1,)), ((), ())), preferred_element_type=f32)
    for p, sl in enumerate(pairs()):
        emit_residues(1, p, _rope_lanes(t_kb[:, sl], dcos, dsf, dsb, DIL_ROT_DIM // 2))
    t_vb = seg(_OFF_VB, DIL_WIDTH)
    kpe = _rope_lanes(t_kpe, mcos, msf, msb, MLA_ROPE_DIM // 2)
    for h in range(MLA_HEADS):
        km_ref[h] = (t_kf[:, h * LANES:(h + 1) * LANES] + kpe).astype(bf16)
    t_ga = seg(_OFF_GA, MLA_WIDTH)
    for c in range(vt_ref.shape[0]):
        vt_ref[c] = t_vt[:, c * MLA_KEY_TILE:(c + 1) * MLA_KEY_TILE].astype(bf16)
    for p, sl in enumerate(pairs()):
        emit_residues(2, p, t_vb[:, sl])
    t_gb = seg(_OFF_GB, DIL_WIDTH)
    ga = jax.nn.silu(t_ga)
    for p, sl in enumerate(pairs()):
        ga_ref[p] = ga[:, sl].astype(bf16)
    gb = jax.nn.silu(t_gb)
    for p, sl in enumerate(pairs()):
        gb_ref[p] = gb[:, sl].astype(bf16)


def _mla_kernel(q_ref, k_ref, vt_ref, g_ref, o_ref, accs_ref, s_ref):
    f32 = jnp.float32
    bf16 = jnp.bfloat16
    tq, tk = MLA_TILE, MLA_KEY_TILE
    n_chunks = tq // tk
    assert n_chunks % 2 == 0
    nq = q_ref.shape[1] // tq
    units = [(h, c) for c in range(n_chunks) for h in range(2)]
    ones = jnp.ones((MLA_ONES_ROWS, tk), bf16)
    key = lax.broadcasted_iota(jnp.int32, (tk, tk), 0)
    qry = lax.broadcasted_iota(jnp.int32, (tk, tk), 1)
    lower = key <= qry

    def lanes(c):
        return slice(c * tk, (c + 1) * tk)

    def scores(qi, j, slot, h, c):
        q = q_ref[h, pl.ds(qi * tq + c * tk, tk), :]
        k = k_ref[h, pl.ds(j * tk if isinstance(j, int) else pl.multiple_of(j * tk, tk), tk), :]
        s = lax.dot_general(k, q, (((1,), (1,)), ((), ())), preferred_element_type=f32)
        s_ref[slot, h, :, lanes(c)] = s
        return jnp.max(s, axis=0, keepdims=True)

    def absorb(acc_ref, j, slot, h, c, m_tile, m_old, masked=False):
        s = s_ref[slot, h, :, lanes(c)]
        if masked:
            s = jnp.where(lower, s, NEG)
            m_tile = jnp.max(s, axis=0, keepdims=True)
        m_new = jnp.maximum(m_old, m_tile)
        alpha = jnp.exp2(m_old - m_new)
        p = jnp.exp2(s - m_new).astype(bf16)
        vt = jnp.concatenate([vt_ref[j, h * MLA_V_DIM:(h + 1) * MLA_V_DIM, :], ones], axis=0)
        acc_ref[h, :, lanes(c)] = alpha * acc_ref[h, :, lanes(c)] + jnp.dot(vt, p, preferred_element_type=f32)
        return m_new

    def per_query_tile(qi, acc_ref):
        rows = pl.ds(qi * tq, tq)
        acc_ref[...] = jnp.zeros_like(acc_ref)

        def step(j, slot, tile_max, m_run):
            next_max, m_new = {}, {}
            for u in units:
                next_max[u] = scores(qi, j + 1, 1 - slot, *u)
                m_new[u] = absorb(acc_ref, j, slot, *u, tile_max[u], m_run[u])
            return next_max, m_new

        def body(jj, carry):
            tile_max, m_run = carry
            for i in range(n_chunks):
                tile_max, m_run = step(n_chunks * jj + i, i % 2, tile_max, m_run)
            return tile_max, m_run

        start = {u: jnp.full((1, tk), NEG, f32) for u in units}
        first = {u: scores(qi, 0, 0, *u) for u in units}
        tile_max, m_run = lax.fori_loop(0, qi, body, (first, start))

        base = qi * n_chunks
        for d in range(n_chunks):
            next_max = {}
            for h, c in units:
                if c > d:
                    next_max[h, c] = scores(qi, base + d + 1, (d + 1) % 2, h, c)
                if c >= d:
                    m_run[h, c] = absorb(acc_ref, base + d, d % 2, h, c, tile_max[h, c], m_run[h, c],
                                         masked=(c == d))
            tile_max = next_max

        yt = jnp.concatenate([acc_ref[h, :MLA_V_DIM, :] / acc_ref[h, MLA_V_DIM:MLA_V_DIM + 1, :] for h in range(2)],
                             axis=0)
        o_ref[rows, :] = (yt.T * g_ref[0, rows, :].astype(f32)).astype(o_ref.dtype)

    for qi in range(nq):
        per_query_tile(qi, accs_ref.at[qi % 2])


def _dilated_kernel(q1_ref, k1_ref, v1_ref, q4_ref, k4_ref, v4_ref, q16_ref, k16_ref, v16_ref,
                    g_ref, o_ref, num_ref, max_ref, den_ref, s_ref, p_ref, bias_ref):
    f32 = jnp.float32
    seq = num_ref.shape[1]

    lane = lax.broadcasted_iota(jnp.int32, (BLOCK, LANES), 1)
    low_half = lane < DIL_HEAD_DIM
    ones = jnp.ones((2 * BLOCK, LANES), jnp.bfloat16)
    qi = lax.broadcasted_iota(jnp.int32, (2 * BLOCK, 2 * BLOCK), 0) % BLOCK
    kj = lax.broadcasted_iota(jnp.int32, (2 * BLOCK, 2 * BLOCK), 1)
    dist_first = qi - kj
    dist_later = dist_first + BLOCK
    for kind, dist in enumerate((dist_first, dist_later)):
        bias_ref[kind] = jnp.where((dist >= 0) & (dist <= BLOCK), 0.0, NEG).astype(f32)

    n_groups = seq // BLOCK // DIL_GROUP
    assert n_groups % 2 == 0 and n_groups >= 4

    def branch(idx, dil, q_ref, k_ref, v_ref):
        per_residue = q_ref.shape[3] // BLOCK

        def locate(b):
            return b // per_residue, b % per_residue

        def key_rows(n):
            return pl.ds(pl.multiple_of(jnp.maximum(n - 1, 0) * BLOCK, BLOCK), 2 * BLOCK)

        def out_rows(r, n):
            if dil == 1:
                return pl.ds(pl.multiple_of(n * BLOCK, BLOCK), BLOCK)
            return pl.ds(r + n * (BLOCK * dil), BLOCK, stride=dil)

        def scores(g, slot, i):
            r, n = locate(g * DIL_GROUP + i)
            q = q_ref[0, 0, r, pl.ds(pl.multiple_of(n * BLOCK, BLOCK), BLOCK), :]
            k = k_ref[0, 0, r, key_rows(n), :]
            zero = jnp.zeros_like(q)
            q2 = jnp.concatenate([jnp.where(low_half, q, zero), jnp.where(low_half, zero, q)], axis=0)
            s_ref[slot, i] = lax.dot_general(q2, k, (((1,), (1,)), ((), ())), preferred_element_type=f32)

        def softmax(g, slot, i):
            r, n = locate(g * DIL_GROUP + i)
            s = s_ref[slot, i] + bias_ref[jnp.minimum(n, 1)]
            m = jnp.max(s, axis=1, keepdims=True)
            p_ref[slot, i] = jnp.exp2(s - m).astype(jnp.bfloat16)
            max_ref[idx, out_rows(r, n), :] = jnp.where(low_half, m[:BLOCK], m[BLOCK:])

        def values(g, slot, i):
            r, n = locate(g * DIL_GROUP + i)
            v = jnp.concatenate([v_ref[0, 0, r, key_rows(n), :], ones], axis=1)
            both = jnp.dot(p_ref[slot, i], v, preferred_element_type=f32)
            num, den = both[:, :LANES], both[:, LANES:]
            rows = out_rows(r, n)
            num_ref[idx, rows, :] = jnp.where(low_half, num[:BLOCK], num[BLOCK:])
            den_ref[idx, rows, :] = jnp.where(low_half, den[:BLOCK], den[BLOCK:])

        return scores, softmax, values

    stages = [branch(0, DILATIONS[0], q1_ref, k1_ref, v1_ref),
              branch(1, DILATIONS[1], q4_ref, k4_ref, v4_ref),
              branch(2, DILATIONS[2], q16_ref, k16_ref, v16_ref)]
    last = n_groups - 1

    def trip(*work):
        for stage, g, slot in work:
            for i in range(DIL_GROUP):
                stage(g, slot, i)

    for b, (scores, softmax, values) in enumerate(stages):
        if b == 0:
            trip((scores, 0, 0))
            trip((scores, 1, 1), (softmax, 0, 0))
        else:
            trip((stages[b - 1][2], last, 1), (scores, 1, 1), (softmax, 0, 0))

        def two_trips(t, carry, scores=scores, softmax=softmax, values=values):
            trip((values, 2 * t, 0), (scores, 2 * t + 2, 0), (softmax, 2 * t + 1, 1))
            trip((values, 2 * t + 1, 1), (scores, 2 * t + 3, 1), (softmax, 2 * t + 2, 0))
            return carry

        lax.fori_loop(0, (n_groups - 2) // 2, two_trips, 0)
        if b + 1 < len(stages):
            trip((values, last - 1, 0), (stages[b + 1][0], 0, 0), (softmax, last, 1))
        else:
            trip((values, last - 1, 0), (softmax, last, 1))
    trip((stages[-1][2], last, 1))

    def merge(c, carry):
        rows = pl.ds(pl.multiple_of(c * COMBINE_ROWS, COMBINE_ROWS), COMBINE_ROWS)
        ms = [max_ref[i, rows, :] for i in range(3)]
        m_all = jnp.maximum(jnp.maximum(ms[0], ms[1]), ms[2])
        num = jnp.zeros((COMBINE_ROWS, LANES), f32)
        den = jnp.zeros((COMBINE_ROWS, LANES), f32)
        for i in range(3):
            w = jnp.exp2(ms[i] - m_all)
            num = num + w * num_ref[i, rows, :]
            den = den + w * den_ref[i, rows, :]
        o_ref[rows, :] = (num / den * g_ref[0, rows, :].astype(f32)).astype(o_ref.dtype)
        return carry

    lax.fori_loop(0, seq // COMBINE_ROWS, merge, 0)


def _out_kernel(x_ref, ya_ref, yb_ref, wa_ref, wb_ref, g_ref, b_ref, o_ref):
    f32 = jnp.float32
    chunks = [pl.ds(r, OUT_CHUNK) for r in range(0, x_ref.shape[0], OUT_CHUNK)]

    def project(rows):
        return (jnp.dot(ya_ref[rows, :], wa_ref[...], preferred_element_type=f32)
                + jnp.dot(yb_ref[rows, :], wb_ref[...], preferred_element_type=f32))

    def normalise(rows, mixed):
        h = DEEPNORM_ALPHA * x_ref[rows, :] + mixed
        mu = jnp.mean(h, axis=-1, keepdims=True)
        c = h - mu
        var = jnp.mean(c * c, axis=-1, keepdims=True)
        o_ref[rows, :] = c * lax.rsqrt(var + LN_EPS) * g_ref[...] + b_ref[...]

    mixed = project(chunks[0])
    for i, rows in enumerate(chunks):
        ahead = project(chunks[i + 1]) if i + 1 < len(chunks) else None
        normalise(rows, mixed)
        mixed = ahead


def _rope_tables(seq, rot_dim, period, rot_offset, pass_rest):
    half = rot_dim // 2
    inv_freq = ROPE_THETA ** (-jnp.arange(0, rot_dim, 2, dtype=jnp.float32) / rot_dim)
    ang = jnp.arange(seq, dtype=jnp.float32)[:, None] * inv_freq[None, :]
    cos, sin = jnp.cos(ang), jnp.sin(ang)
    zeros = jnp.zeros((seq, half), jnp.float32)
    rest = period - rot_offset - rot_dim
    fill = jnp.ones if pass_rest else jnp.zeros
    group_cos = jnp.concatenate([jnp.ones((seq, rot_offset), jnp.float32), cos, cos,
                                 fill((seq, rest), jnp.float32)], axis=1)
    group_fwd = jnp.concatenate([jnp.zeros((seq, rot_offset), jnp.float32), zeros, sin,
                                 jnp.zeros((seq, rest), jnp.float32)], axis=1)
    group_bwd = jnp.concatenate([jnp.zeros((seq, rot_offset), jnp.float32), -sin, zeros,
                                 jnp.zeros((seq, rest), jnp.float32)], axis=1)
    reps = LANES // period
    return tuple(jnp.tile(t, (1, reps)) for t in (group_cos, group_fwd, group_bwd))


def _params(*semantics, flags=None):
    return pltpu.CompilerParams(dimension_semantics=semantics, vmem_limit_bytes=VMEM_LIMIT_BYTES, flags=flags)


def kernel(x, w_in, q_norm_g, kv_norm_g, w_uq, w_ukv, w_out, ln_g, ln_b):
    f32, bf16 = jnp.float32, jnp.bfloat16
    batch, seq, _ = x.shape
    rows = batch * seq
    x2 = x.reshape(rows, D_MODEL)

    offs = (0, Q_LORA_RANK, Q_LORA_RANK + KV_LORA_RANK, Q_LORA_RANK + KV_LORA_RANK + MLA_ROPE_DIM)
    kpe_w = jnp.zeros((D_MODEL, LANES), f32).at[:, MLA_NOPE_DIM:MLA_NOPE_DIM + MLA_ROPE_DIM].set(
        w_in[:, offs[2]:offs[3]])
    w_big = jnp.concatenate([w_in[:, :offs[2]], kpe_w, w_in[:, offs[3]:]], axis=1).astype(bf16)
    assert w_big.shape[1] == _W_BIG
    dk = MLA_NOPE_DIM + MLA_ROPE_DIM
    wuq = jnp.pad(w_uq.reshape(Q_LORA_RANK, MLA_HEADS, dk), ((0, 0), (0, 0), (0, LANES - dk)))
    wuq = wuq.reshape(Q_LORA_RANK, MLA_HEADS * LANES).astype(bf16)
    wukv = w_ukv.reshape(KV_LORA_RANK, MLA_HEADS, MLA_NOPE_DIM + MLA_V_DIM)
    wuk = jnp.pad(wukv[:, :, :MLA_NOPE_DIM], ((0, 0), (0, 0), (0, LANES - MLA_NOPE_DIM)))
    wuk = wuk.reshape(KV_LORA_RANK, MLA_HEADS * LANES).astype(bf16)
    wuvt = wukv[:, :, MLA_NOPE_DIM:].reshape(KV_LORA_RANK, MLA_WIDTH).T.astype(bf16)
    wa = w_out[:MLA_WIDTH].astype(bf16)
    wb = w_out[MLA_WIDTH:].astype(bf16)

    mla_tabs = _rope_tables(seq, MLA_ROPE_DIM, LANES, MLA_NOPE_DIM, pass_rest=False)
    dil_tabs = _rope_tables(seq, DIL_ROT_DIM, DIL_HEAD_DIM, 0, pass_rest=True)

    tm = PROJ_ROWS
    seq_tiles = seq // tm
    full = lambda shape: pl.BlockSpec(shape, lambda i: (0,) * len(shape))
    tab = pl.BlockSpec((tm, LANES), lambda i: (i % seq_tiles, 0))
    slab = lambda n: pl.BlockSpec((n, tm, LANES), lambda i: (0, i, 0))
    slab_shape = lambda n: jax.ShapeDtypeStruct((n, rows, LANES), bf16)
    tk = MLA_KEY_TILE
    vt_spec = pl.BlockSpec((tm // tk, MLA_WIDTH, tk), lambda i: (i, 0, 0))
    vt_shape = jax.ShapeDtypeStruct((rows // tk, MLA_WIDTH, tk), bf16)
    res_specs, res_shapes = [], []
    for dil in DILATIONS:
        spec = pl.BlockSpec((HEAD_PAIRS, 1, dil, tm // dil, LANES),
                            lambda i: (0, i // seq_tiles, 0, i % seq_tiles, 0))
        shape = jax.ShapeDtypeStruct((HEAD_PAIRS, batch, dil, seq // dil, LANES), bf16)
        res_specs += [spec] * 3
        res_shapes += [shape] * 3
    qm, km, vt, ga, gb, *dil_in = pl.pallas_call(
        _proj_kernel,
        grid=(rows // tm,),
        in_specs=[pl.BlockSpec((tm, D_MODEL), lambda i: (i, 0)),
                  full((D_MODEL, _W_BIG)), full(wuq.shape), full(wuk.shape), full(wuvt.shape),
                  full((1, Q_LORA_RANK)), full((1, KV_LORA_RANK)),
                  tab, tab, tab, tab, tab, tab],
        out_specs=[slab(MLA_HEADS), slab(MLA_HEADS), vt_spec, slab(HEAD_PAIRS), slab(HEAD_PAIRS)] + res_specs,
        out_shape=[slab_shape(MLA_HEADS), slab_shape(MLA_HEADS), vt_shape, slab_shape(HEAD_PAIRS),
                   slab_shape(HEAD_PAIRS)] + res_shapes,
        scratch_shapes=[pltpu.VMEM((3 * HEAD_PAIRS, tm, LANES), f32)],
        compiler_params=_params("parallel"),
        name="proj",
    )(x2, w_big, wuq, wuk, wuvt, q_norm_g.reshape(1, -1), kv_norm_g.reshape(1, -1), *mla_tabs, *dil_tabs)

    t = MLA_TILE
    ya = pl.pallas_call(
        _mla_kernel,
        grid=(batch, HEAD_PAIRS),
        in_specs=[pl.BlockSpec((2, seq, LANES), lambda b, p: (p, b, 0)),
                  pl.BlockSpec((2, seq, LANES), lambda b, p: (p, b, 0)),
                  pl.BlockSpec((seq // tk, LANES, tk), lambda b, p: (b, p, 0)),
                  pl.BlockSpec((1, seq, LANES), lambda b, p: (p, b, 0))],
        out_specs=pl.BlockSpec((seq, LANES), lambda b, p: (b, p)),
        out_shape=jax.ShapeDtypeStruct((rows, MLA_WIDTH), bf16),
        scratch_shapes=[pltpu.VMEM((2, 2, MLA_V_DIM + MLA_ONES_ROWS, t), f32), pltpu.VMEM((2, 2, tk, t), f32)],
        compiler_params=_params("parallel", "parallel"),
        name="mla",
    )(qm, km, vt, ga)

    dil_specs = [pl.BlockSpec((1, 1, dil, seq // dil, LANES), lambda b, p: (p, b, 0, 0, 0))
                 for dil in DILATIONS for _ in range(3)]
    yb = pl.pallas_call(
        _dilated_kernel,
        grid=(batch, HEAD_PAIRS),
        in_specs=dil_specs + [pl.BlockSpec((1, seq, LANES), lambda b, p: (p, b, 0))],
        out_specs=pl.BlockSpec((seq, LANES), lambda b, p: (b, p)),
        out_shape=jax.ShapeDtypeStruct((rows, DIL_WIDTH), bf16),
        scratch_shapes=[pltpu.VMEM((3, seq, LANES), f32)] * 3 + [
            pltpu.VMEM((2, DIL_GROUP, 2 * BLOCK, 2 * BLOCK), f32),
            pltpu.VMEM((2, DIL_GROUP, 2 * BLOCK, 2 * BLOCK), bf16),
            pltpu.VMEM((2, 2 * BLOCK, 2 * BLOCK), f32)],
        compiler_params=_params("parallel", "parallel"),
        name="dilated",
    )(*dil_in, gb)

    to = OUT_ROWS
    const = lambda shape: pl.BlockSpec(shape, lambda i: (0,) * len(shape))
    out = pl.pallas_call(
        _out_kernel,
        grid=(rows // to,),
        in_specs=[pl.BlockSpec((to, D_MODEL), lambda i: (i, 0)),
                  pl.BlockSpec((to, MLA_WIDTH), lambda i: (i, 0)),
                  pl.BlockSpec((to, DIL_WIDTH), lambda i: (i, 0)),
                  const((MLA_WIDTH, D_MODEL)), const((DIL_WIDTH, D_MODEL)),
                  const((1, D_MODEL)), const((1, D_MODEL))],
        out_specs=pl.BlockSpec((to, D_MODEL), lambda i: (i, 0)),
        out_shape=jax.ShapeDtypeStruct((rows, D_MODEL), f32),
        compiler_params=_params("parallel"),
        name="out",
    )(x2, ya, yb, wa, wb, ln_g.reshape(1, -1), ln_b.reshape(1, -1))
    return out.reshape(batch, seq, D_MODEL)
```

```python
import functools

import jax
import jax.numpy as jnp
import numpy as np
from jax import lax
from jax.experimental import pallas as pl
from jax.experimental.pallas import tpu as pltpu

D_MODEL = 1024
ROPE_THETA = 500000.0
BLOCK = 128
NEG = -1e30
RMS_EPS = 1e-6
LN_EPS = 1e-5

MLA_HEADS = 8
MLA_NOPE_DIM = 64
MLA_ROPE_DIM = 32
MLA_V_DIM = 64
Q_LORA_RANK = 384
KV_LORA_RANK = 256
MLA_WIDTH = MLA_HEADS * MLA_V_DIM

DIL_HEADS = 8
DIL_HEAD_DIM = 64
DIL_ROT_DIM = DIL_HEAD_DIM // 4
DIL_WIDTH = DIL_HEADS * DIL_HEAD_DIM
DILATIONS = (1, 4, 16)

DEPTH = 1
DEEPNORM_ALPHA = (2.0 * DEPTH) ** 0.25
LOG2_E = 1.4426950408889634

LANES = 128
HEAD_PAIRS = MLA_HEADS // 2
VMEM_LIMIT_BYTES = 56 * 1024 * 1024

_LATENT_WIDTH = Q_LORA_RANK + KV_LORA_RANK
_OFF_GA = 0
_OFF_QB = _OFF_GA + MLA_WIDTH
_OFF_KB = _OFF_QB + DIL_WIDTH
_OFF_VB = _OFF_KB + DIL_WIDTH
_OFF_GB = _OFF_VB + DIL_WIDTH
_REST_WIDTH = _OFF_GB + DIL_WIDTH

PROJ_ROWS = 512
MLA_KEY_TILE = 256
MLA_TILE = 1024
MLA_ONES_ROWS = 16
OUT_ROWS = 1024
OUT_CHUNK = 256
COMBINE_ROWS = 256
DIL_GROUP = 8


def _rope_lanes(x, cos, sin_fwd, sin_bwd, half):
    fwd = pltpu.roll(x, half, 1)
    bwd = pltpu.roll(x, LANES - half, 1)
    return x * cos + fwd * sin_fwd + bwd * sin_bwd


def _proj_kernel(x_ref, wlat_ref, wkpe_ref, w_ref, wuq_ref, wuk_ref, wuvt_ref, qg_ref, kvg_ref,
                 mcos_ref, msf_ref, msb_ref, dcos_ref, dsf_ref, dsb_ref,
                 qm_ref, km_ref, vt_ref, ga_ref, gb_ref, *dil_and_scratch):
    f32 = jnp.float32
    bf16 = jnp.bfloat16
    xb = x_ref[...].astype(bf16)
    dil_refs = [dil_and_scratch[3 * i:3 * i + 3] for i in range(len(DILATIONS))]
    stage_ref = dil_and_scratch[-1]
    tm = x_ref.shape[0]

    def emit_residues(which, p, val):
        slab = which * HEAD_PAIRS + p
        stage_ref[slab] = val
        for refs, dil in zip(dil_refs, DILATIONS):
            for r in range(dil):
                rows = val if dil == 1 else stage_ref[slab, pl.ds(r, tm // dil, stride=dil), :]
                refs[which][p, 0, r] = rows.astype(bf16)

    def seg(lo, width):
        return jnp.dot(xb, w_ref[:, lo:lo + width], preferred_element_type=f32)

    def rms(t, g):
        return t * lax.rsqrt(jnp.mean(t * t, axis=-1, keepdims=True) + RMS_EPS) * g

    mcos, msf, msb = mcos_ref[...], msf_ref[...], msb_ref[...]
    dcos, dsf, dsb = dcos_ref[...], dsf_ref[...], dsb_ref[...]
    mla_scale = (MLA_NOPE_DIM + MLA_ROPE_DIM) ** -0.5 * LOG2_E
    dil_scale = DIL_HEAD_DIM ** -0.5 * LOG2_E

    def pairs():
        return [slice(p * LANES, (p + 1) * LANES) for p in range(HEAD_PAIRS)]

    t_cq = jnp.dot(xb, wlat_ref[:, :Q_LORA_RANK], preferred_element_type=f32)
    t_ckv = jnp.dot(xb, wlat_ref[:, Q_LORA_RANK:], preferred_element_type=f32)
    t_qb = seg(_OFF_QB, DIL_WIDTH)
    cq = rms(t_cq, qg_ref[...]).astype(bf16)
    t_qf = jnp.dot(cq, wuq_ref[...], preferred_element_type=f32)
    ckv = rms(t_ckv, kvg_ref[...]).astype(bf16)
    t_kf = jnp.dot(ckv, wuk_ref[...], preferred_element_type=f32)
    for p, sl in enumerate(pairs()):
        emit_residues(0, p, _rope_lanes(t_qb[:, sl], dcos, dsf, dsb, DIL_ROT_DIM // 2) * dil_scale)
    t_kb = seg(_OFF_KB, DIL_WIDTH)
    for h in range(MLA_HEADS):
        blk = t_qf[:, h * LANES:(h + 1) * LANES]
        qm_ref[h] = (_rope_lanes(blk, mcos, msf, msb, MLA_ROPE_DIM // 2) * mla_scale).astype(bf16)
    t_kpe = jnp.dot(xb, wkpe_ref[...], preferred_element_type=f32)
    t_vt = lax.dot_general(wuvt_ref[...], ckv, (((1,), (1,)), ((), ())), preferred_element_type=f32)
    for p, sl in enumerate(pairs()):
        emit_residues(1, p, _rope_lanes(t_kb[:, sl], dcos, dsf, dsb, DIL_ROT_DIM // 2))
    t_vb = seg(_OFF_VB, DIL_WIDTH)
    kpe = _rope_lanes(t_kpe, mcos, msf, msb, MLA_ROPE_DIM // 2)
    for h in range(MLA_HEADS):
        km_ref[h] = (t_kf[:, h * LANES:(h + 1) * LANES] + kpe).astype(bf16)
    t_ga = seg(_OFF_GA, MLA_WIDTH)
    for c in range(vt_ref.shape[0]):
        vt_ref[c] = t_vt[:, c * MLA_KEY_TILE:(c + 1) * MLA_KEY_TILE].astype(bf16)
    for p, sl in enumerate(pairs()):
        emit_residues(2, p, t_vb[:, sl])
    t_gb = seg(_OFF_GB, DIL_WIDTH)
    ga = jax.nn.silu(t_ga)
    for p, sl in enumerate(pairs()):
        ga_ref[p] = ga[:, sl].astype(bf16)
    gb = jax.nn.silu(t_gb)
    for p, sl in enumerate(pairs()):
        gb_ref[p] = gb[:, sl].astype(bf16)


def _mla_kernel(q_ref, k_ref, vt_ref, g_ref, o_ref, accs_ref, s_ref):
    f32 = jnp.float32
    bf16 = jnp.bfloat16
    tq, tk = MLA_TILE, MLA_KEY_TILE
    n_chunks = tq // tk
    assert n_chunks % 2 == 0
    nq = q_ref.shape[1] // tq
    units = [(h, c) for c in range(n_chunks) for h in range(2)]
    ones = jnp.ones((MLA_ONES_ROWS, tk), bf16)
    key = lax.broadcasted_iota(jnp.int32, (tk, tk), 0)
    qry = lax.broadcasted_iota(jnp.int32, (tk, tk), 1)
    lower = key <= qry

    def lanes(c):
        return slice(c * tk, (c + 1) * tk)

    def scores(qi, j, slot, h, c):
        q = q_ref[h, pl.ds(qi * tq + c * tk, tk), :]
        k = k_ref[h, pl.ds(j * tk if isinstance(j, int) else pl.multiple_of(j * tk, tk), tk), :]
        s = lax.dot_general(k, q, (((1,), (1,)), ((), ())), preferred_element_type=f32)
        s_ref[slot, h, :, lanes(c)] = s
        return jnp.max(s, axis=0, keepdims=True)

    def absorb(acc_ref, j, slot, h, c, m_tile, m_old, masked=False):
        s = s_ref[slot, h, :, lanes(c)]
        if masked:
            s = jnp.where(lower, s, NEG)
            m_tile = jnp.max(s, axis=0, keepdims=True)
        m_new = jnp.maximum(m_old, m_tile)
        alpha = jnp.exp2(m_old - m_new)
        p = jnp.exp2(s - m_new).astype(bf16)
        vt = jnp.concatenate([vt_ref[j, h * MLA_V_DIM:(h + 1) * MLA_V_DIM, :], ones], axis=0)
        acc_ref[h, :, lanes(c)] = alpha * acc_ref[h, :, lanes(c)] + jnp.dot(vt, p, preferred_element_type=f32)
        return m_new

    def per_query_tile(qi, acc_ref):
        rows = pl.ds(qi * tq, tq)
        acc_ref[...] = jnp.zeros_like(acc_ref)

        def step(j, slot, tile_max, m_run):
            next_max, m_new = {}, {}
            for u in units:
                next_max[u] = scores(qi, j + 1, 1 - slot, *u)
                m_new[u] = absorb(acc_ref, j, slot, *u, tile_max[u], m_run[u])
            return next_max, m_new

        def body(jj, carry):
            tile_max, m_run = carry
            for i in range(n_chunks):
                tile_max, m_run = step(n_chunks * jj + i, i % 2, tile_max, m_run)
            return tile_max, m_run

        start = {u: jnp.full((1, tk), NEG, f32) for u in units}
        first = {u: scores(qi, 0, 0, *u) for u in units}
        tile_max, m_run = lax.fori_loop(0, qi, body, (first, start))

        base = qi * n_chunks
        for d in range(n_chunks):
            next_max = {}
            for h, c in units:
                if c > d:
                    next_max[h, c] = scores(qi, base + d + 1, (d + 1) % 2, h, c)
                if c >= d:
                    m_run[h, c] = absorb(acc_ref, base + d, d % 2, h, c, tile_max[h, c], m_run[h, c],
                                         masked=(c == d))
            tile_max = next_max

        yt = jnp.concatenate([acc_ref[h, :MLA_V_DIM, :] / acc_ref[h, MLA_V_DIM:MLA_V_DIM + 1, :] for h in range(2)],
                             axis=0)
        o_ref[rows, :] = (yt.T * g_ref[0, rows, :].astype(f32)).astype(o_ref.dtype)

    for qi in range(nq):
        per_query_tile(qi, accs_ref.at[qi % 2])


def _dilated_kernel(q1_ref, k1_ref, v1_ref, q4_ref, k4_ref, v4_ref, q16_ref, k16_ref, v16_ref,
                    g_ref, o_ref, num_ref, max_ref, den_ref, s_ref, p_ref, bias_ref):
    f32 = jnp.float32
    seq = num_ref.shape[1]

    lane = lax.broadcasted_iota(jnp.int32, (BLOCK, LANES), 1)
    low_half = lane < DIL_HEAD_DIM
    ones = jnp.ones((2 * BLOCK, LANES), jnp.bfloat16)
    qi = lax.broadcasted_iota(jnp.int32, (2 * BLOCK, 2 * BLOCK), 0) % BLOCK
    kj = lax.broadcasted_iota(jnp.int32, (2 * BLOCK, 2 * BLOCK), 1)
    dist_first = qi - kj
    dist_later = dist_first + BLOCK
    for kind, dist in enumerate((dist_first, dist_later)):
        bias_ref[kind] = jnp.where((dist >= 0) & (dist <= BLOCK), 0.0, NEG).astype(f32)

    n_groups = seq // BLOCK // DIL_GROUP
    assert n_groups % 2 == 0 and n_groups >= 4

    def branch(idx, dil, q_ref, k_ref, v_ref):
        per_residue = q_ref.shape[3] // BLOCK

        def locate(b):
            return b // per_residue, b % per_residue

        def key_rows(n):
            return pl.ds(pl.multiple_of(jnp.maximum(n - 1, 0) * BLOCK, BLOCK), 2 * BLOCK)

        def out_rows(r, n):
            if dil == 1:
                return pl.ds(pl.multiple_of(n * BLOCK, BLOCK), BLOCK)
            return pl.ds(r + n * (BLOCK * dil), BLOCK, stride=dil)

        def scores(g, slot, i):
            r, n = locate(g * DIL_GROUP + i)
            q = q_ref[0, 0, r, pl.ds(pl.multiple_of(n * BLOCK, BLOCK), BLOCK), :]
            k = k_ref[0, 0, r, key_rows(n), :]
            zero = jnp.zeros_like(q)
            q2 = jnp.concatenate([jnp.where(low_half, q, zero), jnp.where(low_half, zero, q)], axis=0)
            s_ref[slot, i] = lax.dot_general(q2, k, (((1,), (1,)), ((), ())), preferred_element_type=f32)

        def softmax(g, slot, i):
            r, n = locate(g * DIL_GROUP + i)
            s = s_ref[slot, i] + bias_ref[jnp.minimum(n, 1)]
            m = jnp.max(s, axis=1, keepdims=True)
            p_ref[slot, i] = jnp.exp2(s - m).astype(jnp.bfloat16)
            max_ref[idx, out_rows(r, n), :] = jnp.where(low_half, m[:BLOCK], m[BLOCK:])

        def values(g, slot, i):
            r, n = locate(g * DIL_GROUP + i)
            v = jnp.concatenate([v_ref[0, 0, r, key_rows(n), :], ones], axis=1)
            both = jnp.dot(p_ref[slot, i], v, preferred_element_type=f32)
            num, den = both[:, :LANES], both[:, LANES:]
            rows = out_rows(r, n)
            num_ref[idx, rows, :] = jnp.where(low_half, num[:BLOCK], num[BLOCK:])
            den_ref[idx, rows, :] = jnp.where(low_half, den[:BLOCK], den[BLOCK:])

        return scores, softmax, values

    stages = [branch(0, DILATIONS[0], q1_ref, k1_ref, v1_ref),
              branch(1, DILATIONS[1], q4_ref, k4_ref, v4_ref),
              branch(2, DILATIONS[2], q16_ref, k16_ref, v16_ref)]
    last = n_groups - 1

    def trip(*work):
        for stage, g, slot in work:
            for i in range(DIL_GROUP):
                stage(g, slot, i)

    for b, (scores, softmax, values) in enumerate(stages):
        if b == 0:
            trip((scores, 0, 0))
            trip((scores, 1, 1), (softmax, 0, 0))
        else:
            trip((stages[b - 1][2], last, 1), (scores, 1, 1), (softmax, 0, 0))

        def two_trips(t, carry, scores=scores, softmax=softmax, values=values):
            trip((values, 2 * t, 0), (scores, 2 * t + 2, 0), (softmax, 2 * t + 1, 1))
            trip((values, 2 * t + 1, 1), (scores, 2 * t + 3, 1), (softmax, 2 * t + 2, 0))
            return carry

        lax.fori_loop(0, (n_groups - 2) // 2, two_trips, 0)
        if b + 1 < len(stages):
            trip((values, last - 1, 0), (stages[b + 1][0], 0, 0), (softmax, last, 1))
        else:
            trip((values, last - 1, 0), (softmax, last, 1))
    trip((stages[-1][2], last, 1))

    def merge(c, carry):
        rows = pl.ds(pl.multiple_of(c * COMBINE_ROWS, COMBINE_ROWS), COMBINE_ROWS)
        ms = [max_ref[i, rows, :] for i in range(3)]
        m_all = jnp.maximum(jnp.maximum(ms[0], ms[1]), ms[2])
        num = jnp.zeros((COMBINE_ROWS, LANES), f32)
        den = jnp.zeros((COMBINE_ROWS, LANES), f32)
        for i in range(3):
            w = jnp.exp2(ms[i] - m_all)
            num = num + w * num_ref[i, rows, :]
            den = den + w * den_ref[i, rows, :]
        o_ref[rows, :] = (num / den * g_ref[0, rows, :].astype(f32)).astype(o_ref.dtype)
        return carry

    lax.fori_loop(0, seq // COMBINE_ROWS, merge, 0)


def _out_kernel(x_ref, ya_ref, yb_ref, wa_ref, wb_ref, g_ref, b_ref, o_ref):
    f32 = jnp.float32
    chunks = [pl.ds(r, OUT_CHUNK) for r in range(0, x_ref.shape[0], OUT_CHUNK)]

    def project(rows):
        return (jnp.dot(ya_ref[rows, :], wa_ref[...], preferred_element_type=f32)
                + jnp.dot(yb_ref[rows, :], wb_ref[...], preferred_element_type=f32))

    def normalise(rows, mixed):
        h = DEEPNORM_ALPHA * x_ref[rows, :] + mixed
        mu = jnp.mean(h, axis=-1, keepdims=True)
        c = h - mu
        var = jnp.mean(c * c, axis=-1, keepdims=True)
        o_ref[rows, :] = c * lax.rsqrt(var + LN_EPS) * g_ref[...] + b_ref[...]

    mixed = project(chunks[0])
    for i, rows in enumerate(chunks):
        ahead = project(chunks[i + 1]) if i + 1 < len(chunks) else None
        normalise(rows, mixed)
        mixed = ahead


def _rope_tables(seq, rot_dim, period, rot_offset, pass_rest):
    half = rot_dim // 2
    f32 = np.float32
    inv_freq = ROPE_THETA ** (-np.arange(0, rot_dim, 2, dtype=np.float64) / rot_dim)
    ang = np.arange(seq, dtype=np.float64)[:, None] * inv_freq[None, :]
    cos, sin = np.cos(ang).astype(f32), np.sin(ang).astype(f32)
    zeros = np.zeros((seq, half), f32)
    rest = period - rot_offset - rot_dim
    fill = np.ones if pass_rest else np.zeros
    group_cos = np.concatenate([np.ones((seq, rot_offset), f32), cos, cos, fill((seq, rest), f32)], axis=1)
    group_fwd = np.concatenate([np.zeros((seq, rot_offset), f32), zeros, sin, np.zeros((seq, rest), f32)], axis=1)
    group_bwd = np.concatenate([np.zeros((seq, rot_offset), f32), -sin, zeros, np.zeros((seq, rest), f32)], axis=1)
    reps = LANES // period
    return tuple(jnp.asarray(np.tile(t, (1, reps))) for t in (group_cos, group_fwd, group_bwd))


def _params(*semantics, flags=None):
    return pltpu.CompilerParams(dimension_semantics=semantics, vmem_limit_bytes=VMEM_LIMIT_BYTES, flags=flags)


def kernel(x, w_in, q_norm_g, kv_norm_g, w_uq, w_ukv, w_out, ln_g, ln_b):
    f32, bf16 = jnp.float32, jnp.bfloat16
    batch, seq, _ = x.shape
    rows = batch * seq
    x2 = x.reshape(rows, D_MODEL)

    rope_end = _LATENT_WIDTH + MLA_ROPE_DIM
    w_lat = w_in[:, :_LATENT_WIDTH].astype(bf16)
    w_kpe = jnp.pad(w_in[:, _LATENT_WIDTH:rope_end].astype(bf16),
                    ((0, 0), (MLA_NOPE_DIM, LANES - MLA_NOPE_DIM - MLA_ROPE_DIM)))
    w_rest = w_in[:, rope_end:].astype(bf16)
    assert w_rest.shape[1] == _REST_WIDTH
    dk = MLA_NOPE_DIM + MLA_ROPE_DIM
    wuq = jnp.pad(w_uq.reshape(Q_LORA_RANK, MLA_HEADS, dk), ((0, 0), (0, 0), (0, LANES - dk)))
    wuq = wuq.reshape(Q_LORA_RANK, MLA_HEADS * LANES).astype(bf16)
    wukv = w_ukv.reshape(KV_LORA_RANK, MLA_HEADS, MLA_NOPE_DIM + MLA_V_DIM)
    wuk = jnp.pad(wukv[:, :, :MLA_NOPE_DIM], ((0, 0), (0, 0), (0, LANES - MLA_NOPE_DIM)))
    wuk = wuk.reshape(KV_LORA_RANK, MLA_HEADS * LANES).astype(bf16)
    wuvt = wukv[:, :, MLA_NOPE_DIM:].reshape(KV_LORA_RANK, MLA_WIDTH).T.astype(bf16)
    wa = w_out[:MLA_WIDTH].astype(bf16)
    wb = w_out[MLA_WIDTH:].astype(bf16)

    mla_tabs = _rope_tables(seq, MLA_ROPE_DIM, LANES, MLA_NOPE_DIM, pass_rest=False)
    dil_tabs = _rope_tables(seq, DIL_ROT_DIM, DIL_HEAD_DIM, 0, pass_rest=True)

    tm = PROJ_ROWS
    seq_tiles = seq // tm
    full = lambda shape: pl.BlockSpec(shape, lambda i: (0,) * len(shape))
    tab = pl.BlockSpec((tm, LANES), lambda i: (i % seq_tiles, 0))
    slab = lambda n: pl.BlockSpec((n, tm, LANES), lambda i: (0, i, 0))
    slab_shape = lambda n: jax.ShapeDtypeStruct((n, rows, LANES), bf16)
    tk = MLA_KEY_TILE
    vt_spec = pl.BlockSpec((tm // tk, MLA_WIDTH, tk), lambda i: (i, 0, 0))
    vt_shape = jax.ShapeDtypeStruct((rows // tk, MLA_WIDTH, tk), bf16)
    res_specs, res_shapes = [], []
    for dil in DILATIONS:
        spec = pl.BlockSpec((HEAD_PAIRS, 1, dil, tm // dil, LANES),
                            lambda i: (0, i // seq_tiles, 0, i % seq_tiles, 0))
        shape = jax.ShapeDtypeStruct((HEAD_PAIRS, batch, dil, seq // dil, LANES), bf16)
        res_specs += [spec] * 3
        res_shapes += [shape] * 3
    qm, km, vt, ga, gb, *dil_in = pl.pallas_call(
        _proj_kernel,
        grid=(rows // tm,),
        in_specs=[pl.BlockSpec((tm, D_MODEL), lambda i: (i, 0)),
                  full(w_lat.shape), full(w_kpe.shape), full(w_rest.shape),
                  full(wuq.shape), full(wuk.shape), full(wuvt.shape),
                  full((1, Q_LORA_RANK)), full((1, KV_LORA_RANK)),
                  tab, tab, tab, tab, tab, tab],
        out_specs=[slab(MLA_HEADS), slab(MLA_HEADS), vt_spec, slab(HEAD_PAIRS), slab(HEAD_PAIRS)] + res_specs,
        out_shape=[slab_shape(MLA_HEADS), slab_shape(MLA_HEADS), vt_shape, slab_shape(HEAD_PAIRS),
                   slab_shape(HEAD_PAIRS)] + res_shapes,
        scratch_shapes=[pltpu.VMEM((3 * HEAD_PAIRS, tm, LANES), f32)],
        compiler_params=_params("parallel"),
        name="proj",
    )(x2, w_lat, w_kpe, w_rest, wuq, wuk, wuvt, q_norm_g.reshape(1, -1), kv_norm_g.reshape(1, -1),
      *mla_tabs, *dil_tabs)

    t = MLA_TILE
    ya = pl.pallas_call(
        _mla_kernel,
        grid=(batch, HEAD_PAIRS),
        in_specs=[pl.BlockSpec((2, seq, LANES), lambda b, p: (p, b, 0)),
                  pl.BlockSpec((2, seq, LANES), lambda b, p: (p, b, 0)),
                  pl.BlockSpec((seq // tk, LANES, tk), lambda b, p: (b, p, 0)),
                  pl.BlockSpec((1, seq, LANES), lambda b, p: (p, b, 0))],
        out_specs=pl.BlockSpec((seq, LANES), lambda b, p: (b, p)),
        out_shape=jax.ShapeDtypeStruct((rows, MLA_WIDTH), bf16),
        scratch_shapes=[pltpu.VMEM((2, 2, MLA_V_DIM + MLA_ONES_ROWS, t), f32), pltpu.VMEM((2, 2, tk, t), f32)],
        compiler_params=_params("parallel", "parallel"),
        name="mla",
    )(qm, km, vt, ga)

    dil_specs = [pl.BlockSpec((1, 1, dil, seq // dil, LANES), lambda b, p: (p, b, 0, 0, 0))
                 for dil in DILATIONS for _ in range(3)]
    yb = pl.pallas_call(
        _dilated_kernel,
        grid=(batch, HEAD_PAIRS),
        in_specs=dil_specs + [pl.BlockSpec((1, seq, LANES), lambda b, p: (p, b, 0))],
        out_specs=pl.BlockSpec((seq, LANES), lambda b, p: (b, p)),
        out_shape=jax.ShapeDtypeStruct((rows, DIL_WIDTH), bf16),
        scratch_shapes=[pltpu.VMEM((3, seq, LANES), f32)] * 3 + [
            pltpu.VMEM((2, DIL_GROUP, 2 * BLOCK, 2 * BLOCK), f32),
            pltpu.VMEM((2, DIL_GROUP, 2 * BLOCK, 2 * BLOCK), bf16),
            pltpu.VMEM((2, 2 * BLOCK, 2 * BLOCK), f32)],
        compiler_params=_params("parallel", "parallel"),
        name="dilated",
    )(*dil_in, gb)

    to = OUT_ROWS
    const = lambda shape: pl.BlockSpec(shape, lambda i: (0,) * len(shape))
    out = pl.pallas_call(
        _out_kernel,
        grid=(rows // to,),
        in_specs=[pl.BlockSpec((to, D_MODEL), lambda i: (i, 0)),
                  pl.BlockSpec((to, MLA_WIDTH), lambda i: (i, 0)),
                  pl.BlockSpec((to, DIL_WIDTH), lambda i: (i, 0)),
                  const((MLA_WIDTH, D_MODEL)), const((DIL_WIDTH, D_MODEL)),
                  const((1, D_MODEL)), const((1, D_MODEL))],
        out_specs=pl.BlockSpec((to, D_MODEL), lambda i: (i, 0)),
        out_shape=jax.ShapeDtypeStruct((rows, D_MODEL), f32),
        compiler_params=_params("parallel"),
        name="out",
    )(x2, ya, yb, wa, wb, ln_g.reshape(1, -1), ln_b.reshape(1, -1))
    return out.reshape(batch, seq, D_MODEL)
```

```python
import functools

import jax
import jax.numpy as jnp
import numpy as np
from jax import lax
from jax.experimental import pallas as pl
from jax.experimental.pallas import tpu as pltpu

D_MODEL = 1024
ROPE_THETA = 500000.0
BLOCK = 128
NEG = -1e30
RMS_EPS = 1e-6
LN_EPS = 1e-5

MLA_HEADS = 8
MLA_NOPE_DIM = 64
MLA_ROPE_DIM = 32
MLA_V_DIM = 64
Q_LORA_RANK = 384
KV_LORA_RANK = 256
MLA_WIDTH = MLA_HEADS * MLA_V_DIM

DIL_HEADS = 8
DIL_HEAD_DIM = 64
DIL_ROT_DIM = DIL_HEAD_DIM // 4
DIL_WIDTH = DIL_HEADS * DIL_HEAD_DIM
DILATIONS = (1, 4, 16)

DEPTH = 1
DEEPNORM_ALPHA = (2.0 * DEPTH) ** 0.25
LOG2_E = 1.4426950408889634

LANES = 128
HEAD_PAIRS = MLA_HEADS // 2
VMEM_LIMIT_BYTES = 56 * 1024 * 1024

_LATENT_WIDTH = Q_LORA_RANK + KV_LORA_RANK
_OFF_GA = 0
_OFF_QB = _OFF_GA + MLA_WIDTH
_OFF_KB = _OFF_QB + DIL_WIDTH
_OFF_VB = _OFF_KB + DIL_WIDTH
_OFF_GB = _OFF_VB + DIL_WIDTH
_REST_WIDTH = _OFF_GB + DIL_WIDTH

PROJ_ROWS = 512
MLA_KEY_TILE = 256
MLA_TILE = 1024
MLA_ONES_ROWS = 16
OUT_ROWS = 1024
OUT_CHUNK = 256
DIL_GROUP = 8


def _rope_lanes(x, cos, sin_fwd, sin_bwd, half):
    fwd = pltpu.roll(x, half, 1)
    bwd = pltpu.roll(x, LANES - half, 1)
    return x * cos + fwd * sin_fwd + bwd * sin_bwd


def _proj_kernel(x_ref, wlat_ref, wkpe_ref, w_ref, wuq_ref, wuk_ref, wuvt_ref, qg_ref, kvg_ref,
                 mcos_ref, msf_ref, msb_ref, dcos_ref, dsf_ref, dsb_ref,
                 qm_ref, km_ref, vt_ref, ga_ref, gb_ref, *dil_and_scratch):
    f32 = jnp.float32
    bf16 = jnp.bfloat16
    xb = x_ref[...].astype(bf16)
    dil_refs = [dil_and_scratch[3 * i:3 * i + 3] for i in range(len(DILATIONS))]
    stage_ref = dil_and_scratch[-1]
    tm = x_ref.shape[0]

    def emit_residues(which, p, val):
        slab = which * HEAD_PAIRS + p
        stage_ref[slab] = val
        for refs, dil in zip(dil_refs, DILATIONS):
            for r in range(dil):
                rows = val if dil == 1 else stage_ref[slab, pl.ds(r, tm // dil, stride=dil), :]
                refs[which][p, 0, r] = rows.astype(bf16)

    def seg(lo, width):
        return jnp.dot(xb, w_ref[:, lo:lo + width], preferred_element_type=f32)

    def rms(t, g):
        return t * lax.rsqrt(jnp.mean(t * t, axis=-1, keepdims=True) + RMS_EPS) * g

    mcos, msf, msb = mcos_ref[...], msf_ref[...], msb_ref[...]
    dcos, dsf, dsb = dcos_ref[...], dsf_ref[...], dsb_ref[...]
    mla_scale = (MLA_NOPE_DIM + MLA_ROPE_DIM) ** -0.5 * LOG2_E
    dil_scale = DIL_HEAD_DIM ** -0.5 * LOG2_E

    def pairs():
        return [slice(p * LANES, (p + 1) * LANES) for p in range(HEAD_PAIRS)]

    t_cq = jnp.dot(xb, wlat_ref[:, :Q_LORA_RANK], preferred_element_type=f32)
    t_ckv = jnp.dot(xb, wlat_ref[:, Q_LORA_RANK:], preferred_element_type=f32)
    t_qb = seg(_OFF_QB, DIL_WIDTH)
    cq = rms(t_cq, qg_ref[...]).astype(bf16)
    t_qf = jnp.dot(cq, wuq_ref[...], preferred_element_type=f32)
    ckv = rms(t_ckv, kvg_ref[...]).astype(bf16)
    t_kf = jnp.dot(ckv, wuk_ref[...], preferred_element_type=f32)
    for p, sl in enumerate(pairs()):
        emit_residues(0, p, _rope_lanes(t_qb[:, sl], dcos, dsf, dsb, DIL_ROT_DIM // 2) * dil_scale)
    t_kb = seg(_OFF_KB, DIL_WIDTH)
    for h in range(MLA_HEADS):
        blk = t_qf[:, h * LANES:(h + 1) * LANES]
        qm_ref[h] = (_rope_lanes(blk, mcos, msf, msb, MLA_ROPE_DIM // 2) * mla_scale).astype(bf16)
    t_kpe = jnp.dot(xb, wkpe_ref[...], preferred_element_type=f32)
    t_vt = lax.dot_general(wuvt_ref[...], ckv, (((1,), (1,)), ((), ())), preferred_element_type=f32)
    for p, sl in enumerate(pairs()):
        emit_residues(1, p, _rope_lanes(t_kb[:, sl], dcos, dsf, dsb, DIL_ROT_DIM // 2))
    t_vb = seg(_OFF_VB, DIL_WIDTH)
    kpe = _rope_lanes(t_kpe, mcos, msf, msb, MLA_ROPE_DIM // 2)
    for h in range(MLA_HEADS):
        km_ref[h] = (t_kf[:, h * LANES:(h + 1) * LANES] + kpe).astype(bf16)
    t_ga = seg(_OFF_GA, MLA_WIDTH)
    for c in range(vt_ref.shape[0]):
        vt_ref[c] = t_vt[:, c * MLA_KEY_TILE:(c + 1) * MLA_KEY_TILE].astype(bf16)
    for p, sl in enumerate(pairs()):
        emit_residues(2, p, t_vb[:, sl])
    t_gb = seg(_OFF_GB, DIL_WIDTH)
    ga = jax.nn.silu(t_ga)
    for p, sl in enumerate(pairs()):
        ga_ref[p] = ga[:, sl].astype(bf16)
    gb = jax.nn.silu(t_gb)
    for p, sl in enumerate(pairs()):
        gb_ref[p] = gb[:, sl].astype(bf16)


def _mla_kernel(q_ref, k_ref, vt_ref, g_ref, o_ref, accs_ref, s_ref):
    f32 = jnp.float32
    bf16 = jnp.bfloat16
    tq, tk = MLA_TILE, MLA_KEY_TILE
    n_chunks = tq // tk
    assert n_chunks % 2 == 0
    nq = q_ref.shape[1] // tq
    units = [(h, c) for c in range(n_chunks) for h in range(2)]
    ones = jnp.ones((MLA_ONES_ROWS, tk), bf16)
    key = lax.broadcasted_iota(jnp.int32, (tk, tk), 0)
    qry = lax.broadcasted_iota(jnp.int32, (tk, tk), 1)
    lower = key <= qry

    def lanes(c):
        return slice(c * tk, (c + 1) * tk)

    def scores(qi, j, slot, h, c):
        q = q_ref[h, pl.ds(qi * tq + c * tk, tk), :]
        k = k_ref[h, pl.ds(j * tk if isinstance(j, int) else pl.multiple_of(j * tk, tk), tk), :]
        s = lax.dot_general(k, q, (((1,), (1,)), ((), ())), preferred_element_type=f32)
        s_ref[slot, h, :, lanes(c)] = s
        return jnp.max(s, axis=0, keepdims=True)

    def absorb(acc_ref, j, slot, h, c, m_tile, m_old, masked=False):
        s = s_ref[slot, h, :, lanes(c)]
        if masked:
            s = jnp.where(lower, s, NEG)
            m_tile = jnp.max(s, axis=0, keepdims=True)
        m_new = jnp.maximum(m_old, m_tile)
        alpha = jnp.exp2(m_old - m_new)
        p = jnp.exp2(s - m_new).astype(bf16)
        vt = jnp.concatenate([vt_ref[j, h * MLA_V_DIM:(h + 1) * MLA_V_DIM, :], ones], axis=0)
        acc_ref[h, :, lanes(c)] = alpha * acc_ref[h, :, lanes(c)] + jnp.dot(vt, p, preferred_element_type=f32)
        return m_new

    def per_query_tile(qi, acc_ref):
        rows = pl.ds(qi * tq, tq)
        acc_ref[...] = jnp.zeros_like(acc_ref)

        def step(j, slot, tile_max, m_run):
            next_max, m_new = {}, {}
            for u in units:
                next_max[u] = scores(qi, j + 1, 1 - slot, *u)
                m_new[u] = absorb(acc_ref, j, slot, *u, tile_max[u], m_run[u])
            return next_max, m_new

        def body(jj, carry):
            tile_max, m_run = carry
            for i in range(n_chunks):
                tile_max, m_run = step(n_chunks * jj + i, i % 2, tile_max, m_run)
            return tile_max, m_run

        start = {u: jnp.full((1, tk), NEG, f32) for u in units}
        first = {u: scores(qi, 0, 0, *u) for u in units}
        tile_max, m_run = lax.fori_loop(0, qi, body, (first, start))

        base = qi * n_chunks
        for d in range(n_chunks):
            next_max = {}
            for h, c in units:
                if c > d:
                    next_max[h, c] = scores(qi, base + d + 1, (d + 1) % 2, h, c)
                if c >= d:
                    m_run[h, c] = absorb(acc_ref, base + d, d % 2, h, c, tile_max[h, c], m_run[h, c],
                                         masked=(c == d))
            tile_max = next_max

        yt = jnp.concatenate([acc_ref[h, :MLA_V_DIM, :] / acc_ref[h, MLA_V_DIM:MLA_V_DIM + 1, :] for h in range(2)],
                             axis=0)
        o_ref[rows, :] = (yt.T * g_ref[0, rows, :].astype(f32)).astype(o_ref.dtype)

    for qi in range(nq):
        per_query_tile(qi, accs_ref.at[qi % 2])


def _dilated_kernel(q1_ref, k1_ref, v1_ref, q4_ref, k4_ref, v4_ref, q16_ref, k16_ref, v16_ref,
                    g_ref, o_ref, num_ref, max_ref, den_ref, s_ref, p_ref, bias_ref):
    f32 = jnp.float32
    seq = num_ref.shape[1]

    lane = lax.broadcasted_iota(jnp.int32, (BLOCK, LANES), 1)
    low_half = lane < DIL_HEAD_DIM
    ones = jnp.ones((2 * BLOCK, LANES), jnp.bfloat16)
    qi = lax.broadcasted_iota(jnp.int32, (2 * BLOCK, 2 * BLOCK), 0) % BLOCK
    kj = lax.broadcasted_iota(jnp.int32, (2 * BLOCK, 2 * BLOCK), 1)
    dist_first = qi - kj
    dist_later = dist_first + BLOCK
    for kind, dist in enumerate((dist_first, dist_later)):
        bias_ref[kind] = jnp.where((dist >= 0) & (dist <= BLOCK), 0.0, NEG).astype(f32)

    n_groups = seq // BLOCK // DIL_GROUP
    assert n_groups % 2 == 0 and n_groups >= 4

    def branch(idx, dil, q_ref, k_ref, v_ref):
        per_residue = q_ref.shape[3] // BLOCK

        def locate(b):
            return b // per_residue, b % per_residue

        def key_rows(n):
            return pl.ds(pl.multiple_of(jnp.maximum(n - 1, 0) * BLOCK, BLOCK), 2 * BLOCK)

        def out_rows(r, n):
            if dil == 1:
                return pl.ds(pl.multiple_of(n * BLOCK, BLOCK), BLOCK)
            return pl.ds(r + n * (BLOCK * dil), BLOCK, stride=dil)

        def scores(g, slot, i):
            r, n = locate(g * DIL_GROUP + i)
            q = q_ref[0, 0, r, pl.ds(pl.multiple_of(n * BLOCK, BLOCK), BLOCK), :]
            k = k_ref[0, 0, r, key_rows(n), :]
            zero = jnp.zeros_like(q)
            q2 = jnp.concatenate([jnp.where(low_half, q, zero), jnp.where(low_half, zero, q)], axis=0)
            s_ref[slot, i] = lax.dot_general(q2, k, (((1,), (1,)), ((), ())), preferred_element_type=f32)

        def softmax(g, slot, i):
            r, n = locate(g * DIL_GROUP + i)
            s = s_ref[slot, i] + bias_ref[jnp.minimum(n, 1)]
            m = jnp.max(s, axis=1, keepdims=True)
            p_ref[slot, i] = jnp.exp2(s - m).astype(jnp.bfloat16)
            max_ref[idx, out_rows(r, n), :] = jnp.where(low_half, m[:BLOCK], m[BLOCK:])

        def values(g, slot, i):
            r, n = locate(g * DIL_GROUP + i)
            v = jnp.concatenate([v_ref[0, 0, r, key_rows(n), :], ones], axis=1)
            both = jnp.dot(p_ref[slot, i], v, preferred_element_type=f32)
            num, den = both[:, :LANES], both[:, LANES:]
            rows = out_rows(r, n)
            num = jnp.where(low_half, num[:BLOCK], num[BLOCK:])
            den = jnp.where(low_half, den[:BLOCK], den[BLOCK:])
            if dil != 1:
                num_ref[idx, rows, :] = num
                den_ref[idx, rows, :] = den
                return
            others = [i for i in range(len(DILATIONS)) if i != idx]
            ms = [max_ref[idx, rows, :]] + [max_ref[i, rows, :] for i in others]
            m_all = functools.reduce(jnp.maximum, ms)
            w = jnp.exp2(ms[0] - m_all)
            num, den = w * num, w * den
            for m, i in zip(ms[1:], others):
                w = jnp.exp2(m - m_all)
                num = num + w * num_ref[i, rows, :]
                den = den + w * den_ref[i, rows, :]
            o_ref[rows, :] = (num / den * g_ref[0, rows, :].astype(f32)).astype(o_ref.dtype)

        return scores, softmax, values

    assert DILATIONS[0] == 1
    stages = [branch(2, DILATIONS[2], q16_ref, k16_ref, v16_ref),
              branch(1, DILATIONS[1], q4_ref, k4_ref, v4_ref),
              branch(0, DILATIONS[0], q1_ref, k1_ref, v1_ref)]
    last = n_groups - 1

    def trip(*work):
        for stage, g, slot in work:
            for i in range(DIL_GROUP):
                stage(g, slot, i)

    for b, (scores, softmax, values) in enumerate(stages):
        if b == 0:
            trip((scores, 0, 0))
            trip((scores, 1, 1), (softmax, 0, 0))
        else:
            trip((stages[b - 1][2], last, 1), (scores, 1, 1), (softmax, 0, 0))

        def two_trips(t, carry, scores=scores, softmax=softmax, values=values):
            trip((values, 2 * t, 0), (scores, 2 * t + 2, 0), (softmax, 2 * t + 1, 1))
            trip((values, 2 * t + 1, 1), (scores, 2 * t + 3, 1), (softmax, 2 * t + 2, 0))
            return carry

        lax.fori_loop(0, (n_groups - 2) // 2, two_trips, 0)
        if b + 1 < len(stages):
            trip((values, last - 1, 0), (stages[b + 1][0], 0, 0), (softmax, last, 1))
        else:
            trip((values, last - 1, 0), (softmax, last, 1))
    trip((stages[-1][2], last, 1))


def _out_kernel(x_ref, ya_ref, yb_ref, wa_ref, wb_ref, g_ref, b_ref, o_ref):
    f32 = jnp.float32
    chunks = [pl.ds(r, OUT_CHUNK) for r in range(0, x_ref.shape[0], OUT_CHUNK)]

    def project(rows):
        return (jnp.dot(ya_ref[rows, :], wa_ref[...], preferred_element_type=f32)
                + jnp.dot(yb_ref[rows, :], wb_ref[...], preferred_element_type=f32))

    def normalise(rows, mixed):
        h = DEEPNORM_ALPHA * x_ref[rows, :] + mixed
        mu = jnp.mean(h, axis=-1, keepdims=True)
        c = h - mu
        var = jnp.mean(c * c, axis=-1, keepdims=True)
        o_ref[rows, :] = c * lax.rsqrt(var + LN_EPS) * g_ref[...] + b_ref[...]

    mixed = project(chunks[0])
    for i, rows in enumerate(chunks):
        ahead = project(chunks[i + 1]) if i + 1 < len(chunks) else None
        normalise(rows, mixed)
        mixed = ahead


def _rope_tables(seq, rot_dim, period, rot_offset, pass_rest):
    half = rot_dim // 2
    f32 = np.float32
    inv_freq = ROPE_THETA ** (-np.arange(0, rot_dim, 2, dtype=np.float64) / rot_dim)
    ang = np.arange(seq, dtype=np.float64)[:, None] * inv_freq[None, :]
    cos, sin = np.cos(ang).astype(f32), np.sin(ang).astype(f32)
    zeros = np.zeros((seq, half), f32)
    rest = period - rot_offset - rot_dim
    fill = np.ones if pass_rest else np.zeros
    group_cos = np.concatenate([np.ones((seq, rot_offset), f32), cos, cos, fill((seq, rest), f32)], axis=1)
    group_fwd = np.concatenate([np.zeros((seq, rot_offset), f32), zeros, sin, np.zeros((seq, rest), f32)], axis=1)
    group_bwd = np.concatenate([np.zeros((seq, rot_offset), f32), -sin, zeros, np.zeros((seq, rest), f32)], axis=1)
    reps = LANES // period
    return tuple(jnp.asarray(np.tile(t, (1, reps))) for t in (group_cos, group_fwd, group_bwd))


def _params(*semantics, flags=None):
    return pltpu.CompilerParams(dimension_semantics=semantics, vmem_limit_bytes=VMEM_LIMIT_BYTES, flags=flags)


def kernel(x, w_in, q_norm_g, kv_norm_g, w_uq, w_ukv, w_out, ln_g, ln_b):
    f32, bf16 = jnp.float32, jnp.bfloat16
    batch, seq, _ = x.shape
    rows = batch * seq
    x2 = x.reshape(rows, D_MODEL)

    rope_end = _LATENT_WIDTH + MLA_ROPE_DIM
    w_lat = w_in[:, :_LATENT_WIDTH].astype(bf16)
    w_kpe = jnp.pad(w_in[:, _LATENT_WIDTH:rope_end].astype(bf16),
                    ((0, 0), (MLA_NOPE_DIM, LANES - MLA_NOPE_DIM - MLA_ROPE_DIM)))
    w_rest = w_in[:, rope_end:].astype(bf16)
    assert w_rest.shape[1] == _REST_WIDTH
    dk = MLA_NOPE_DIM + MLA_ROPE_DIM
    wuq = jnp.pad(w_uq.reshape(Q_LORA_RANK, MLA_HEADS, dk), ((0, 0), (0, 0), (0, LANES - dk)))
    wuq = wuq.reshape(Q_LORA_RANK, MLA_HEADS * LANES).astype(bf16)
    wukv = w_ukv.reshape(KV_LORA_RANK, MLA_HEADS, MLA_NOPE_DIM + MLA_V_DIM)
    wuk = jnp.pad(wukv[:, :, :MLA_NOPE_DIM], ((0, 0), (0, 0), (0, LANES - MLA_NOPE_DIM)))
    wuk = wuk.reshape(KV_LORA_RANK, MLA_HEADS * LANES).astype(bf16)
    wuvt = wukv[:, :, MLA_NOPE_DIM:].reshape(KV_LORA_RANK, MLA_WIDTH).T.astype(bf16)
    wa = w_out[:MLA_WIDTH].astype(bf16)
    wb = w_out[MLA_WIDTH:].astype(bf16)

    mla_tabs = _rope_tables(seq, MLA_ROPE_DIM, LANES, MLA_NOPE_DIM, pass_rest=False)
    dil_tabs = _rope_tables(seq, DIL_ROT_DIM, DIL_HEAD_DIM, 0, pass_rest=True)

    tm = PROJ_ROWS
    seq_tiles = seq // tm
    full = lambda shape: pl.BlockSpec(shape, lambda i: (0,) * len(shape))
    tab = pl.BlockSpec((tm, LANES), lambda i: (i % seq_tiles, 0))
    slab = lambda n: pl.BlockSpec((n, tm, LANES), lambda i: (0, i, 0))
    slab_shape = lambda n: jax.ShapeDtypeStruct((n, rows, LANES), bf16)
    tk = MLA_KEY_TILE
    vt_spec = pl.BlockSpec((tm // tk, MLA_WIDTH, tk), lambda i: (i, 0, 0))
    vt_shape = jax.ShapeDtypeStruct((rows // tk, MLA_WIDTH, tk), bf16)
    res_specs, res_shapes = [], []
    for dil in DILATIONS:
        spec = pl.BlockSpec((HEAD_PAIRS, 1, dil, tm // dil, LANES),
                            lambda i: (0, i // seq_tiles, 0, i % seq_tiles, 0))
        shape = jax.ShapeDtypeStruct((HEAD_PAIRS, batch, dil, seq // dil, LANES), bf16)
        res_specs += [spec] * 3
        res_shapes += [shape] * 3
    qm, km, vt, ga, gb, *dil_in = pl.pallas_call(
        _proj_kernel,
        grid=(rows // tm,),
        in_specs=[pl.BlockSpec((tm, D_MODEL), lambda i: (i, 0)),
                  full(w_lat.shape), full(w_kpe.shape), full(w_rest.shape),
                  full(wuq.shape), full(wuk.shape), full(wuvt.shape),
                  full((1, Q_LORA_RANK)), full((1, KV_LORA_RANK)),
                  tab, tab, tab, tab, tab, tab],
        out_specs=[slab(MLA_HEADS), slab(MLA_HEADS), vt_spec, slab(HEAD_PAIRS), slab(HEAD_PAIRS)] + res_specs,
        out_shape=[slab_shape(MLA_HEADS), slab_shape(MLA_HEADS), vt_shape, slab_shape(HEAD_PAIRS),
                   slab_shape(HEAD_PAIRS)] + res_shapes,
        scratch_shapes=[pltpu.VMEM((3 * HEAD_PAIRS, tm, LANES), f32)],
        compiler_params=_params("parallel"),
        name="proj",
    )(x2, w_lat, w_kpe, w_rest, wuq, wuk, wuvt, q_norm_g.reshape(1, -1), kv_norm_g.reshape(1, -1),
      *mla_tabs, *dil_tabs)

    t = MLA_TILE
    ya = pl.pallas_call(
        _mla_kernel,
        grid=(batch, HEAD_PAIRS),
        in_specs=[pl.BlockSpec((2, seq, LANES), lambda b, p: (p, b, 0)),
                  pl.BlockSpec((2, seq, LANES), lambda b, p: (p, b, 0)),
                  pl.BlockSpec((seq // tk, LANES, tk), lambda b, p: (b, p, 0)),
                  pl.BlockSpec((1, seq, LANES), lambda b, p: (p, b, 0))],
        out_specs=pl.BlockSpec((seq, LANES), lambda b, p: (b, p)),
        out_shape=jax.ShapeDtypeStruct((rows, MLA_WIDTH), bf16),
        scratch_shapes=[pltpu.VMEM((2, 2, MLA_V_DIM + MLA_ONES_ROWS, t), f32), pltpu.VMEM((2, 2, tk, t), f32)],
        compiler_params=_params("parallel", "parallel"),
        name="mla",
    )(qm, km, vt, ga)

    dil_specs = [pl.BlockSpec((1, 1, dil, seq // dil, LANES), lambda b, p: (p, b, 0, 0, 0))
                 for dil in DILATIONS for _ in range(3)]
    yb = pl.pallas_call(
        _dilated_kernel,
        grid=(batch, HEAD_PAIRS),
        in_specs=dil_specs + [pl.BlockSpec((1, seq, LANES), lambda b, p: (p, b, 0))],
        out_specs=pl.BlockSpec((seq, LANES), lambda b, p: (b, p)),
        out_shape=jax.ShapeDtypeStruct((rows, DIL_WIDTH), bf16),
        scratch_shapes=[pltpu.VMEM((3, seq, LANES), f32)] * 3 + [
            pltpu.VMEM((2, DIL_GROUP, 2 * BLOCK, 2 * BLOCK), f32),
            pltpu.VMEM((2, DIL_GROUP, 2 * BLOCK, 2 * BLOCK), bf16),
            pltpu.VMEM((2, 2 * BLOCK, 2 * BLOCK), f32)],
        compiler_params=_params("parallel", "parallel"),
        name="dilated",
    )(*dil_in, gb)

    to = OUT_ROWS
    const = lambda shape: pl.BlockSpec(shape, lambda i: (0,) * len(shape))
    out = pl.pallas_call(
        _out_kernel,
        grid=(rows // to,),
        in_specs=[pl.BlockSpec((to, D_MODEL), lambda i: (i, 0)),
                  pl.BlockSpec((to, MLA_WIDTH), lambda i: (i, 0)),
                  pl.BlockSpec((to, DIL_WIDTH), lambda i: (i, 0)),
                  const((MLA_WIDTH, D_MODEL)), const((DIL_WIDTH, D_MODEL)),
                  const((1, D_MODEL)), const((1, D_MODEL))],
        out_specs=pl.BlockSpec((to, D_MODEL), lambda i: (i, 0)),
        out_shape=jax.ShapeDtypeStruct((rows, D_MODEL), f32),
        compiler_params=_params("parallel"),
        name="out",
    )(x2, ya, yb, wa, wb, ln_g.reshape(1, -1), ln_b.reshape(1, -1))
    return out.reshape(batch, seq, D_MODEL)
```

```python
import functools

import jax
import jax.numpy as jnp
import numpy as np
from jax import lax
from jax.experimental import pallas as pl
from jax.experimental.pallas import tpu as pltpu

D_MODEL = 1024
ROPE_THETA = 500000.0
BLOCK = 128
NEG = -1e30
RMS_EPS = 1e-6
LN_EPS = 1e-5

MLA_HEADS = 8
MLA_NOPE_DIM = 64
MLA_ROPE_DIM = 32
MLA_V_DIM = 64
Q_LORA_RANK = 384
KV_LORA_RANK = 256
MLA_WIDTH = MLA_HEADS * MLA_V_DIM

DIL_HEADS = 8
DIL_HEAD_DIM = 64
DIL_ROT_DIM = DIL_HEAD_DIM // 4
DIL_WIDTH = DIL_HEADS * DIL_HEAD_DIM
DILATIONS = (1, 4, 16)

DEPTH = 1
DEEPNORM_ALPHA = (2.0 * DEPTH) ** 0.25
LOG2_E = 1.4426950408889634

LANES = 128
HEAD_PAIRS = MLA_HEADS // 2
VMEM_LIMIT_BYTES = 56 * 1024 * 1024

_LATENT_WIDTH = Q_LORA_RANK + KV_LORA_RANK
_OFF_GA = 0
_OFF_QB = _OFF_GA + MLA_WIDTH
_OFF_KB = _OFF_QB + DIL_WIDTH
_OFF_VB = _OFF_KB + DIL_WIDTH
_OFF_GB = _OFF_VB + DIL_WIDTH
_REST_WIDTH = _OFF_GB + DIL_WIDTH

PROJ_ROWS = 512
MLA_KEY_TILE = 256
MLA_TILE = 1024
MLA_ONES_ROWS = 16
OUT_ROWS = 1024
OUT_CHUNK = 256
DIL_GROUP = 8
SCATTER_STRIDE = 4


def _rope_lanes(x, cos, sin_fwd, sin_bwd, half):
    fwd = pltpu.roll(x, half, 1)
    bwd = pltpu.roll(x, LANES - half, 1)
    return x * cos + fwd * sin_fwd + bwd * sin_bwd


def _proj_kernel(x_ref, wlat_ref, wkpe_ref, w_ref, wuq_ref, wuk_ref, wuvt_ref, qg_ref, kvg_ref,
                 mcos_ref, msf_ref, msb_ref, dcos_ref, dsf_ref, dsb_ref,
                 qm_ref, km_ref, vt_ref, ga_ref, gb_ref, *dil_and_scratch):
    f32 = jnp.float32
    bf16 = jnp.bfloat16
    xb = x_ref[...].astype(bf16)
    dil_refs = [dil_and_scratch[3 * i:3 * i + 3] for i in range(len(DILATIONS))]
    stage_ref = dil_and_scratch[-1]
    tm = x_ref.shape[0]

    def emit_residues(which, p, val):
        slab = which * HEAD_PAIRS + p
        stage_ref[slab] = val
        for refs, dil in zip(dil_refs, DILATIONS):
            for r in range(dil):
                rows = val if dil == 1 else stage_ref[slab, pl.ds(r, tm // dil, stride=dil), :]
                refs[which][p, 0, r] = rows.astype(bf16)

    def seg(lo, width):
        return jnp.dot(xb, w_ref[:, lo:lo + width], preferred_element_type=f32)

    def rms(t, g):
        return t * lax.rsqrt(jnp.mean(t * t, axis=-1, keepdims=True) + RMS_EPS) * g

    mcos, msf, msb = mcos_ref[...], msf_ref[...], msb_ref[...]
    dcos, dsf, dsb = dcos_ref[...], dsf_ref[...], dsb_ref[...]
    mla_scale = (MLA_NOPE_DIM + MLA_ROPE_DIM) ** -0.5 * LOG2_E
    dil_scale = DIL_HEAD_DIM ** -0.5 * LOG2_E

    unit = 2 * LANES
    latent = {}

    def slabs(t):
        return [t[:, j * LANES:(j + 1) * LANES] for j in range(t.shape[1] // LANES)]

    def dilated(which, lo, first_pair):
        def epilogue(t):
            for j, val in enumerate(slabs(t)):
                if which < 2:
                    val = _rope_lanes(val, dcos, dsf, dsb, DIL_ROT_DIM // 2)
                if which == 0:
                    val = val * dil_scale
                emit_residues(which, first_pair + j, val)
        return (lambda: seg(lo, unit)), epilogue

    def gate(out_ref, lo, first_pair):
        def epilogue(t):
            for j, val in enumerate(slabs(jax.nn.silu(t))):
                out_ref[first_pair + j] = val.astype(bf16)
        return (lambda: seg(lo, unit)), epilogue

    def mla_q(first_head):
        def epilogue(t):
            for j, val in enumerate(slabs(t)):
                val = _rope_lanes(val, mcos, msf, msb, MLA_ROPE_DIM // 2) * mla_scale
                qm_ref[first_head + j] = val.astype(bf16)
        lo = first_head * LANES
        return (lambda: jnp.dot(latent["cq"], wuq_ref[:, lo:lo + unit], preferred_element_type=f32)), epilogue

    def mla_k(first_head):
        def epilogue(t):
            for j, val in enumerate(slabs(t)):
                km_ref[first_head + j] = (val + latent["kpe"]).astype(bf16)
        lo = first_head * LANES
        return (lambda: jnp.dot(latent["ckv"], wuk_ref[:, lo:lo + unit], preferred_element_type=f32)), epilogue

    def store_vt(t):
        for c in range(vt_ref.shape[0]):
            vt_ref[c] = t[:, c * MLA_KEY_TILE:(c + 1) * MLA_KEY_TILE].astype(bf16)

    split = unit
    units = [
        (lambda: jnp.dot(xb, wlat_ref[:, :split], preferred_element_type=f32),
         lambda t: latent.update(cq_head=t)),
        (lambda: jnp.dot(xb, wlat_ref[:, split:Q_LORA_RANK], preferred_element_type=f32),
         lambda t: latent.update(cq=rms(jnp.concatenate([latent["cq_head"], t], axis=1), qg_ref[...]).astype(bf16))),
        (lambda: jnp.dot(xb, wlat_ref[:, Q_LORA_RANK:], preferred_element_type=f32),
         lambda t: latent.update(ckv=rms(t, kvg_ref[...]).astype(bf16))),
        dilated(0, _OFF_QB, 0), mla_q(0), dilated(0, _OFF_QB + unit, 2), mla_q(2),
        dilated(1, _OFF_KB, 0), mla_q(4), dilated(1, _OFF_KB + unit, 2), mla_q(6),
        (lambda: jnp.dot(xb, wkpe_ref[...], preferred_element_type=f32),
         lambda t: latent.update(kpe=_rope_lanes(t, mcos, msf, msb, MLA_ROPE_DIM // 2))),
        mla_k(0), dilated(2, _OFF_VB, 0), mla_k(2), dilated(2, _OFF_VB + unit, 2),
        mla_k(4), gate(ga_ref, _OFF_GA, 0), mla_k(6), gate(ga_ref, _OFF_GA + unit, 2),
        (lambda: lax.dot_general(wuvt_ref[...], latent["ckv"], (((1,), (1,)), ((), ())),
                                 preferred_element_type=f32), store_vt),
        gate(gb_ref, _OFF_GB, 0), gate(gb_ref, _OFF_GB + unit, 2),
    ]
    pending = None
    for matmul, epilogue in units:
        result = matmul()
        if pending is not None:
            pending[0](pending[1])
        pending = (epilogue, result)
    pending[0](pending[1])


def _mla_kernel(q_ref, k_ref, vt_ref, g_ref, o_ref, accs_ref, s_ref):
    f32 = jnp.float32
    bf16 = jnp.bfloat16
    tq, tk = MLA_TILE, MLA_KEY_TILE
    n_chunks = tq // tk
    assert n_chunks % 2 == 0
    nq = q_ref.shape[1] // tq
    units = [(h, c) for c in range(n_chunks) for h in range(2)]
    ones = jnp.ones((MLA_ONES_ROWS, tk), bf16)
    key = lax.broadcasted_iota(jnp.int32, (tk, tk), 0)
    qry = lax.broadcasted_iota(jnp.int32, (tk, tk), 1)
    lower = key <= qry

    def lanes(c):
        return slice(c * tk, (c + 1) * tk)

    def scores(qi, j, slot, h, c):
        q = q_ref[h, pl.ds(qi * tq + c * tk, tk), :]
        k = k_ref[h, pl.ds(j * tk if isinstance(j, int) else pl.multiple_of(j * tk, tk), tk), :]
        s = lax.dot_general(k, q, (((1,), (1,)), ((), ())), preferred_element_type=f32)
        s_ref[slot, h, :, lanes(c)] = s
        return jnp.max(s, axis=0, keepdims=True)

    def absorb(acc_ref, j, slot, h, c, m_tile, m_old, masked=False):
        s = s_ref[slot, h, :, lanes(c)]
        if masked:
            s = jnp.where(lower, s, NEG)
            m_tile = jnp.max(s, axis=0, keepdims=True)
        m_new = jnp.maximum(m_old, m_tile)
        alpha = jnp.exp2(m_old - m_new)
        p = jnp.exp2(s - m_new).astype(bf16)
        vt = jnp.concatenate([vt_ref[j, h * MLA_V_DIM:(h + 1) * MLA_V_DIM, :], ones], axis=0)
        acc_ref[h, :, lanes(c)] = alpha * acc_ref[h, :, lanes(c)] + jnp.dot(vt, p, preferred_element_type=f32)
        return m_new

    def per_query_tile(qi, acc_ref):
        rows = pl.ds(qi * tq, tq)
        acc_ref[...] = jnp.zeros_like(acc_ref)

        def step(j, slot, tile_max, m_run):
            next_max, m_new = {}, {}
            for u in units:
                next_max[u] = scores(qi, j + 1, 1 - slot, *u)
                m_new[u] = absorb(acc_ref, j, slot, *u, tile_max[u], m_run[u])
            return next_max, m_new

        def body(jj, carry):
            tile_max, m_run = carry
            for i in range(n_chunks):
                tile_max, m_run = step(n_chunks * jj + i, i % 2, tile_max, m_run)
            return tile_max, m_run

        start = {u: jnp.full((1, tk), NEG, f32) for u in units}
        first = {u: scores(qi, 0, 0, *u) for u in units}
        tile_max, m_run = lax.fori_loop(0, qi, body, (first, start))

        base = qi * n_chunks
        for d in range(n_chunks):
            next_max = {}
            for h, c in units:
                if c > d:
                    next_max[h, c] = scores(qi, base + d + 1, (d + 1) % 2, h, c)
                if c >= d:
                    m_run[h, c] = absorb(acc_ref, base + d, d % 2, h, c, tile_max[h, c], m_run[h, c],
                                         masked=(c == d))
            tile_max = next_max

        yt = jnp.concatenate([acc_ref[h, :MLA_V_DIM, :] / acc_ref[h, MLA_V_DIM:MLA_V_DIM + 1, :] for h in range(2)],
                             axis=0)
        o_ref[rows, :] = (yt.T * g_ref[0, rows, :].astype(f32)).astype(o_ref.dtype)

    for qi in range(nq):
        per_query_tile(qi, accs_ref.at[qi % 2])


def _dilated_kernel(q1_ref, k1_ref, v1_ref, q4_ref, k4_ref, v4_ref, q16_ref, k16_ref, v16_ref,
                    g_ref, o_ref, num_ref, max_ref, den_ref, hop_ref, s_ref, p_ref, bias_ref):
    f32 = jnp.float32
    seq = num_ref.shape[1]

    lane = lax.broadcasted_iota(jnp.int32, (BLOCK, LANES), 1)
    low_half = lane < DIL_HEAD_DIM
    ones = jnp.ones((2 * BLOCK, LANES), jnp.bfloat16)
    qi = lax.broadcasted_iota(jnp.int32, (2 * BLOCK, 2 * BLOCK), 0) % BLOCK
    kj = lax.broadcasted_iota(jnp.int32, (2 * BLOCK, 2 * BLOCK), 1)
    dist_first = qi - kj
    dist_later = dist_first + BLOCK
    for kind, dist in enumerate((dist_first, dist_later)):
        bias_ref[kind] = jnp.where((dist >= 0) & (dist <= BLOCK), 0.0, NEG).astype(f32)

    n_groups = seq // BLOCK // DIL_GROUP
    assert n_groups % 2 == 0 and n_groups >= 4

    def branch(idx, dil, q_ref, k_ref, v_ref):
        per_residue = q_ref.shape[3] // BLOCK

        def locate(b):
            return b // per_residue, b % per_residue

        def key_rows(n):
            return pl.ds(pl.multiple_of(jnp.maximum(n - 1, 0) * BLOCK, BLOCK), 2 * BLOCK)

        def out_rows(r, n):
            if dil == 1:
                return pl.ds(pl.multiple_of(n * BLOCK, BLOCK), BLOCK)
            return pl.ds(r + n * (BLOCK * dil), BLOCK, stride=dil)

        two_hops = dil > SCATTER_STRIDE
        inner = dil // SCATTER_STRIDE

        def scatter(kind, stat_ref, r, n, val):
            if two_hops:
                rows = pl.ds(r // SCATTER_STRIDE + n * (BLOCK * inner), BLOCK, stride=inner)
                hop_ref[kind, r % SCATTER_STRIDE, rows, :] = val
            else:
                stat_ref[idx, out_rows(r, n), :] = val

        def finish():
            if not two_hops:
                return
            for kind, stat_ref in enumerate((num_ref, max_ref, den_ref)):
                for a in range(SCATTER_STRIDE):
                    for c in range(hop_ref.shape[2] // BLOCK):
                        rows = pl.ds(a + c * (BLOCK * SCATTER_STRIDE), BLOCK, stride=SCATTER_STRIDE)
                        stat_ref[idx, rows, :] = hop_ref[kind, a, c * BLOCK:(c + 1) * BLOCK, :]

        def scores(g, slot, i):
            r, n = locate(g * DIL_GROUP + i)
            q = q_ref[0, 0, r, pl.ds(pl.multiple_of(n * BLOCK, BLOCK), BLOCK), :]
            k = k_ref[0, 0, r, key_rows(n), :]
            zero = jnp.zeros_like(q)
            q2 = jnp.concatenate([jnp.where(low_half, q, zero), jnp.where(low_half, zero, q)], axis=0)
            s_ref[slot, i] = lax.dot_general(q2, k, (((1,), (1,)), ((), ())), preferred_element_type=f32)

        def softmax(g, slot, i):
            r, n = locate(g * DIL_GROUP + i)
            s = s_ref[slot, i] + bias_ref[jnp.minimum(n, 1)]
            m = jnp.max(s, axis=1, keepdims=True)
            p_ref[slot, i] = jnp.exp2(s - m).astype(jnp.bfloat16)
            scatter(1, max_ref, r, n, jnp.where(low_half, m[:BLOCK], m[BLOCK:]))

        def values(g, slot, i):
            r, n = locate(g * DIL_GROUP + i)
            v = jnp.concatenate([v_ref[0, 0, r, key_rows(n), :], ones], axis=1)
            both = jnp.dot(p_ref[slot, i], v, preferred_element_type=f32)
            num, den = both[:, :LANES], both[:, LANES:]
            num = jnp.where(low_half, num[:BLOCK], num[BLOCK:])
            den = jnp.where(low_half, den[:BLOCK], den[BLOCK:])
            if dil != 1:
                scatter(0, num_ref, r, n, num)
                scatter(2, den_ref, r, n, den)
                return
            rows = out_rows(r, n)
            others = [i for i in range(len(DILATIONS)) if i != idx]
            ms = [max_ref[idx, rows, :]] + [max_ref[i, rows, :] for i in others]
            m_all = functools.reduce(jnp.maximum, ms)
            w = jnp.exp2(ms[0] - m_all)
            num, den = w * num, w * den
            for m, i in zip(ms[1:], others):
                w = jnp.exp2(m - m_all)
                num = num + w * num_ref[i, rows, :]
                den = den + w * den_ref[i, rows, :]
            o_ref[rows, :] = (num / den * g_ref[0, rows, :].astype(f32)).astype(o_ref.dtype)

        return scores, softmax, values, finish

    assert DILATIONS[0] == 1
    stages = [branch(2, DILATIONS[2], q16_ref, k16_ref, v16_ref),
              branch(1, DILATIONS[1], q4_ref, k4_ref, v4_ref),
              branch(0, DILATIONS[0], q1_ref, k1_ref, v1_ref)]
    last = n_groups - 1

    def trip(*work):
        for stage, g, slot in work:
            for i in range(DIL_GROUP):
                stage(g, slot, i)

    for b, (scores, softmax, values, _) in enumerate(stages):
        if b == 0:
            trip((scores, 0, 0))
            trip((scores, 1, 1), (softmax, 0, 0))
        else:
            trip((stages[b - 1][2], last, 1), (scores, 1, 1), (softmax, 0, 0))
            stages[b - 1][3]()

        def two_trips(t, carry, scores=scores, softmax=softmax, values=values):
            trip((values, 2 * t, 0), (scores, 2 * t + 2, 0), (softmax, 2 * t + 1, 1))
            trip((values, 2 * t + 1, 1), (scores, 2 * t + 3, 1), (softmax, 2 * t + 2, 0))
            return carry

        lax.fori_loop(0, (n_groups - 2) // 2, two_trips, 0)
        if b + 1 < len(stages):
            trip((values, last - 1, 0), (stages[b + 1][0], 0, 0), (softmax, last, 1))
        else:
            trip((values, last - 1, 0), (softmax, last, 1))
    trip((stages[-1][2], last, 1))


def _out_kernel(x_ref, ya_ref, yb_ref, wa_ref, wb_ref, g_ref, b_ref, o_ref):
    f32 = jnp.float32
    chunks = [pl.ds(r, OUT_CHUNK) for r in range(0, x_ref.shape[0], OUT_CHUNK)]

    def project(rows):
        return (jnp.dot(ya_ref[rows, :], wa_ref[...], preferred_element_type=f32)
                + jnp.dot(yb_ref[rows, :], wb_ref[...], preferred_element_type=f32))

    def normalise(rows, mixed):
        h = DEEPNORM_ALPHA * x_ref[rows, :] + mixed
        mu = jnp.mean(h, axis=-1, keepdims=True)
        c = h - mu
        var = jnp.mean(c * c, axis=-1, keepdims=True)
        o_ref[rows, :] = c * lax.rsqrt(var + LN_EPS) * g_ref[...] + b_ref[...]

    mixed = project(chunks[0])
    for i, rows in enumerate(chunks):
        ahead = project(chunks[i + 1]) if i + 1 < len(chunks) else None
        normalise(rows, mixed)
        mixed = ahead


def _rope_tables(seq, rot_dim, period, rot_offset, pass_rest):
    half = rot_dim // 2
    f32 = np.float32
    inv_freq = ROPE_THETA ** (-np.arange(0, rot_dim, 2, dtype=np.float64) / rot_dim)
    ang = np.arange(seq, dtype=np.float64)[:, None] * inv_freq[None, :]
    cos, sin = np.cos(ang).astype(f32), np.sin(ang).astype(f32)
    zeros = np.zeros((seq, half), f32)
    rest = period - rot_offset - rot_dim
    fill = np.ones if pass_rest else np.zeros
    group_cos = np.concatenate([np.ones((seq, rot_offset), f32), cos, cos, fill((seq, rest), f32)], axis=1)
    group_fwd = np.concatenate([np.zeros((seq, rot_offset), f32), zeros, sin, np.zeros((seq, rest), f32)], axis=1)
    group_bwd = np.concatenate([np.zeros((seq, rot_offset), f32), -sin, zeros, np.zeros((seq, rest), f32)], axis=1)
    reps = LANES // period
    return tuple(jnp.asarray(np.tile(t, (1, reps))) for t in (group_cos, group_fwd, group_bwd))


def _params(*semantics, flags=None):
    return pltpu.CompilerParams(dimension_semantics=semantics, vmem_limit_bytes=VMEM_LIMIT_BYTES, flags=flags)


def kernel(x, w_in, q_norm_g, kv_norm_g, w_uq, w_ukv, w_out, ln_g, ln_b):
    f32, bf16 = jnp.float32, jnp.bfloat16
    batch, seq, _ = x.shape
    rows = batch * seq
    x2 = x.reshape(rows, D_MODEL)

    rope_end = _LATENT_WIDTH + MLA_ROPE_DIM
    w_lat = w_in[:, :_LATENT_WIDTH].astype(bf16)
    w_kpe = jnp.pad(w_in[:, _LATENT_WIDTH:rope_end].astype(bf16),
                    ((0, 0), (MLA_NOPE_DIM, LANES - MLA_NOPE_DIM - MLA_ROPE_DIM)))
    w_rest = w_in[:, rope_end:].astype(bf16)
    assert w_rest.shape[1] == _REST_WIDTH
    dk = MLA_NOPE_DIM + MLA_ROPE_DIM
    wuq = jnp.pad(w_uq.reshape(Q_LORA_RANK, MLA_HEADS, dk), ((0, 0), (0, 0), (0, LANES - dk)))
    wuq = wuq.reshape(Q_LORA_RANK, MLA_HEADS * LANES).astype(bf16)
    wukv = w_ukv.reshape(KV_LORA_RANK, MLA_HEADS, MLA_NOPE_DIM + MLA_V_DIM)
    wuk = jnp.pad(wukv[:, :, :MLA_NOPE_DIM], ((0, 0), (0, 0), (0, LANES - MLA_NOPE_DIM)))
    wuk = wuk.reshape(KV_LORA_RANK, MLA_HEADS * LANES).astype(bf16)
    wuvt = wukv[:, :, MLA_NOPE_DIM:].reshape(KV_LORA_RANK, MLA_WIDTH).T.astype(bf16)
    wa = w_out[:MLA_WIDTH].astype(bf16)
    wb = w_out[MLA_WIDTH:].astype(bf16)

    mla_tabs = _rope_tables(seq, MLA_ROPE_DIM, LANES, MLA_NOPE_DIM, pass_rest=False)
    dil_tabs = _rope_tables(seq, DIL_ROT_DIM, DIL_HEAD_DIM, 0, pass_rest=True)

    tm = PROJ_ROWS
    seq_tiles = seq // tm
    full = lambda shape: pl.BlockSpec(shape, lambda i: (0,) * len(shape))
    tab = pl.BlockSpec((tm, LANES), lambda i: (i % seq_tiles, 0))
    slab = lambda n: pl.BlockSpec((n, tm, LANES), lambda i: (0, i, 0))
    slab_shape = lambda n: jax.ShapeDtypeStruct((n, rows, LANES), bf16)
    tk = MLA_KEY_TILE
    vt_spec = pl.BlockSpec((tm // tk, MLA_WIDTH, tk), lambda i: (i, 0, 0))
    vt_shape = jax.ShapeDtypeStruct((rows // tk, MLA_WIDTH, tk), bf16)
    res_specs, res_shapes = [], []
    for dil in DILATIONS:
        spec = pl.BlockSpec((HEAD_PAIRS, 1, dil, tm // dil, LANES),
                            lambda i: (0, i // seq_tiles, 0, i % seq_tiles, 0))
        shape = jax.ShapeDtypeStruct((HEAD_PAIRS, batch, dil, seq // dil, LANES), bf16)
        res_specs += [spec] * 3
        res_shapes += [shape] * 3
    qm, km, vt, ga, gb, *dil_in = pl.pallas_call(
        _proj_kernel,
        grid=(rows // tm,),
        in_specs=[pl.BlockSpec((tm, D_MODEL), lambda i: (i, 0)),
                  full(w_lat.shape), full(w_kpe.shape), full(w_rest.shape),
                  full(wuq.shape), full(wuk.shape), full(wuvt.shape),
                  full((1, Q_LORA_RANK)), full((1, KV_LORA_RANK)),
                  tab, tab, tab, tab, tab, tab],
        out_specs=[slab(MLA_HEADS), slab(MLA_HEADS), vt_spec, slab(HEAD_PAIRS), slab(HEAD_PAIRS)] + res_specs,
        out_shape=[slab_shape(MLA_HEADS), slab_shape(MLA_HEADS), vt_shape, slab_shape(HEAD_PAIRS),
                   slab_shape(HEAD_PAIRS)] + res_shapes,
        scratch_shapes=[pltpu.VMEM((3 * HEAD_PAIRS, tm, LANES), f32)],
        compiler_params=_params("parallel"),
        name="proj",
    )(x2, w_lat, w_kpe, w_rest, wuq, wuk, wuvt, q_norm_g.reshape(1, -1), kv_norm_g.reshape(1, -1),
      *mla_tabs, *dil_tabs)

    t = MLA_TILE
    ya = pl.pallas_call(
        _mla_kernel,
        grid=(batch, HEAD_PAIRS),
        in_specs=[pl.BlockSpec((2, seq, LANES), lambda b, p: (p, b, 0)),
                  pl.BlockSpec((2, seq, LANES), lambda b, p: (p, b, 0)),
                  pl.BlockSpec((seq // tk, LANES, tk), lambda b, p: (b, p, 0)),
                  pl.BlockSpec((1, seq, LANES), lambda b, p: (p, b, 0))],
        out_specs=pl.BlockSpec((seq, LANES), lambda b, p: (b, p)),
        out_shape=jax.ShapeDtypeStruct((rows, MLA_WIDTH), bf16),
        scratch_shapes=[pltpu.VMEM((2, 2, MLA_V_DIM + MLA_ONES_ROWS, t), f32), pltpu.VMEM((2, 2, tk, t), f32)],
        compiler_params=_params("parallel", "parallel"),
        name="mla",
    )(qm, km, vt, ga)

    dil_specs = [pl.BlockSpec((1, 1, dil, seq // dil, LANES), lambda b, p: (p, b, 0, 0, 0))
                 for dil in DILATIONS for _ in range(3)]
    yb = pl.pallas_call(
        _dilated_kernel,
        grid=(batch, HEAD_PAIRS),
        in_specs=dil_specs + [pl.BlockSpec((1, seq, LANES), lambda b, p: (p, b, 0))],
        out_specs=pl.BlockSpec((seq, LANES), lambda b, p: (b, p)),
        out_shape=jax.ShapeDtypeStruct((rows, DIL_WIDTH), bf16),
        scratch_shapes=[pltpu.VMEM((3, seq, LANES), f32)] * 3 + [
            pltpu.VMEM((3, SCATTER_STRIDE, seq // SCATTER_STRIDE, LANES), f32),
            pltpu.VMEM((2, DIL_GROUP, 2 * BLOCK, 2 * BLOCK), f32),
            pltpu.VMEM((2, DIL_GROUP, 2 * BLOCK, 2 * BLOCK), bf16),
            pltpu.VMEM((2, 2 * BLOCK, 2 * BLOCK), f32)],
        compiler_params=_params("parallel", "parallel"),
        name="dilated",
    )(*dil_in, gb)

    to = OUT_ROWS
    const = lambda shape: pl.BlockSpec(shape, lambda i: (0,) * len(shape))
    out = pl.pallas_call(
        _out_kernel,
        grid=(rows // to,),
        in_specs=[pl.BlockSpec((to, D_MODEL), lambda i: (i, 0)),
                  pl.BlockSpec((to, MLA_WIDTH), lambda i: (i, 0)),
                  pl.BlockSpec((to, DIL_WIDTH), lambda i: (i, 0)),
                  const((MLA_WIDTH, D_MODEL)), const((DIL_WIDTH, D_MODEL)),
                  const((1, D_MODEL)), const((1, D_MODEL))],
        out_specs=pl.BlockSpec((to, D_MODEL), lambda i: (i, 0)),
        out_shape=jax.ShapeDtypeStruct((rows, D_MODEL), f32),
        compiler_params=_params("parallel"),
        name="out",
    )(x2, ya, yb, wa, wb, ln_g.reshape(1, -1), ln_b.reshape(1, -1))
    return out.reshape(batch, seq, D_MODEL)
```

```python
import functools

import jax
import jax.numpy as jnp
import numpy as np
from jax import lax
from jax.experimental import pallas as pl
from jax.experimental.pallas import tpu as pltpu

D_MODEL = 1024
ROPE_THETA = 500000.0
BLOCK = 128
NEG = -1e30
RMS_EPS = 1e-6
LN_EPS = 1e-5

MLA_HEADS = 8
MLA_NOPE_DIM = 64
MLA_ROPE_DIM = 32
MLA_V_DIM = 64
Q_LORA_RANK = 384
KV_LORA_RANK = 256
MLA_WIDTH = MLA_HEADS * MLA_V_DIM

DIL_HEADS = 8
DIL_HEAD_DIM = 64
DIL_ROT_DIM = DIL_HEAD_DIM // 4
DIL_WIDTH = DIL_HEADS * DIL_HEAD_DIM
DILATIONS = (1, 4, 16)

DEPTH = 1
DEEPNORM_ALPHA = (2.0 * DEPTH) ** 0.25
LOG2_E = 1.4426950408889634

LANES = 128
HEAD_PAIRS = MLA_HEADS // 2
VMEM_LIMIT_BYTES = 56 * 1024 * 1024

_LATENT_WIDTH = Q_LORA_RANK + KV_LORA_RANK
_OFF_GA = 0
_OFF_QB = _OFF_GA + MLA_WIDTH
_OFF_KB = _OFF_QB + DIL_WIDTH
_OFF_VB = _OFF_KB + DIL_WIDTH
_OFF_GB = _OFF_VB + DIL_WIDTH
_REST_WIDTH = _OFF_GB + DIL_WIDTH

PROJ_ROWS = 512
MLA_KEY_TILE = 256
MLA_TILE = 1024
MLA_ONES_ROWS = 16
OUT_ROWS = 1024
OUT_CHUNK = 256
DIL_GROUP = 4
SCATTER_STRIDE = 4


def _rope_lanes(x, cos, sin_fwd, sin_bwd, half):
    fwd = pltpu.roll(x, half, 1)
    bwd = pltpu.roll(x, LANES - half, 1)
    return x * cos + fwd * sin_fwd + bwd * sin_bwd


def _proj_kernel(x_ref, wlat_ref, wkpe_ref, w_ref, wuq_ref, wuk_ref, wuvt_ref, qg_ref, kvg_ref,
                 mcos_ref, msf_ref, msb_ref, dcos_ref, dsf_ref, dsb_ref,
                 qm_ref, km_ref, vt_ref, ga_ref, gb_ref, *dil_and_scratch):
    f32 = jnp.float32
    bf16 = jnp.bfloat16
    xb = x_ref[...].astype(bf16)
    dil_refs = [dil_and_scratch[3 * i:3 * i + 3] for i in range(len(DILATIONS))]
    stage_ref = dil_and_scratch[-1]
    tm = x_ref.shape[0]

    def emit_residues(which, p, val):
        slab = which * HEAD_PAIRS + p
        stage_ref[slab] = val
        for refs, dil in zip(dil_refs, DILATIONS):
            for r in range(dil):
                rows = val if dil == 1 else stage_ref[slab, pl.ds(r, tm // dil, stride=dil), :]
                refs[which][p, 0, r] = rows.astype(bf16)

    def seg(lo, width):
        return jnp.dot(xb, w_ref[:, lo:lo + width], preferred_element_type=f32)

    def rms(t, g):
        return t * lax.rsqrt(jnp.mean(t * t, axis=-1, keepdims=True) + RMS_EPS) * g

    mcos, msf, msb = mcos_ref[...], msf_ref[...], msb_ref[...]
    dcos, dsf, dsb = dcos_ref[...], dsf_ref[...], dsb_ref[...]
    mla_scale = (MLA_NOPE_DIM + MLA_ROPE_DIM) ** -0.5 * LOG2_E
    dil_scale = DIL_HEAD_DIM ** -0.5 * LOG2_E

    unit = 2 * LANES
    latent = {}

    def slabs(t):
        return [t[:, j * LANES:(j + 1) * LANES] for j in range(t.shape[1] // LANES)]

    def dilated(which, lo, first_pair):
        def epilogue(t):
            for j, val in enumerate(slabs(t)):
                if which < 2:
                    val = _rope_lanes(val, dcos, dsf, dsb, DIL_ROT_DIM // 2)
                if which == 0:
                    val = val * dil_scale
                emit_residues(which, first_pair + j, val)
        return (lambda: seg(lo, unit)), epilogue

    def gate(out_ref, lo, first_pair):
        def epilogue(t):
            for j, val in enumerate(slabs(jax.nn.silu(t))):
                out_ref[first_pair + j] = val.astype(bf16)
        return (lambda: seg(lo, unit)), epilogue

    def mla_q(first_head):
        def epilogue(t):
            for j, val in enumerate(slabs(t)):
                val = _rope_lanes(val, mcos, msf, msb, MLA_ROPE_DIM // 2) * mla_scale
                qm_ref[first_head + j] = val.astype(bf16)
        lo = first_head * LANES
        return (lambda: jnp.dot(latent["cq"], wuq_ref[:, lo:lo + unit], preferred_element_type=f32)), epilogue

    def mla_k(first_head):
        def epilogue(t):
            for j, val in enumerate(slabs(t)):
                km_ref[first_head + j] = (val + latent["kpe"]).astype(bf16)
        lo = first_head * LANES
        return (lambda: jnp.dot(latent["ckv"], wuk_ref[:, lo:lo + unit], preferred_element_type=f32)), epilogue

    def store_vt(t):
        for c in range(vt_ref.shape[0]):
            vt_ref[c] = t[:, c * MLA_KEY_TILE:(c + 1) * MLA_KEY_TILE].astype(bf16)

    split = unit
    units = [
        (lambda: jnp.dot(xb, wlat_ref[:, :split], preferred_element_type=f32),
         lambda t: latent.update(cq_head=t)),
        (lambda: jnp.dot(xb, wlat_ref[:, split:Q_LORA_RANK], preferred_element_type=f32),
         lambda t: latent.update(cq=rms(jnp.concatenate([latent["cq_head"], t], axis=1), qg_ref[...]).astype(bf16))),
        (lambda: jnp.dot(xb, wlat_ref[:, Q_LORA_RANK:], preferred_element_type=f32),
         lambda t: latent.update(ckv=rms(t, kvg_ref[...]).astype(bf16))),
        dilated(0, _OFF_QB, 0), mla_q(0), dilated(0, _OFF_QB + unit, 2), mla_q(2),
        dilated(1, _OFF_KB, 0), mla_q(4), dilated(1, _OFF_KB + unit, 2), mla_q(6),
        (lambda: jnp.dot(xb, wkpe_ref[...], preferred_element_type=f32),
         lambda t: latent.update(kpe=_rope_lanes(t, mcos, msf, msb, MLA_ROPE_DIM // 2))),
        mla_k(0), dilated(2, _OFF_VB, 0), mla_k(2), dilated(2, _OFF_VB + unit, 2),
        mla_k(4), gate(ga_ref, _OFF_GA, 0), mla_k(6), gate(ga_ref, _OFF_GA + unit, 2),
        (lambda: lax.dot_general(wuvt_ref[...], latent["ckv"], (((1,), (1,)), ((), ())),
                                 preferred_element_type=f32), store_vt),
        gate(gb_ref, _OFF_GB, 0), gate(gb_ref, _OFF_GB + unit, 2),
    ]
    pending = None
    for matmul, epilogue in units:
        result = matmul()
        if pending is not None:
            pending[0](pending[1])
        pending = (epilogue, result)
    pending[0](pending[1])


def _mla_kernel(q_ref, k_ref, vt_ref, g_ref, o_ref, accs_ref, s_ref):
    f32 = jnp.float32
    bf16 = jnp.bfloat16
    tq, tk = MLA_TILE, MLA_KEY_TILE
    n_chunks = tq // tk
    assert n_chunks % 2 == 0
    nq = q_ref.shape[1] // tq
    units = [(h, c) for c in range(n_chunks) for h in range(2)]
    ones = jnp.ones((MLA_ONES_ROWS, tk), bf16)
    key = lax.broadcasted_iota(jnp.int32, (tk, tk), 0)
    qry = lax.broadcasted_iota(jnp.int32, (tk, tk), 1)
    lower = key <= qry

    def lanes(c):
        return slice(c * tk, (c + 1) * tk)

    def scores(qi, j, slot, h, c):
        q = q_ref[h, pl.ds(qi * tq + c * tk, tk), :]
        k = k_ref[h, pl.ds(j * tk if isinstance(j, int) else pl.multiple_of(j * tk, tk), tk), :]
        s = lax.dot_general(k, q, (((1,), (1,)), ((), ())), preferred_element_type=f32)
        s_ref[slot, h, :, lanes(c)] = s
        return jnp.max(s, axis=0, keepdims=True)

    def absorb(acc_ref, j, slot, h, c, m_tile, m_old, masked=False):
        s = s_ref[slot, h, :, lanes(c)]
        if masked:
            s = jnp.where(lower, s, NEG)
            m_tile = jnp.max(s, axis=0, keepdims=True)
        m_new = jnp.maximum(m_old, m_tile)
        alpha = jnp.exp2(m_old - m_new)
        p = jnp.exp2(s - m_new).astype(bf16)
        vt = jnp.concatenate([vt_ref[j, h * MLA_V_DIM:(h + 1) * MLA_V_DIM, :], ones], axis=0)
        acc_ref[h, :, lanes(c)] = alpha * acc_ref[h, :, lanes(c)] + jnp.dot(vt, p, preferred_element_type=f32)
        return m_new

    def per_query_tile(qi, acc_ref):
        rows = pl.ds(qi * tq, tq)
        acc_ref[...] = jnp.zeros_like(acc_ref)

        def step(j, slot, tile_max, m_run):
            next_max, m_new = {}, {}
            for u in units:
                next_max[u] = scores(qi, j + 1, 1 - slot, *u)
                m_new[u] = absorb(acc_ref, j, slot, *u, tile_max[u], m_run[u])
            return next_max, m_new

        def body(jj, carry):
            tile_max, m_run = carry
            for i in range(n_chunks):
                tile_max, m_run = step(n_chunks * jj + i, i % 2, tile_max, m_run)
            return tile_max, m_run

        start = {u: jnp.full((1, tk), NEG, f32) for u in units}
        first = {u: scores(qi, 0, 0, *u) for u in units}
        tile_max, m_run = lax.fori_loop(0, qi, body, (first, start))

        base = qi * n_chunks
        for d in range(n_chunks):
            next_max = {}
            for h, c in units:
                if c > d:
                    next_max[h, c] = scores(qi, base + d + 1, (d + 1) % 2, h, c)
                if c >= d:
                    m_run[h, c] = absorb(acc_ref, base + d, d % 2, h, c, tile_max[h, c], m_run[h, c],
                                         masked=(c == d))
            tile_max = next_max

        yt = jnp.concatenate([acc_ref[h, :MLA_V_DIM, :] / acc_ref[h, MLA_V_DIM:MLA_V_DIM + 1, :] for h in range(2)],
                             axis=0)
        o_ref[rows, :] = (yt.T * g_ref[0, rows, :].astype(f32)).astype(o_ref.dtype)

    for qi in range(nq):
        per_query_tile(qi, accs_ref.at[qi % 2])


def _dilated_kernel(q1_ref, k1_ref, v1_ref, q4_ref, k4_ref, v4_ref, q16_ref, k16_ref, v16_ref,
                    g_ref, o_ref, num_ref, max_ref, den_ref, hop_ref, s_ref, p_ref, bias_ref):
    f32 = jnp.float32
    seq = num_ref.shape[1]

    lane = lax.broadcasted_iota(jnp.int32, (BLOCK, LANES), 1)
    low_half = lane < DIL_HEAD_DIM
    ones = jnp.ones((2 * BLOCK, LANES), jnp.bfloat16)
    qi = lax.broadcasted_iota(jnp.int32, (2 * BLOCK, 2 * BLOCK), 0) % BLOCK
    kj = lax.broadcasted_iota(jnp.int32, (2 * BLOCK, 2 * BLOCK), 1)
    dist_first = qi - kj
    dist_later = dist_first + BLOCK
    for kind, dist in enumerate((dist_first, dist_later)):
        bias_ref[kind] = jnp.where((dist >= 0) & (dist <= BLOCK), 0.0, NEG).astype(f32)

    n_groups = seq // BLOCK // DIL_GROUP
    assert n_groups % 2 == 0 and n_groups >= 4

    def branch(idx, dil, q_ref, k_ref, v_ref):
        per_residue = q_ref.shape[3] // BLOCK

        def locate(b):
            return b // per_residue, b % per_residue

        def key_rows(n):
            return pl.ds(pl.multiple_of(jnp.maximum(n - 1, 0) * BLOCK, BLOCK), 2 * BLOCK)

        def out_rows(r, n):
            if dil == 1:
                return pl.ds(pl.multiple_of(n * BLOCK, BLOCK), BLOCK)
            return pl.ds(r + n * (BLOCK * dil), BLOCK, stride=dil)

        two_hops = dil > SCATTER_STRIDE
        inner = dil // SCATTER_STRIDE

        def scatter(kind, stat_ref, r, n, val):
            if two_hops:
                rows = pl.ds(r // SCATTER_STRIDE + n * (BLOCK * inner), BLOCK, stride=inner)
                hop_ref[kind, r % SCATTER_STRIDE, rows, :] = val
            else:
                stat_ref[idx, out_rows(r, n), :] = val

        def finish():
            if not two_hops:
                return
            for kind, stat_ref in enumerate((num_ref, max_ref, den_ref)):
                for a in range(SCATTER_STRIDE):
                    for c in range(hop_ref.shape[2] // BLOCK):
                        rows = pl.ds(a + c * (BLOCK * SCATTER_STRIDE), BLOCK, stride=SCATTER_STRIDE)
                        stat_ref[idx, rows, :] = hop_ref[kind, a, c * BLOCK:(c + 1) * BLOCK, :]

        def scores(g, slot, i):
            r, n = locate(g * DIL_GROUP + i)
            q = q_ref[0, 0, r, pl.ds(pl.multiple_of(n * BLOCK, BLOCK), BLOCK), :]
            k = k_ref[0, 0, r, key_rows(n), :]
            zero = jnp.zeros_like(q)
            q2 = jnp.concatenate([jnp.where(low_half, q, zero), jnp.where(low_half, zero, q)], axis=0)
            s_ref[slot, i] = lax.dot_general(q2, k, (((1,), (1,)), ((), ())), preferred_element_type=f32)

        def softmax(g, slot, i):
            r, n = locate(g * DIL_GROUP + i)
            s = s_ref[slot, i] + bias_ref[jnp.minimum(n, 1)]
            m = jnp.max(s, axis=1, keepdims=True)
            p_ref[slot, i] = jnp.exp2(s - m).astype(jnp.bfloat16)
            scatter(1, max_ref, r, n, jnp.where(low_half, m[:BLOCK], m[BLOCK:]))

        def values(g, slot, i):
            r, n = locate(g * DIL_GROUP + i)
            v = jnp.concatenate([v_ref[0, 0, r, key_rows(n), :], ones], axis=1)
            both = jnp.dot(p_ref[slot, i], v, preferred_element_type=f32)
            num, den = both[:, :LANES], both[:, LANES:]
            num = jnp.where(low_half, num[:BLOCK], num[BLOCK:])
            den = jnp.where(low_half, den[:BLOCK], den[BLOCK:])
            if dil != 1:
                scatter(0, num_ref, r, n, num)
                scatter(2, den_ref, r, n, den)
                return
            rows = out_rows(r, n)
            others = [i for i in range(len(DILATIONS)) if i != idx]
            ms = [max_ref[idx, rows, :]] + [max_ref[i, rows, :] for i in others]
            m_all = functools.reduce(jnp.maximum, ms)
            w = jnp.exp2(ms[0] - m_all)
            num, den = w * num, w * den
            for m, i in zip(ms[1:], others):
                w = jnp.exp2(m - m_all)
                num = num + w * num_ref[i, rows, :]
                den = den + w * den_ref[i, rows, :]
            o_ref[rows, :] = (num / den * g_ref[0, rows, :].astype(f32)).astype(o_ref.dtype)

        return scores, softmax, values, finish

    assert DILATIONS[0] == 1
    stages = [branch(2, DILATIONS[2], q16_ref, k16_ref, v16_ref),
              branch(1, DILATIONS[1], q4_ref, k4_ref, v4_ref),
              branch(0, DILATIONS[0], q1_ref, k1_ref, v1_ref)]
    last = n_groups - 1

    def trip(*work):
        for stage, g, slot in work:
            for i in range(DIL_GROUP):
                stage(g, slot, i)

    for b, (scores, softmax, values, _) in enumerate(stages):
        if b == 0:
            trip((scores, 0, 0))
            trip((scores, 1, 1), (softmax, 0, 0))
        else:
            trip((stages[b - 1][2], last, 1), (scores, 1, 1), (softmax, 0, 0))
            stages[b - 1][3]()

        def two_trips(t, carry, scores=scores, softmax=softmax, values=values):
            trip((values, 2 * t, 0), (scores, 2 * t + 2, 0), (softmax, 2 * t + 1, 1))
            trip((values, 2 * t + 1, 1), (scores, 2 * t + 3, 1), (softmax, 2 * t + 2, 0))
            return carry

        for t in range((n_groups - 2) // 2):
            two_trips(t, 0)
        if b + 1 < len(stages):
            trip((values, last - 1, 0), (stages[b + 1][0], 0, 0), (softmax, last, 1))
        else:
            trip((values, last - 1, 0), (softmax, last, 1))
    trip((stages[-1][2], last, 1))


def _out_kernel(x_ref, ya_ref, yb_ref, wa_ref, wb_ref, g_ref, b_ref, o_ref):
    f32 = jnp.float32
    chunks = [pl.ds(r, OUT_CHUNK) for r in range(0, x_ref.shape[0], OUT_CHUNK)]

    def project(rows):
        return (jnp.dot(ya_ref[rows, :], wa_ref[...], preferred_element_type=f32)
                + jnp.dot(yb_ref[rows, :], wb_ref[...], preferred_element_type=f32))

    def normalise(rows, mixed):
        h = DEEPNORM_ALPHA * x_ref[rows, :] + mixed
        mu = jnp.mean(h, axis=-1, keepdims=True)
        c = h - mu
        var = jnp.mean(c * c, axis=-1, keepdims=True)
        o_ref[rows, :] = c * lax.rsqrt(var + LN_EPS) * g_ref[...] + b_ref[...]

    mixed = project(chunks[0])
    for i, rows in enumerate(chunks):
        ahead = project(chunks[i + 1]) if i + 1 < len(chunks) else None
        normalise(rows, mixed)
        mixed = ahead


def _rope_tables(seq, rot_dim, period, rot_offset, pass_rest):
    half = rot_dim // 2
    f32 = np.float32
    inv_freq = ROPE_THETA ** (-np.arange(0, rot_dim, 2, dtype=np.float64) / rot_dim)
    ang = np.arange(seq, dtype=np.float64)[:, None] * inv_freq[None, :]
    cos, sin = np.cos(ang).astype(f32), np.sin(ang).astype(f32)
    zeros = np.zeros((seq, half), f32)
    rest = period - rot_offset - rot_dim
    fill = np.ones if pass_rest else np.zeros
    group_cos = np.concatenate([np.ones((seq, rot_offset), f32), cos, cos, fill((seq, rest), f32)], axis=1)
    group_fwd = np.concatenate([np.zeros((seq, rot_offset), f32), zeros, sin, np.zeros((seq, rest), f32)], axis=1)
    group_bwd = np.concatenate([np.zeros((seq, rot_offset), f32), -sin, zeros, np.zeros((seq, rest), f32)], axis=1)
    reps = LANES // period
    return tuple(jnp.asarray(np.tile(t, (1, reps))) for t in (group_cos, group_fwd, group_bwd))


def _params(*semantics, flags=None):
    return pltpu.CompilerParams(dimension_semantics=semantics, vmem_limit_bytes=VMEM_LIMIT_BYTES, flags=flags)


def kernel(x, w_in, q_norm_g, kv_norm_g, w_uq, w_ukv, w_out, ln_g, ln_b):
    f32, bf16 = jnp.float32, jnp.bfloat16
    batch, seq, _ = x.shape
    rows = batch * seq
    x2 = x.reshape(rows, D_MODEL)

    rope_end = _LATENT_WIDTH + MLA_ROPE_DIM
    w_lat = w_in[:, :_LATENT_WIDTH].astype(bf16)
    w_kpe = jnp.pad(w_in[:, _LATENT_WIDTH:rope_end].astype(bf16),
                    ((0, 0), (MLA_NOPE_DIM, LANES - MLA_NOPE_DIM - MLA_ROPE_DIM)))
    w_rest = w_in[:, rope_end:].astype(bf16)
    assert w_rest.shape[1] == _REST_WIDTH
    dk = MLA_NOPE_DIM + MLA_ROPE_DIM
    wuq = jnp.pad(w_uq.reshape(Q_LORA_RANK, MLA_HEADS, dk), ((0, 0), (0, 0), (0, LANES - dk)))
    wuq = wuq.reshape(Q_LORA_RANK, MLA_HEADS * LANES).astype(bf16)
    wukv = w_ukv.reshape(KV_LORA_RANK, MLA_HEADS, MLA_NOPE_DIM + MLA_V_DIM)
    wuk = jnp.pad(wukv[:, :, :MLA_NOPE_DIM], ((0, 0), (0, 0), (0, LANES - MLA_NOPE_DIM)))
    wuk = wuk.reshape(KV_LORA_RANK, MLA_HEADS * LANES).astype(bf16)
    wuvt = wukv[:, :, MLA_NOPE_DIM:].reshape(KV_LORA_RANK, MLA_WIDTH).T.astype(bf16)
    wa = w_out[:MLA_WIDTH].astype(bf16)
    wb = w_out[MLA_WIDTH:].astype(bf16)

    mla_tabs = _rope_tables(seq, MLA_ROPE_DIM, LANES, MLA_NOPE_DIM, pass_rest=False)
    dil_tabs = _rope_tables(seq, DIL_ROT_DIM, DIL_HEAD_DIM, 0, pass_rest=True)

    tm = PROJ_ROWS
    seq_tiles = seq // tm
    full = lambda shape: pl.BlockSpec(shape, lambda i: (0,) * len(shape))
    tab = pl.BlockSpec((tm, LANES), lambda i: (i % seq_tiles, 0))
    slab = lambda n: pl.BlockSpec((n, tm, LANES), lambda i: (0, i, 0))
    slab_shape = lambda n: jax.ShapeDtypeStruct((n, rows, LANES), bf16)
    tk = MLA_KEY_TILE
    vt_spec = pl.BlockSpec((tm // tk, MLA_WIDTH, tk), lambda i: (i, 0, 0))
    vt_shape = jax.ShapeDtypeStruct((rows // tk, MLA_WIDTH, tk), bf16)
    res_specs, res_shapes = [], []
    for dil in DILATIONS:
        spec = pl.BlockSpec((HEAD_PAIRS, 1, dil, tm // dil, LANES),
                            lambda i: (0, i // seq_tiles, 0, i % seq_tiles, 0))
        shape = jax.ShapeDtypeStruct((HEAD_PAIRS, batch, dil, seq // dil, LANES), bf16)
        res_specs += [spec] * 3
        res_shapes += [shape] * 3
    qm, km, vt, ga, gb, *dil_in = pl.pallas_call(
        _proj_kernel,
        grid=(rows // tm,),
        in_specs=[pl.BlockSpec((tm, D_MODEL), lambda i: (i, 0)),
                  full(w_lat.shape), full(w_kpe.shape), full(w_rest.shape),
                  full(wuq.shape), full(wuk.shape), full(wuvt.shape),
                  full((1, Q_LORA_RANK)), full((1, KV_LORA_RANK)),
                  tab, tab, tab, tab, tab, tab],
        out_specs=[slab(MLA_HEADS), slab(MLA_HEADS), vt_spec, slab(HEAD_PAIRS), slab(HEAD_PAIRS)] + res_specs,
        out_shape=[slab_shape(MLA_HEADS), slab_shape(MLA_HEADS), vt_shape, slab_shape(HEAD_PAIRS),
                   slab_shape(HEAD_PAIRS)] + res_shapes,
        scratch_shapes=[pltpu.VMEM((3 * HEAD_PAIRS, tm, LANES), f32)],
        compiler_params=_params("parallel"),
        name="proj",
    )(x2, w_lat, w_kpe, w_rest, wuq, wuk, wuvt, q_norm_g.reshape(1, -1), kv_norm_g.reshape(1, -1),
      *mla_tabs, *dil_tabs)

    t = MLA_TILE
    ya = pl.pallas_call(
        _mla_kernel,
        grid=(batch, HEAD_PAIRS),
        in_specs=[pl.BlockSpec((2, seq, LANES), lambda b, p: (p, b, 0)),
                  pl.BlockSpec((2, seq, LANES), lambda b, p: (p, b, 0)),
                  pl.BlockSpec((seq // tk, LANES, tk), lambda b, p: (b, p, 0)),
                  pl.BlockSpec((1, seq, LANES), lambda b, p: (p, b, 0))],
        out_specs=pl.BlockSpec((seq, LANES), lambda b, p: (b, p)),
        out_shape=jax.ShapeDtypeStruct((rows, MLA_WIDTH), bf16),
        scratch_shapes=[pltpu.VMEM((2, 2, MLA_V_DIM + MLA_ONES_ROWS, t), f32), pltpu.VMEM((2, 2, tk, t), f32)],
        compiler_params=_params("parallel", "parallel"),
        name="mla",
    )(qm, km, vt, ga)

    dil_specs = [pl.BlockSpec((1, 1, dil, seq // dil, LANES), lambda b, p: (p, b, 0, 0, 0))
                 for dil in DILATIONS for _ in range(3)]
    yb = pl.pallas_call(
        _dilated_kernel,
        grid=(batch, HEAD_PAIRS),
        in_specs=dil_specs + [pl.BlockSpec((1, seq, LANES), lambda b, p: (p, b, 0))],
        out_specs=pl.BlockSpec((seq, LANES), lambda b, p: (b, p)),
        out_shape=jax.ShapeDtypeStruct((rows, DIL_WIDTH), bf16),
        scratch_shapes=[pltpu.VMEM((3, seq, LANES), f32)] * 3 + [
            pltpu.VMEM((3, SCATTER_STRIDE, seq // SCATTER_STRIDE, LANES), f32),
            pltpu.VMEM((2, DIL_GROUP, 2 * BLOCK, 2 * BLOCK), f32),
            pltpu.VMEM((2, DIL_GROUP, 2 * BLOCK, 2 * BLOCK), bf16),
            pltpu.VMEM((2, 2 * BLOCK, 2 * BLOCK), f32)],
        compiler_params=_params("parallel", "parallel"),
        name="dilated",
    )(*dil_in, gb)

    to = OUT_ROWS
    const = lambda shape: pl.BlockSpec(shape, lambda i: (0,) * len(shape))
    out = pl.pallas_call(
        _out_kernel,
        grid=(rows // to,),
        in_specs=[pl.BlockSpec((to, D_MODEL), lambda i: (i, 0)),
                  pl.BlockSpec((to, MLA_WIDTH), lambda i: (i, 0)),
                  pl.BlockSpec((to, DIL_WIDTH), lambda i: (i, 0)),
                  const((MLA_WIDTH, D_MODEL)), const((DIL_WIDTH, D_MODEL)),
                  const((1, D_MODEL)), const((1, D_MODEL))],
        out_specs=pl.BlockSpec((to, D_MODEL), lambda i: (i, 0)),
        out_shape=jax.ShapeDtypeStruct((rows, D_MODEL), f32),
        compiler_params=_params("parallel"),
        name="out",
    )(x2, ya, yb, wa, wb, ln_g.reshape(1, -1), ln_b.reshape(1, -1))
    return out.reshape(batch, seq, D_MODEL)
```

```python
import functools

import jax
import jax.numpy as jnp
import numpy as np
from jax import lax
from jax.experimental import pallas as pl
from jax.experimental.pallas import tpu as pltpu

D_MODEL = 1024
ROPE_THETA = 500000.0
BLOCK = 128
NEG = -1e30
RMS_EPS = 1e-6
LN_EPS = 1e-5

MLA_HEADS = 8
MLA_NOPE_DIM = 64
MLA_ROPE_DIM = 32
MLA_V_DIM = 64
Q_LORA_RANK = 384
KV_LORA_RANK = 256
MLA_WIDTH = MLA_HEADS * MLA_V_DIM

DIL_HEADS = 8
DIL_HEAD_DIM = 64
DIL_ROT_DIM = DIL_HEAD_DIM // 4
DIL_WIDTH = DIL_HEADS * DIL_HEAD_DIM
DILATIONS = (1, 4, 16)

DEPTH = 1
DEEPNORM_ALPHA = (2.0 * DEPTH) ** 0.25
LOG2_E = 1.4426950408889634

LANES = 128
HEAD_PAIRS = MLA_HEADS // 2
VMEM_LIMIT_BYTES = 56 * 1024 * 1024

_LATENT_WIDTH = Q_LORA_RANK + KV_LORA_RANK
_OFF_GA = 0
_OFF_QB = _OFF_GA + MLA_WIDTH
_OFF_KB = _OFF_QB + DIL_WIDTH
_OFF_VB = _OFF_KB + DIL_WIDTH
_OFF_GB = _OFF_VB + DIL_WIDTH
_REST_WIDTH = _OFF_GB + DIL_WIDTH

PROJ_ROWS = 512
MLA_KEY_TILE = 256
MLA_TILE = 1024
MLA_UNROLL_TRIPS = 2
MLA_ONES_ROWS = 16
OUT_ROWS = 2048
OUT_CHUNK = 256
DIL_GROUP = 4
SCATTER_STRIDE = 4


def _rope_lanes(x, cos, sin_fwd, sin_bwd, half):
    fwd = pltpu.roll(x, half, 1)
    bwd = pltpu.roll(x, LANES - half, 1)
    return x * cos + fwd * sin_fwd + bwd * sin_bwd


def _proj_kernel(x_ref, wlat_ref, wkpe_ref, w_ref, wuq_ref, wuk_ref, wuvt_ref, qg_ref, kvg_ref,
                 mcos_ref, msf_ref, msb_ref, dcos_ref, dsf_ref, dsb_ref,
                 qm_ref, km_ref, vt_ref, ga_ref, gb_ref, *dil_and_scratch):
    f32 = jnp.float32
    bf16 = jnp.bfloat16
    xb = x_ref[...].astype(bf16)
    dil_refs = [dil_and_scratch[3 * i:3 * i + 3] for i in range(len(DILATIONS))]
    stage_ref = dil_and_scratch[-1]
    tm = x_ref.shape[0]

    def emit_residues(which, p, val):
        slab = which * HEAD_PAIRS + p
        stage_ref[slab] = val
        for refs, dil in zip(dil_refs, DILATIONS):
            for r in range(dil):
                rows = val if dil == 1 else stage_ref[slab, pl.ds(r, tm // dil, stride=dil), :]
                refs[which][p, 0, r] = rows.astype(bf16)

    def seg(lo, width):
        return jnp.dot(xb, w_ref[:, lo:lo + width], preferred_element_type=f32)

    def rms(t, g):
        return t * lax.rsqrt(jnp.mean(t * t, axis=-1, keepdims=True) + RMS_EPS) * g

    mcos, msf, msb = mcos_ref[...], msf_ref[...], msb_ref[...]
    dcos, dsf, dsb = dcos_ref[...], dsf_ref[...], dsb_ref[...]
    mla_scale = (MLA_NOPE_DIM + MLA_ROPE_DIM) ** -0.5 * LOG2_E
    dil_scale = DIL_HEAD_DIM ** -0.5 * LOG2_E

    unit = 2 * LANES
    latent = {}

    def slabs(t):
        return [t[:, j * LANES:(j + 1) * LANES] for j in range(t.shape[1] // LANES)]

    def dilated(which, lo, first_pair):
        def epilogue(t):
            for j, val in enumerate(slabs(t)):
                if which < 2:
                    val = _rope_lanes(val, dcos, dsf, dsb, DIL_ROT_DIM // 2)
                if which == 0:
                    val = val * dil_scale
                emit_residues(which, first_pair + j, val)
        return (lambda: seg(lo, unit)), epilogue

    def gate(out_ref, lo, first_pair):
        def epilogue(t):
            for j, val in enumerate(slabs(jax.nn.silu(t))):
                out_ref[first_pair + j] = val.astype(bf16)
        return (lambda: seg(lo, unit)), epilogue

    def mla_q(first_head):
        def epilogue(t):
            for j, val in enumerate(slabs(t)):
                val = _rope_lanes(val, mcos, msf, msb, MLA_ROPE_DIM // 2) * mla_scale
                qm_ref[first_head + j] = val.astype(bf16)
        lo = first_head * LANES
        return (lambda: jnp.dot(latent["cq"], wuq_ref[:, lo:lo + unit], preferred_element_type=f32)), epilogue

    def mla_k(first_head):
        def epilogue(t):
            for j, val in enumerate(slabs(t)):
                km_ref[first_head + j] = (val + latent["kpe"]).astype(bf16)
        lo = first_head * LANES
        return (lambda: jnp.dot(latent["ckv"], wuk_ref[:, lo:lo + unit], preferred_element_type=f32)), epilogue

    def store_vt(t):
        for c in range(vt_ref.shape[0]):
            vt_ref[c] = t[:, c * MLA_KEY_TILE:(c + 1) * MLA_KEY_TILE].astype(bf16)

    split = unit
    units = [
        (lambda: jnp.dot(xb, wlat_ref[:, :split], preferred_element_type=f32),
         lambda t: latent.update(cq_head=t)),
        (lambda: jnp.dot(xb, wlat_ref[:, split:Q_LORA_RANK], preferred_element_type=f32),
         lambda t: latent.update(cq=rms(jnp.concatenate([latent["cq_head"], t], axis=1), qg_ref[...]).astype(bf16))),
        (lambda: jnp.dot(xb, wlat_ref[:, Q_LORA_RANK:], preferred_element_type=f32),
         lambda t: latent.update(ckv=rms(t, kvg_ref[...]).astype(bf16))),
        dilated(0, _OFF_QB, 0), mla_q(0), dilated(0, _OFF_QB + unit, 2), mla_q(2),
        dilated(1, _OFF_KB, 0), mla_q(4), dilated(1, _OFF_KB + unit, 2), mla_q(6),
        (lambda: jnp.dot(xb, wkpe_ref[...], preferred_element_type=f32),
         lambda t: latent.update(kpe=_rope_lanes(t, mcos, msf, msb, MLA_ROPE_DIM // 2))),
        mla_k(0), dilated(2, _OFF_VB, 0), mla_k(2), dilated(2, _OFF_VB + unit, 2),
        mla_k(4), gate(ga_ref, _OFF_GA, 0), mla_k(6), gate(ga_ref, _OFF_GA + unit, 2),
        (lambda: lax.dot_general(wuvt_ref[...], latent["ckv"], (((1,), (1,)), ((), ())),
                                 preferred_element_type=f32), store_vt),
        gate(gb_ref, _OFF_GB, 0), gate(gb_ref, _OFF_GB + unit, 2),
    ]
    pending = None
    for matmul, epilogue in units:
        result = matmul()
        if pending is not None:
            pending[0](pending[1])
        pending = (epilogue, result)
    pending[0](pending[1])


def _mla_kernel(q_ref, k_ref, vt_ref, g_ref, o_ref, accs_ref, s_ref):
    f32 = jnp.float32
    bf16 = jnp.bfloat16
    tq, tk = MLA_TILE, MLA_KEY_TILE
    n_chunks = tq // tk
    assert n_chunks % 2 == 0
    nq = q_ref.shape[1] // tq
    units = [(h, c) for c in range(n_chunks) for h in range(2)]
    ones = jnp.ones((MLA_ONES_ROWS, tk), bf16)
    key = lax.broadcasted_iota(jnp.int32, (tk, tk), 0)
    qry = lax.broadcasted_iota(jnp.int32, (tk, tk), 1)
    lower = key <= qry

    def lanes(c):
        return slice(c * tk, (c + 1) * tk)

    def scores(qi, j, slot, h, c):
        q = q_ref[h, pl.ds(qi * tq + c * tk, tk), :]
        k = k_ref[h, pl.ds(j * tk if isinstance(j, int) else pl.multiple_of(j * tk, tk), tk), :]
        s = lax.dot_general(k, q, (((1,), (1,)), ((), ())), preferred_element_type=f32)
        s_ref[slot, h, :, lanes(c)] = s
        return jnp.max(s, axis=0, keepdims=True)

    def absorb(acc_ref, j, slot, h, c, m_tile, m_old, masked=False):
        s = s_ref[slot, h, :, lanes(c)]
        if masked:
            s = jnp.where(lower, s, NEG)
            m_tile = jnp.max(s, axis=0, keepdims=True)
        m_new = jnp.maximum(m_old, m_tile)
        alpha = jnp.exp2(m_old - m_new)
        p = jnp.exp2(s - m_new).astype(bf16)
        vt = jnp.concatenate([vt_ref[j, h * MLA_V_DIM:(h + 1) * MLA_V_DIM, :], ones], axis=0)
        acc_ref[h, :, lanes(c)] = alpha * acc_ref[h, :, lanes(c)] + jnp.dot(vt, p, preferred_element_type=f32)
        return m_new

    def per_query_tile(qi, acc_ref):
        rows = pl.ds(qi * tq, tq)
        acc_ref[...] = jnp.zeros_like(acc_ref)

        def step(j, slot, tile_max, m_run):
            next_max, m_new = {}, {}
            for u in units:
                next_max[u] = scores(qi, j + 1, 1 - slot, *u)
                m_new[u] = absorb(acc_ref, j, slot, *u, tile_max[u], m_run[u])
            return next_max, m_new

        def body(jj, carry):
            tile_max, m_run = carry
            for i in range(n_chunks):
                tile_max, m_run = step(n_chunks * jj + i, i % 2, tile_max, m_run)
            return tile_max, m_run

        start = {u: jnp.full((1, tk), NEG, f32) for u in units}
        first = {u: scores(qi, 0, 0, *u) for u in units}
        tile_max, m_run = lax.fori_loop(0, qi, body, (first, start), unroll=qi <= MLA_UNROLL_TRIPS)

        base = qi * n_chunks
        for d in range(n_chunks):
            next_max = {}
            for h, c in units:
                if c > d:
                    next_max[h, c] = scores(qi, base + d + 1, (d + 1) % 2, h, c)
                if c >= d:
                    m_run[h, c] = absorb(acc_ref, base + d, d % 2, h, c, tile_max[h, c], m_run[h, c],
                                         masked=(c == d))
            tile_max = next_max

        yt = jnp.concatenate([acc_ref[h, :MLA_V_DIM, :] / acc_ref[h, MLA_V_DIM:MLA_V_DIM + 1, :] for h in range(2)],
                             axis=0)
        o_ref[rows, :] = (yt.T * g_ref[0, rows, :].astype(f32)).astype(o_ref.dtype)

    for qi in range(nq):
        per_query_tile(qi, accs_ref.at[qi % 2])


def _dilated_kernel(q1_ref, k1_ref, v1_ref, q4_ref, k4_ref, v4_ref, q16_ref, k16_ref, v16_ref,
                    g_ref, o_ref, num_ref, max_ref, den_ref, hop_ref, s_ref, p_ref, bias_ref):
    f32 = jnp.float32
    seq = num_ref.shape[1]

    lane = lax.broadcasted_iota(jnp.int32, (BLOCK, LANES), 1)
    low_half = lane < DIL_HEAD_DIM
    ones = jnp.ones((2 * BLOCK, LANES), jnp.bfloat16)
    qi = lax.broadcasted_iota(jnp.int32, (2 * BLOCK, 2 * BLOCK), 0) % BLOCK
    kj = lax.broadcasted_iota(jnp.int32, (2 * BLOCK, 2 * BLOCK), 1)
    dist_first = qi - kj
    dist_later = dist_first + BLOCK
    for kind, dist in enumerate((dist_first, dist_later)):
        bias_ref[kind] = jnp.where((dist >= 0) & (dist <= BLOCK), 0.0, NEG).astype(f32)

    n_groups = seq // BLOCK // DIL_GROUP
    assert n_groups % 2 == 0 and n_groups >= 4

    def branch(idx, dil, q_ref, k_ref, v_ref):
        per_residue = q_ref.shape[3] // BLOCK

        def locate(b):
            return b // per_residue, b % per_residue

        def key_rows(n):
            return pl.ds(pl.multiple_of(jnp.maximum(n - 1, 0) * BLOCK, BLOCK), 2 * BLOCK)

        def out_rows(r, n):
            if dil == 1:
                return pl.ds(pl.multiple_of(n * BLOCK, BLOCK), BLOCK)
            return pl.ds(r + n * (BLOCK * dil), BLOCK, stride=dil)

        two_hops = dil > SCATTER_STRIDE
        inner = dil // SCATTER_STRIDE

        def scatter(kind, stat_ref, r, n, val):
            if two_hops:
                rows = pl.ds(r // SCATTER_STRIDE + n * (BLOCK * inner), BLOCK, stride=inner)
                hop_ref[kind, r % SCATTER_STRIDE, rows, :] = val
            else:
                stat_ref[idx, out_rows(r, n), :] = val

        def finish():
            if not two_hops:
                return
            for kind, stat_ref in enumerate((num_ref, max_ref, den_ref)):
                for a in range(SCATTER_STRIDE):
                    for c in range(hop_ref.shape[2] // BLOCK):
                        rows = pl.ds(a + c * (BLOCK * SCATTER_STRIDE), BLOCK, stride=SCATTER_STRIDE)
                        stat_ref[idx, rows, :] = hop_ref[kind, a, c * BLOCK:(c + 1) * BLOCK, :]

        def scores(g, slot, i):
            r, n = locate(g * DIL_GROUP + i)
            q = q_ref[0, 0, r, pl.ds(pl.multiple_of(n * BLOCK, BLOCK), BLOCK), :]
            k = k_ref[0, 0, r, key_rows(n), :]
            zero = jnp.zeros_like(q)
            q2 = jnp.concatenate([jnp.where(low_half, q, zero), jnp.where(low_half, zero, q)], axis=0)
            s_ref[slot, i] = lax.dot_general(q2, k, (((1,), (1,)), ((), ())), preferred_element_type=f32)

        def softmax(g, slot, i):
            r, n = locate(g * DIL_GROUP + i)
            s = s_ref[slot, i] + bias_ref[jnp.minimum(n, 1)]
            m = jnp.max(s, axis=1, keepdims=True)
            p_ref[slot, i] = jnp.exp2(s - m).astype(jnp.bfloat16)
            scatter(1, max_ref, r, n, jnp.where(low_half, m[:BLOCK], m[BLOCK:]))

        def values(g, slot, i):
            r, n = locate(g * DIL_GROUP + i)
            v = jnp.concatenate([v_ref[0, 0, r, key_rows(n), :], ones], axis=1)
            both = jnp.dot(p_ref[slot, i], v, preferred_element_type=f32)
            num, den = both[:, :LANES], both[:, LANES:]
            num = jnp.where(low_half, num[:BLOCK], num[BLOCK:])
            den = jnp.where(low_half, den[:BLOCK], den[BLOCK:])
            if dil != 1:
                scatter(0, num_ref, r, n, num)
                scatter(2, den_ref, r, n, den)
                return
            rows = out_rows(r, n)
            others = [i for i in range(len(DILATIONS)) if i != idx]
            ms = [max_ref[idx, rows, :]] + [max_ref[i, rows, :] for i in others]
            m_all = functools.reduce(jnp.maximum, ms)
            w = jnp.exp2(ms[0] - m_all)
            num, den = w * num, w * den
            for m, i in zip(ms[1:], others):
                w = jnp.exp2(m - m_all)
                num = num + w * num_ref[i, rows, :]
                den = den + w * den_ref[i, rows, :]
            o_ref[rows, :] = (num / den * g_ref[0, rows, :].astype(f32)).astype(o_ref.dtype)

        return scores, softmax, values, finish

    assert DILATIONS[0] == 1
    stages = [branch(2, DILATIONS[2], q16_ref, k16_ref, v16_ref),
              branch(1, DILATIONS[1], q4_ref, k4_ref, v4_ref),
              branch(0, DILATIONS[0], q1_ref, k1_ref, v1_ref)]
    last = n_groups - 1

    def trip(*work):
        for stage, g, slot in work:
            for i in range(DIL_GROUP):
                stage(g, slot, i)

    for b, (scores, softmax, values, _) in enumerate(stages):
        if b == 0:
            trip((scores, 0, 0))
            trip((scores, 1, 1), (softmax, 0, 0))
        else:
            trip((stages[b - 1][2], last, 1), (scores, 1, 1), (softmax, 0, 0))
            stages[b - 1][3]()

        def two_trips(t, carry, scores=scores, softmax=softmax, values=values):
            trip((values, 2 * t, 0), (scores, 2 * t + 2, 0), (softmax, 2 * t + 1, 1))
            trip((values, 2 * t + 1, 1), (scores, 2 * t + 3, 1), (softmax, 2 * t + 2, 0))
            return carry

        for t in range((n_groups - 2) // 2):
            two_trips(t, 0)
        if b + 1 < len(stages):
            trip((values, last - 1, 0), (stages[b + 1][0], 0, 0), (softmax, last, 1))
        else:
            trip((values, last - 1, 0), (softmax, last, 1))
    trip((stages[-1][2], last, 1))


def _out_kernel(x_ref, ya_ref, yb_ref, wa_ref, wb_ref, g_ref, b_ref, o_ref):
    f32 = jnp.float32
    chunks = [pl.ds(r, OUT_CHUNK) for r in range(0, x_ref.shape[0], OUT_CHUNK)]

    def project(rows):
        return (jnp.dot(ya_ref[rows, :], wa_ref[...], preferred_element_type=f32)
                + jnp.dot(yb_ref[rows, :], wb_ref[...], preferred_element_type=f32))

    def normalise(rows, mixed):
        h = DEEPNORM_ALPHA * x_ref[rows, :] + mixed
        mu = jnp.mean(h, axis=-1, keepdims=True)
        c = h - mu
        var = jnp.mean(c * c, axis=-1, keepdims=True)
        o_ref[rows, :] = c * lax.rsqrt(var + LN_EPS) * g_ref[...] + b_ref[...]

    mixed = project(chunks[0])
    for i, rows in enumerate(chunks):
        ahead = project(chunks[i + 1]) if i + 1 < len(chunks) else None
        normalise(rows, mixed)
        mixed = ahead


def _rope_tables(seq, rot_dim, period, rot_offset, pass_rest):
    half = rot_dim // 2
    f32 = np.float32
    inv_freq = ROPE_THETA ** (-np.arange(0, rot_dim, 2, dtype=np.float64) / rot_dim)
    ang = np.arange(seq, dtype=np.float64)[:, None] * inv_freq[None, :]
    cos, sin = np.cos(ang).astype(f32), np.sin(ang).astype(f32)
    zeros = np.zeros((seq, half), f32)
    rest = period - rot_offset - rot_dim
    fill = np.ones if pass_rest else np.zeros
    group_cos = np.concatenate([np.ones((seq, rot_offset), f32), cos, cos, fill((seq, rest), f32)], axis=1)
    group_fwd = np.concatenate([np.zeros((seq, rot_offset), f32), zeros, sin, np.zeros((seq, rest), f32)], axis=1)
    group_bwd = np.concatenate([np.zeros((seq, rot_offset), f32), -sin, zeros, np.zeros((seq, rest), f32)], axis=1)
    reps = LANES // period
    return tuple(jnp.asarray(np.tile(t, (1, reps))) for t in (group_cos, group_fwd, group_bwd))


def _params(*semantics, flags=None):
    return pltpu.CompilerParams(dimension_semantics=semantics, vmem_limit_bytes=VMEM_LIMIT_BYTES, flags=flags)


def kernel(x, w_in, q_norm_g, kv_norm_g, w_uq, w_ukv, w_out, ln_g, ln_b):
    f32, bf16 = jnp.float32, jnp.bfloat16
    batch, seq, _ = x.shape
    rows = batch * seq
    x2 = x.reshape(rows, D_MODEL)

    rope_end = _LATENT_WIDTH + MLA_ROPE_DIM
    w_lat = w_in[:, :_LATENT_WIDTH].astype(bf16)
    w_kpe = jnp.pad(w_in[:, _LATENT_WIDTH:rope_end].astype(bf16),
                    ((0, 0), (MLA_NOPE_DIM, LANES - MLA_NOPE_DIM - MLA_ROPE_DIM)))
    w_rest = w_in[:, rope_end:].astype(bf16)
    assert w_rest.shape[1] == _REST_WIDTH
    dk = MLA_NOPE_DIM + MLA_ROPE_DIM
    wuq = jnp.pad(w_uq.reshape(Q_LORA_RANK, MLA_HEADS, dk), ((0, 0), (0, 0), (0, LANES - dk)))
    wuq = wuq.reshape(Q_LORA_RANK, MLA_HEADS * LANES).astype(bf16)
    wukv = w_ukv.reshape(KV_LORA_RANK, MLA_HEADS, MLA_NOPE_DIM + MLA_V_DIM)
    wuk = jnp.pad(wukv[:, :, :MLA_NOPE_DIM], ((0, 0), (0, 0), (0, LANES - MLA_NOPE_DIM)))
    wuk = wuk.reshape(KV_LORA_RANK, MLA_HEADS * LANES).astype(bf16)
    wuvt = wukv[:, :, MLA_NOPE_DIM:].reshape(KV_LORA_RANK, MLA_WIDTH).T.astype(bf16)
    wa = w_out[:MLA_WIDTH].astype(bf16)
    wb = w_out[MLA_WIDTH:].astype(bf16)

    mla_tabs = _rope_tables(seq, MLA_ROPE_DIM, LANES, MLA_NOPE_DIM, pass_rest=False)
    dil_tabs = _rope_tables(seq, DIL_ROT_DIM, DIL_HEAD_DIM, 0, pass_rest=True)

    tm = PROJ_ROWS
    seq_tiles = seq // tm
    full = lambda shape: pl.BlockSpec(shape, lambda i: (0,) * len(shape))
    tab = pl.BlockSpec((tm, LANES), lambda i: (i % seq_tiles, 0))
    slab = lambda n: pl.BlockSpec((n, tm, LANES), lambda i: (0, i, 0))
    slab_shape = lambda n: jax.ShapeDtypeStruct((n, rows, LANES), bf16)
    tk = MLA_KEY_TILE
    vt_spec = pl.BlockSpec((tm // tk, MLA_WIDTH, tk), lambda i: (i, 0, 0))
    vt_shape = jax.ShapeDtypeStruct((rows // tk, MLA_WIDTH, tk), bf16)
    res_specs, res_shapes = [], []
    for dil in DILATIONS:
        spec = pl.BlockSpec((HEAD_PAIRS, 1, dil, tm // dil, LANES),
                            lambda i: (0, i // seq_tiles, 0, i % seq_tiles, 0))
        shape = jax.ShapeDtypeStruct((HEAD_PAIRS, batch, dil, seq // dil, LANES), bf16)
        res_specs += [spec] * 3
        res_shapes += [shape] * 3
    qm, km, vt, ga, gb, *dil_in = pl.pallas_call(
        _proj_kernel,
        grid=(rows // tm,),
        in_specs=[pl.BlockSpec((tm, D_MODEL), lambda i: (i, 0)),
                  full(w_lat.shape), full(w_kpe.shape), full(w_rest.shape),
                  full(wuq.shape), full(wuk.shape), full(wuvt.shape),
                  full((1, Q_LORA_RANK)), full((1, KV_LORA_RANK)),
                  tab, tab, tab, tab, tab, tab],
        out_specs=[slab(MLA_HEADS), slab(MLA_HEADS), vt_spec, slab(HEAD_PAIRS), slab(HEAD_PAIRS)] + res_specs,
        out_shape=[slab_shape(MLA_HEADS), slab_shape(MLA_HEADS), vt_shape, slab_shape(HEAD_PAIRS),
                   slab_shape(HEAD_PAIRS)] + res_shapes,
        scratch_shapes=[pltpu.VMEM((3 * HEAD_PAIRS, tm, LANES), f32)],
        compiler_params=_params("parallel"),
        name="proj",
    )(x2, w_lat, w_kpe, w_rest, wuq, wuk, wuvt, q_norm_g.reshape(1, -1), kv_norm_g.reshape(1, -1),
      *mla_tabs, *dil_tabs)

    t = MLA_TILE
    ya = pl.pallas_call(
        _mla_kernel,
        grid=(batch, HEAD_PAIRS),
        in_specs=[pl.BlockSpec((2, seq, LANES), lambda b, p: (p, b, 0)),
                  pl.BlockSpec((2, seq, LANES), lambda b, p: (p, b, 0)),
                  pl.BlockSpec((seq // tk, LANES, tk), lambda b, p: (b, p, 0)),
                  pl.BlockSpec((1, seq, LANES), lambda b, p: (p, b, 0))],
        out_specs=pl.BlockSpec((seq, LANES), lambda b, p: (b, p)),
        out_shape=jax.ShapeDtypeStruct((rows, MLA_WIDTH), bf16),
        scratch_shapes=[pltpu.VMEM((2, 2, MLA_V_DIM + MLA_ONES_ROWS, t), f32), pltpu.VMEM((2, 2, tk, t), f32)],
        compiler_params=_params("parallel", "parallel"),
        name="mla",
    )(qm, km, vt, ga)

    dil_specs = [pl.BlockSpec((1, 1, dil, seq // dil, LANES), lambda b, p: (p, b, 0, 0, 0))
                 for dil in DILATIONS for _ in range(3)]
    yb = pl.pallas_call(
        _dilated_kernel,
        grid=(batch, HEAD_PAIRS),
        in_specs=dil_specs + [pl.BlockSpec((1, seq, LANES), lambda b, p: (p, b, 0))],
        out_specs=pl.BlockSpec((seq, LANES), lambda b, p: (b, p)),
        out_shape=jax.ShapeDtypeStruct((rows, DIL_WIDTH), bf16),
        scratch_shapes=[pltpu.VMEM((3, seq, LANES), f32)] * 3 + [
            pltpu.VMEM((3, SCATTER_STRIDE, seq // SCATTER_STRIDE, LANES), f32),
            pltpu.VMEM((2, DIL_GROUP, 2 * BLOCK, 2 * BLOCK), f32),
            pltpu.VMEM((2, DIL_GROUP, 2 * BLOCK, 2 * BLOCK), bf16),
            pltpu.VMEM((2, 2 * BLOCK, 2 * BLOCK), f32)],
        compiler_params=_params("parallel", "parallel"),
        name="dilated",
    )(*dil_in, gb)

    to = OUT_ROWS
    const = lambda shape: pl.BlockSpec(shape, lambda i: (0,) * len(shape))
    out = pl.pallas_call(
        _out_kernel,
        grid=(rows // to,),
        in_specs=[pl.BlockSpec((to, D_MODEL), lambda i: (i, 0)),
                  pl.BlockSpec((to, MLA_WIDTH), lambda i: (i, 0)),
                  pl.BlockSpec((to, DIL_WIDTH), lambda i: (i, 0)),
                  const((MLA_WIDTH, D_MODEL)), const((DIL_WIDTH, D_MODEL)),
                  const((1, D_MODEL)), const((1, D_MODEL))],
        out_specs=pl.BlockSpec((to, D_MODEL), lambda i: (i, 0)),
        out_shape=jax.ShapeDtypeStruct((rows, D_MODEL), f32),
        compiler_params=_params("parallel"),
        name="out",
    )(x2, ya, yb, wa, wb, ln_g.reshape(1, -1), ln_b.reshape(1, -1))
    return out.reshape(batch, seq, D_MODEL)
```

```python
import functools

import jax
import jax.numpy as jnp
import numpy as np
from jax import lax
from jax.experimental import pallas as pl
from jax.experimental.pallas import tpu as pltpu

D_MODEL = 1024
ROPE_THETA = 500000.0
BLOCK = 128
NEG = -1e30
RMS_EPS = 1e-6
LN_EPS = 1e-5

MLA_HEADS = 8
MLA_NOPE_DIM = 64
MLA_ROPE_DIM = 32
MLA_V_DIM = 64
Q_LORA_RANK = 384
KV_LORA_RANK = 256
MLA_WIDTH = MLA_HEADS * MLA_V_DIM

DIL_HEADS = 8
DIL_HEAD_DIM = 64
DIL_ROT_DIM = DIL_HEAD_DIM // 4
DIL_WIDTH = DIL_HEADS * DIL_HEAD_DIM
DILATIONS = (1, 4, 16)

DEPTH = 1
DEEPNORM_ALPHA = (2.0 * DEPTH) ** 0.25
LOG2_E = 1.4426950408889634

LANES = 128
HEAD_PAIRS = MLA_HEADS // 2
VMEM_LIMIT_BYTES = 56 * 1024 * 1024

_LATENT_WIDTH = Q_LORA_RANK + KV_LORA_RANK
_OFF_GA = 0
_OFF_QB = _OFF_GA + MLA_WIDTH
_OFF_KB = _OFF_QB + DIL_WIDTH
_OFF_VB = _OFF_KB + DIL_WIDTH
_OFF_GB = _OFF_VB + DIL_WIDTH
_REST_WIDTH = _OFF_GB + DIL_WIDTH

PROJ_ROWS = 512
MLA_KEY_TILE = 256
MLA_TILE = 1024
MLA_UNROLL_TRIPS = 2
MLA_ONES_ROWS = 16
OUT_ROWS = 2048
OUT_CHUNK = 256
DIL_GROUP = 4
SCATTER_STRIDE = 4


def _rope_lanes(x, cos, sin_fwd, sin_bwd, half):
    fwd = pltpu.roll(x, half, 1)
    bwd = pltpu.roll(x, LANES - half, 1)
    return x * cos + fwd * sin_fwd + bwd * sin_bwd


def _proj_kernel(x_ref, wlat_ref, wkpe_ref, w_ref, wuq_ref, wuk_ref, wuvt_ref, qg_ref, kvg_ref,
                 mcos_ref, msf_ref, msb_ref, dcos_ref, dsf_ref, dsb_ref,
                 qm_ref, km_ref, vt_ref, ga_ref, gb_ref, *dil_and_scratch):
    f32 = jnp.float32
    bf16 = jnp.bfloat16
    xb = x_ref[...].astype(bf16)
    dil_refs = [dil_and_scratch[3 * i:3 * i + 3] for i in range(len(DILATIONS))]
    stage_ref, stage4_ref = dil_and_scratch[-2:]
    tm = x_ref.shape[0]
    assert DILATIONS == (1, 4, 16)

    def emit_residues(which, p, val):
        slab = which * HEAD_PAIRS + p
        q1, q4, q16 = (refs[which] for refs in dil_refs)
        q1[p, 0, 0] = val.astype(bf16)
        stage_ref[slab] = val
        for a in range(4):
            rows4 = stage_ref[slab, pl.ds(a, tm // 4, stride=4), :]
            q4[p, 0, a] = rows4.astype(bf16)
            stage4_ref[slab, a] = rows4
        for a in range(4):
            for b in range(4):
                rows16 = stage4_ref[slab, a, pl.ds(b, tm // 16, stride=4), :]
                q16[p, 0, a + 4 * b] = rows16.astype(bf16)

    def seg(lo, width):
        return jnp.dot(xb, w_ref[:, lo:lo + width], preferred_element_type=f32)

    def rms(t, g):
        return t * lax.rsqrt(jnp.mean(t * t, axis=-1, keepdims=True) + RMS_EPS) * g

    mcos, msf, msb = mcos_ref[...], msf_ref[...], msb_ref[...]
    dcos, dsf, dsb = dcos_ref[...], dsf_ref[...], dsb_ref[...]
    mla_scale = (MLA_NOPE_DIM + MLA_ROPE_DIM) ** -0.5 * LOG2_E
    dil_scale = DIL_HEAD_DIM ** -0.5 * LOG2_E

    unit = 2 * LANES
    latent = {}

    def slabs(t):
        return [t[:, j * LANES:(j + 1) * LANES] for j in range(t.shape[1] // LANES)]

    def dilated(which, lo, first_pair):
        def epilogue(t):
            for j, val in enumerate(slabs(t)):
                if which < 2:
                    val = _rope_lanes(val, dcos, dsf, dsb, DIL_ROT_DIM // 2)
                if which == 0:
                    val = val * dil_scale
                emit_residues(which, first_pair + j, val)
        return (lambda: seg(lo, unit)), epilogue

    def gate(out_ref, lo, first_pair):
        def epilogue(t):
            for j, val in enumerate(slabs(jax.nn.silu(t))):
                out_ref[first_pair + j] = val.astype(bf16)
        return (lambda: seg(lo, unit)), epilogue

    def mla_q(first_head):
        def epilogue(t):
            for j, val in enumerate(slabs(t)):
                val = _rope_lanes(val, mcos, msf, msb, MLA_ROPE_DIM // 2) * mla_scale
                qm_ref[first_head + j] = val.astype(bf16)
        lo = first_head * LANES
        return (lambda: jnp.dot(latent["cq"], wuq_ref[:, lo:lo + unit], preferred_element_type=f32)), epilogue

    def mla_k(first_head):
        def epilogue(t):
            for j, val in enumerate(slabs(t)):
                km_ref[first_head + j] = (val + latent["kpe"]).astype(bf16)
        lo = first_head * LANES
        return (lambda: jnp.dot(latent["ckv"], wuk_ref[:, lo:lo + unit], preferred_element_type=f32)), epilogue

    def store_vt(t):
        for c in range(vt_ref.shape[0]):
            vt_ref[c] = t[:, c * MLA_KEY_TILE:(c + 1) * MLA_KEY_TILE].astype(bf16)

    split = unit
    units = [
        (lambda: jnp.dot(xb, wlat_ref[:, :split], preferred_element_type=f32),
         lambda t: latent.update(cq_head=t)),
        (lambda: jnp.dot(xb, wlat_ref[:, split:Q_LORA_RANK], preferred_element_type=f32),
         lambda t: latent.update(cq=rms(jnp.concatenate([latent["cq_head"], t], axis=1), qg_ref[...]).astype(bf16))),
        (lambda: jnp.dot(xb, wlat_ref[:, Q_LORA_RANK:], preferred_element_type=f32),
         lambda t: latent.update(ckv=rms(t, kvg_ref[...]).astype(bf16))),
        dilated(0, _OFF_QB, 0), mla_q(0), dilated(0, _OFF_QB + unit, 2), mla_q(2),
        dilated(1, _OFF_KB, 0), mla_q(4), dilated(1, _OFF_KB + unit, 2), mla_q(6),
        (lambda: jnp.dot(xb, wkpe_ref[...], preferred_element_type=f32),
         lambda t: latent.update(kpe=_rope_lanes(t, mcos, msf, msb, MLA_ROPE_DIM // 2))),
        gate(ga_ref, _OFF_GA, 0), mla_k(0), dilated(2, _OFF_VB, 0), mla_k(2),
        gate(ga_ref, _OFF_GA + unit, 2), mla_k(4), dilated(2, _OFF_VB + unit, 2), mla_k(6),
        gate(gb_ref, _OFF_GB, 0), gate(gb_ref, _OFF_GB + unit, 2),
        (lambda: lax.dot_general(wuvt_ref[...], latent["ckv"], (((1,), (1,)), ((), ())),
                                 preferred_element_type=f32), store_vt),
    ]
    pending = None
    for matmul, epilogue in units:
        result = matmul()
        if pending is not None:
            pending[0](pending[1])
        pending = (epilogue, result)
    pending[0](pending[1])


def _mla_kernel(q_ref, k_ref, vt_ref, g_ref, o_ref, accs_ref, s_ref):
    f32 = jnp.float32
    bf16 = jnp.bfloat16
    tq, tk = MLA_TILE, MLA_KEY_TILE
    n_chunks = tq // tk
    assert n_chunks % 2 == 0
    nq = q_ref.shape[1] // tq
    units = [(h, c) for c in range(n_chunks) for h in range(2)]
    ones = jnp.ones((MLA_ONES_ROWS, tk), bf16)
    key = lax.broadcasted_iota(jnp.int32, (tk, tk), 0)
    qry = lax.broadcasted_iota(jnp.int32, (tk, tk), 1)
    lower = key <= qry

    def lanes(c):
        return slice(c * tk, (c + 1) * tk)

    def scores(qi, j, slot, h, c):
        q = q_ref[h, pl.ds(qi * tq + c * tk, tk), :]
        k = k_ref[h, pl.ds(j * tk if isinstance(j, int) else pl.multiple_of(j * tk, tk), tk), :]
        s = lax.dot_general(k, q, (((1,), (1,)), ((), ())), preferred_element_type=f32)
        s_ref[slot, h, :, lanes(c)] = s
        return jnp.max(s, axis=0, keepdims=True)

    def absorb(acc_ref, j, slot, h, c, m_tile, m_old, masked=False):
        s = s_ref[slot, h, :, lanes(c)]
        if masked:
            s = jnp.where(lower, s, NEG)
            m_tile = jnp.max(s, axis=0, keepdims=True)
        m_new = jnp.maximum(m_old, m_tile)
        alpha = jnp.exp2(m_old - m_new)
        p = jnp.exp2(s - m_new).astype(bf16)
        vt = jnp.concatenate([vt_ref[j, h * MLA_V_DIM:(h + 1) * MLA_V_DIM, :], ones], axis=0)
        acc_ref[h, :, lanes(c)] = alpha * acc_ref[h, :, lanes(c)] + jnp.dot(vt, p, preferred_element_type=f32)
        return m_new

    def per_query_tile(qi, acc_ref):
        rows = pl.ds(qi * tq, tq)
        acc_ref[...] = jnp.zeros_like(acc_ref)

        def step(j, slot, tile_max, m_run):
            next_max, m_new = {}, {}
            for u in units:
                next_max[u] = scores(qi, j + 1, 1 - slot, *u)
                m_new[u] = absorb(acc_ref, j, slot, *u, tile_max[u], m_run[u])
            return next_max, m_new

        def body(jj, carry):
            tile_max, m_run = carry
            for i in range(n_chunks):
                tile_max, m_run = step(n_chunks * jj + i, i % 2, tile_max, m_run)
            return tile_max, m_run

        start = {u: jnp.full((1, tk), NEG, f32) for u in units}
        first = {u: scores(qi, 0, 0, *u) for u in units}
        tile_max, m_run = lax.fori_loop(0, qi, body, (first, start), unroll=qi <= MLA_UNROLL_TRIPS)

        base = qi * n_chunks
        for d in range(n_chunks):
            next_max = {}
            for h, c in units:
                if c > d:
                    next_max[h, c] = scores(qi, base + d + 1, (d + 1) % 2, h, c)
                if c >= d:
                    m_run[h, c] = absorb(acc_ref, base + d, d % 2, h, c, tile_max[h, c], m_run[h, c],
                                         masked=(c == d))
            tile_max = next_max

        yt = jnp.concatenate([acc_ref[h, :MLA_V_DIM, :] / acc_ref[h, MLA_V_DIM:MLA_V_DIM + 1, :] for h in range(2)],
                             axis=0)
        o_ref[rows, :] = (yt.T * g_ref[0, rows, :].astype(f32)).astype(o_ref.dtype)

    for qi in range(nq):
        per_query_tile(qi, accs_ref.at[qi % 2])


def _dilated_kernel(q1_ref, k1_ref, v1_ref, q4_ref, k4_ref, v4_ref, q16_ref, k16_ref, v16_ref,
                    g_ref, o_ref, num_ref, max_ref, den_ref, hop_ref, s_ref, p_ref, bias_ref):
    f32 = jnp.float32
    seq = num_ref.shape[1]

    lane = lax.broadcasted_iota(jnp.int32, (BLOCK, LANES), 1)
    low_half = lane < DIL_HEAD_DIM
    ones = jnp.ones((2 * BLOCK, LANES), jnp.bfloat16)
    qi = lax.broadcasted_iota(jnp.int32, (2 * BLOCK, 2 * BLOCK), 0) % BLOCK
    kj = lax.broadcasted_iota(jnp.int32, (2 * BLOCK, 2 * BLOCK), 1)
    dist_first = qi - kj
    dist_later = dist_first + BLOCK
    for kind, dist in enumerate((dist_first, dist_later)):
        bias_ref[kind] = jnp.where((dist >= 0) & (dist <= BLOCK), 0.0, NEG).astype(f32)

    n_groups = seq // BLOCK // DIL_GROUP
    assert n_groups % 2 == 0 and n_groups >= 4

    def branch(idx, dil, q_ref, k_ref, v_ref):
        per_residue = q_ref.shape[3] // BLOCK

        def locate(b):
            return b // per_residue, b % per_residue

        def key_rows(n):
            return pl.ds(pl.multiple_of(jnp.maximum(n - 1, 0) * BLOCK, BLOCK), 2 * BLOCK)

        def out_rows(r, n):
            if dil == 1:
                return pl.ds(pl.multiple_of(n * BLOCK, BLOCK), BLOCK)
            return pl.ds(r + n * (BLOCK * dil), BLOCK, stride=dil)

        two_hops = dil > SCATTER_STRIDE
        inner = dil // SCATTER_STRIDE

        def scatter(kind, stat_ref, r, n, val):
            if two_hops:
                rows = pl.ds(r // SCATTER_STRIDE + n * (BLOCK * inner), BLOCK, stride=inner)
                hop_ref[kind, r % SCATTER_STRIDE, rows, :] = val
            else:
                stat_ref[idx, out_rows(r, n), :] = val

        def finish():
            if not two_hops:
                return
            for kind, stat_ref in enumerate((num_ref, max_ref, den_ref)):
                for a in range(SCATTER_STRIDE):
                    for c in range(hop_ref.shape[2] // BLOCK):
                        rows = pl.ds(a + c * (BLOCK * SCATTER_STRIDE), BLOCK, stride=SCATTER_STRIDE)
                        stat_ref[idx, rows, :] = hop_ref[kind, a, c * BLOCK:(c + 1) * BLOCK, :]

        def scores(g, slot, i):
            r, n = locate(g * DIL_GROUP + i)
            q = q_ref[0, 0, r, pl.ds(pl.multiple_of(n * BLOCK, BLOCK), BLOCK), :]
            k = k_ref[0, 0, r, key_rows(n), :]
            zero = jnp.zeros_like(q)
            q2 = jnp.concatenate([jnp.where(low_half, q, zero), jnp.where(low_half, zero, q)], axis=0)
            s_ref[slot, i] = lax.dot_general(q2, k, (((1,), (1,)), ((), ())), preferred_element_type=f32)

        def softmax(g, slot, i):
            r, n = locate(g * DIL_GROUP + i)
            s = s_ref[slot, i] + bias_ref[jnp.minimum(n, 1)]
            m = jnp.max(s, axis=1, keepdims=True)
            p_ref[slot, i] = jnp.exp2(s - m).astype(jnp.bfloat16)
            scatter(1, max_ref, r, n, jnp.where(low_half, m[:BLOCK], m[BLOCK:]))

        def values(g, slot, i):
            r, n = locate(g * DIL_GROUP + i)
            v = jnp.concatenate([v_ref[0, 0, r, key_rows(n), :], ones], axis=1)
            both = jnp.dot(p_ref[slot, i], v, preferred_element_type=f32)
            num, den = both[:, :LANES], both[:, LANES:]
            num = jnp.where(low_half, num[:BLOCK], num[BLOCK:])
            den = jnp.where(low_half, den[:BLOCK], den[BLOCK:])
            if dil != 1:
                scatter(0, num_ref, r, n, num)
                scatter(2, den_ref, r, n, den)
                return
            rows = out_rows(r, n)
            others = [i for i in range(len(DILATIONS)) if i != idx]
            ms = [max_ref[idx, rows, :]] + [max_ref[i, rows, :] for i in others]
            m_all = functools.reduce(jnp.maximum, ms)
            w = jnp.exp2(ms[0] - m_all)
            num, den = w * num, w * den
            for m, i in zip(ms[1:], others):
                w = jnp.exp2(m - m_all)
                num = num + w * num_ref[i, rows, :]
                den = den + w * den_ref[i, rows, :]
            o_ref[rows, :] = (num / den * g_ref[0, rows, :].astype(f32)).astype(o_ref.dtype)

        return scores, softmax, values, finish

    assert DILATIONS[0] == 1
    stages = [branch(2, DILATIONS[2], q16_ref, k16_ref, v16_ref),
              branch(1, DILATIONS[1], q4_ref, k4_ref, v4_ref),
              branch(0, DILATIONS[0], q1_ref, k1_ref, v1_ref)]
    last = n_groups - 1

    def trip(*work):
        for stage, g, slot in work:
            for i in range(DIL_GROUP):
                stage(g, slot, i)

    for b, (scores, softmax, values, _) in enumerate(stages):
        if b == 0:
            trip((scores, 0, 0))
            trip((scores, 1, 1), (softmax, 0, 0))
        else:
            trip((stages[b - 1][2], last, 1), (scores, 1, 1), (softmax, 0, 0))
            stages[b - 1][3]()

        def two_trips(t, carry, scores=scores, softmax=softmax, values=values):
            trip((values, 2 * t, 0), (scores, 2 * t + 2, 0), (softmax, 2 * t + 1, 1))
            trip((values, 2 * t + 1, 1), (scores, 2 * t + 3, 1), (softmax, 2 * t + 2, 0))
            return carry

        for t in range((n_groups - 2) // 2):
            two_trips(t, 0)
        if b + 1 < len(stages):
            trip((values, last - 1, 0), (stages[b + 1][0], 0, 0), (softmax, last, 1))
        else:
            trip((values, last - 1, 0), (softmax, last, 1))
    trip((stages[-1][2], last, 1))


def _out_kernel(x_ref, ya_ref, yb_ref, wa_ref, wb_ref, g_ref, b_ref, o_ref):
    f32 = jnp.float32
    chunks = [pl.ds(r, OUT_CHUNK) for r in range(0, x_ref.shape[0], OUT_CHUNK)]

    def project(rows):
        return (jnp.dot(ya_ref[rows, :], wa_ref[...], preferred_element_type=f32)
                + jnp.dot(yb_ref[rows, :], wb_ref[...], preferred_element_type=f32))

    def normalise(rows, mixed):
        h = DEEPNORM_ALPHA * x_ref[rows, :] + mixed
        mu = jnp.mean(h, axis=-1, keepdims=True)
        c = h - mu
        var = jnp.mean(c * c, axis=-1, keepdims=True)
        o_ref[rows, :] = c * lax.rsqrt(var + LN_EPS) * g_ref[...] + b_ref[...]

    mixed = project(chunks[0])
    for i, rows in enumerate(chunks):
        ahead = project(chunks[i + 1]) if i + 1 < len(chunks) else None
        normalise(rows, mixed)
        mixed = ahead


def _rope_tables(seq, rot_dim, period, rot_offset, pass_rest):
    half = rot_dim // 2
    f32 = np.float32
    inv_freq = ROPE_THETA ** (-np.arange(0, rot_dim, 2, dtype=np.float64) / rot_dim)
    ang = np.arange(seq, dtype=np.float64)[:, None] * inv_freq[None, :]
    cos, sin = np.cos(ang).astype(f32), np.sin(ang).astype(f32)
    zeros = np.zeros((seq, half), f32)
    rest = period - rot_offset - rot_dim
    fill = np.ones if pass_rest else np.zeros
    group_cos = np.concatenate([np.ones((seq, rot_offset), f32), cos, cos, fill((seq, rest), f32)], axis=1)
    group_fwd = np.concatenate([np.zeros((seq, rot_offset), f32), zeros, sin, np.zeros((seq, rest), f32)], axis=1)
    group_bwd = np.concatenate([np.zeros((seq, rot_offset), f32), -sin, zeros, np.zeros((seq, rest), f32)], axis=1)
    reps = LANES // period
    return tuple(jnp.asarray(np.tile(t, (1, reps))) for t in (group_cos, group_fwd, group_bwd))


def _params(*semantics, flags=None):
    return pltpu.CompilerParams(dimension_semantics=semantics, vmem_limit_bytes=VMEM_LIMIT_BYTES, flags=flags)


def kernel(x, w_in, q_norm_g, kv_norm_g, w_uq, w_ukv, w_out, ln_g, ln_b):
    f32, bf16 = jnp.float32, jnp.bfloat16
    batch, seq, _ = x.shape
    rows = batch * seq
    x2 = x.reshape(rows, D_MODEL)

    rope_end = _LATENT_WIDTH + MLA_ROPE_DIM
    w_lat = w_in[:, :_LATENT_WIDTH].astype(bf16)
    w_kpe = jnp.pad(w_in[:, _LATENT_WIDTH:rope_end].astype(bf16),
                    ((0, 0), (MLA_NOPE_DIM, LANES - MLA_NOPE_DIM - MLA_ROPE_DIM)))
    w_rest = w_in[:, rope_end:].astype(bf16)
    assert w_rest.shape[1] == _REST_WIDTH
    dk = MLA_NOPE_DIM + MLA_ROPE_DIM
    wuq = jnp.pad(w_uq.reshape(Q_LORA_RANK, MLA_HEADS, dk), ((0, 0), (0, 0), (0, LANES - dk)))
    wuq = wuq.reshape(Q_LORA_RANK, MLA_HEADS * LANES).astype(bf16)
    wukv = w_ukv.reshape(KV_LORA_RANK, MLA_HEADS, MLA_NOPE_DIM + MLA_V_DIM)
    wuk = jnp.pad(wukv[:, :, :MLA_NOPE_DIM], ((0, 0), (0, 0), (0, LANES - MLA_NOPE_DIM)))
    wuk = wuk.reshape(KV_LORA_RANK, MLA_HEADS * LANES).astype(bf16)
    wuvt = wukv[:, :, MLA_NOPE_DIM:].reshape(KV_LORA_RANK, MLA_WIDTH).T.astype(bf16)
    wa = w_out[:MLA_WIDTH].astype(bf16)
    wb = w_out[MLA_WIDTH:].astype(bf16)

    mla_tabs = _rope_tables(seq, MLA_ROPE_DIM, LANES, MLA_NOPE_DIM, pass_rest=False)
    dil_tabs = _rope_tables(seq, DIL_ROT_DIM, DIL_HEAD_DIM, 0, pass_rest=True)

    tm = PROJ_ROWS
    seq_tiles = seq // tm
    full = lambda shape: pl.BlockSpec(shape, lambda i: (0,) * len(shape))
    tab = pl.BlockSpec((tm, LANES), lambda i: (i % seq_tiles, 0))
    slab = lambda n: pl.BlockSpec((n, tm, LANES), lambda i: (0, i, 0))
    slab_shape = lambda n: jax.ShapeDtypeStruct((n, rows, LANES), bf16)
    tk = MLA_KEY_TILE
    vt_spec = pl.BlockSpec((tm // tk, MLA_WIDTH, tk), lambda i: (i, 0, 0))
    vt_shape = jax.ShapeDtypeStruct((rows // tk, MLA_WIDTH, tk), bf16)
    res_specs, res_shapes = [], []
    for dil in DILATIONS:
        spec = pl.BlockSpec((HEAD_PAIRS, 1, dil, tm // dil, LANES),
                            lambda i: (0, i // seq_tiles, 0, i % seq_tiles, 0))
        shape = jax.ShapeDtypeStruct((HEAD_PAIRS, batch, dil, seq // dil, LANES), bf16)
        res_specs += [spec] * 3
        res_shapes += [shape] * 3
    qm, km, vt, ga, gb, *dil_in = pl.pallas_call(
        _proj_kernel,
        grid=(rows // tm,),
        in_specs=[pl.BlockSpec((tm, D_MODEL), lambda i: (i, 0)),
                  full(w_lat.shape), full(w_kpe.shape), full(w_rest.shape),
                  full(wuq.shape), full(wuk.shape), full(wuvt.shape),
                  full((1, Q_LORA_RANK)), full((1, KV_LORA_RANK)),
                  tab, tab, tab, tab, tab, tab],
        out_specs=[slab(MLA_HEADS), slab(MLA_HEADS), vt_spec, slab(HEAD_PAIRS), slab(HEAD_PAIRS)] + res_specs,
        out_shape=[slab_shape(MLA_HEADS), slab_shape(MLA_HEADS), vt_shape, slab_shape(HEAD_PAIRS),
                   slab_shape(HEAD_PAIRS)] + res_shapes,
        scratch_shapes=[pltpu.VMEM((3 * HEAD_PAIRS, tm, LANES), f32),
                        pltpu.VMEM((3 * HEAD_PAIRS, 4, tm // 4, LANES), f32)],
        compiler_params=_params("parallel"),
        name="proj",
    )(x2, w_lat, w_kpe, w_rest, wuq, wuk, wuvt, q_norm_g.reshape(1, -1), kv_norm_g.reshape(1, -1),
      *mla_tabs, *dil_tabs)

    t = MLA_TILE
    ya = pl.pallas_call(
        _mla_kernel,
        grid=(batch, HEAD_PAIRS),
        in_specs=[pl.BlockSpec((2, seq, LANES), lambda b, p: (p, b, 0)),
                  pl.BlockSpec((2, seq, LANES), lambda b, p: (p, b, 0)),
                  pl.BlockSpec((seq // tk, LANES, tk), lambda b, p: (b, p, 0)),
                  pl.BlockSpec((1, seq, LANES), lambda b, p: (p, b, 0))],
        out_specs=pl.BlockSpec((seq, LANES), lambda b, p: (b, p)),
        out_shape=jax.ShapeDtypeStruct((rows, MLA_WIDTH), bf16),
        scratch_shapes=[pltpu.VMEM((2, 2, MLA_V_DIM + MLA_ONES_ROWS, t), f32), pltpu.VMEM((2, 2, tk, t), f32)],
        compiler_params=_params("parallel", "parallel"),
        name="mla",
    )(qm, km, vt, ga)

    dil_specs = [pl.BlockSpec((1, 1, dil, seq // dil, LANES), lambda b, p: (p, b, 0, 0, 0))
                 for dil in DILATIONS for _ in range(3)]
    yb = pl.pallas_call(
        _dilated_kernel,
        grid=(batch, HEAD_PAIRS),
        in_specs=dil_specs + [pl.BlockSpec((1, seq, LANES), lambda b, p: (p, b, 0))],
        out_specs=pl.BlockSpec((seq, LANES), lambda b, p: (b, p)),
        out_shape=jax.ShapeDtypeStruct((rows, DIL_WIDTH), bf16),
        scratch_shapes=[pltpu.VMEM((3, seq, LANES), f32)] * 3 + [
            pltpu.VMEM((3, SCATTER_STRIDE, seq // SCATTER_STRIDE, LANES), f32),
            pltpu.VMEM((2, DIL_GROUP, 2 * BLOCK, 2 * BLOCK), f32),
            pltpu.VMEM((2, DIL_GROUP, 2 * BLOCK, 2 * BLOCK), bf16),
            pltpu.VMEM((2, 2 * BLOCK, 2 * BLOCK), f32)],
        compiler_params=_params("parallel", "parallel"),
        name="dilated",
    )(*dil_in, gb)

    to = OUT_ROWS
    const = lambda shape: pl.BlockSpec(shape, lambda i: (0,) * len(shape))
    out = pl.pallas_call(
        _out_kernel,
        grid=(rows // to,),
        in_specs=[pl.BlockSpec((to, D_MODEL), lambda i: (i, 0)),
                  pl.BlockSpec((to, MLA_WIDTH), lambda i: (i, 0)),
                  pl.BlockSpec((to, DIL_WIDTH), lambda i: (i, 0)),
                  const((MLA_WIDTH, D_MODEL)), const((DIL_WIDTH, D_MODEL)),
                  const((1, D_MODEL)), const((1, D_MODEL))],
        out_specs=pl.BlockSpec((to, D_MODEL), lambda i: (i, 0)),
        out_shape=jax.ShapeDtypeStruct((rows, D_MODEL), f32),
        compiler_params=_params("parallel"),
        name="out",
    )(x2, ya, yb, wa, wb, ln_g.reshape(1, -1), ln_b.reshape(1, -1))
    return out.reshape(batch, seq, D_MODEL)
```

```python
import functools

import jax
import jax.numpy as jnp
import numpy as np
from jax import lax
from jax.experimental import pallas as pl
from jax.experimental.pallas import tpu as pltpu

D_MODEL = 1024
ROPE_THETA = 500000.0
BLOCK = 128
NEG = -1e30
RMS_EPS = 1e-6
LN_EPS = 1e-5

MLA_HEADS = 8
MLA_NOPE_DIM = 64
MLA_ROPE_DIM = 32
MLA_V_DIM = 64
Q_LORA_RANK = 384
KV_LORA_RANK = 256
MLA_WIDTH = MLA_HEADS * MLA_V_DIM

DIL_HEADS = 8
DIL_HEAD_DIM = 64
DIL_ROT_DIM = DIL_HEAD_DIM // 4
DIL_WIDTH = DIL_HEADS * DIL_HEAD_DIM
DILATIONS = (1, 4, 16)

DEPTH = 1
DEEPNORM_ALPHA = (2.0 * DEPTH) ** 0.25
LOG2_E = 1.4426950408889634

LANES = 128
HEAD_PAIRS = MLA_HEADS // 2
VMEM_LIMIT_BYTES = 56 * 1024 * 1024

_LATENT_WIDTH = Q_LORA_RANK + KV_LORA_RANK
_OFF_GA = 0
_OFF_QB = _OFF_GA + MLA_WIDTH
_OFF_KB = _OFF_QB + DIL_WIDTH
_OFF_VB = _OFF_KB + DIL_WIDTH
_OFF_GB = _OFF_VB + DIL_WIDTH
_REST_WIDTH = _OFF_GB + DIL_WIDTH

PROJ_ROWS = 512
MLA_KEY_TILE = 256
MLA_TILE = 1024
MLA_UNROLL_TRIPS = 2
MLA_ONES_ROWS = 16
OUT_ROWS = 2048
OUT_CHUNK = 256
DIL_GROUP = 4
SCATTER_STRIDE = 4


def _rope_lanes(x, cos, sin):
    return x * cos + pltpu.roll(x, LANES // 2, 1) * sin


def _proj_kernel(x_ref, wlat_ref, wkpe_ref, w_ref, wuq_ref, wuk_ref, wuvt_ref, qg_ref, kvg_ref,
                 mcos_ref, msin_ref, dcos_ref, dsin_ref,
                 qm_ref, km_ref, vt_ref, ga_ref, gb_ref, *dil_and_scratch):
    f32 = jnp.float32
    bf16 = jnp.bfloat16
    xb = x_ref[...].astype(bf16)
    dil_refs = [dil_and_scratch[3 * i:3 * i + 3] for i in range(len(DILATIONS))]
    stage_ref, stage4_ref = dil_and_scratch[-2:]
    tm = x_ref.shape[0]
    assert DILATIONS == (1, 4, 16)

    def emit_residues(which, p, val):
        slab = which * HEAD_PAIRS + p
        q1, q4, q16 = (refs[which] for refs in dil_refs)
        q1[p, 0, 0] = val.astype(bf16)
        stage_ref[slab] = val
        for a in range(4):
            rows4 = stage_ref[slab, pl.ds(a, tm // 4, stride=4), :]
            q4[p, 0, a] = rows4.astype(bf16)
            stage4_ref[slab, a] = rows4
        for a in range(4):
            for b in range(4):
                rows16 = stage4_ref[slab, a, pl.ds(b, tm // 16, stride=4), :]
                q16[p, 0, a + 4 * b] = rows16.astype(bf16)

    def seg(lo, width):
        return jnp.dot(xb, w_ref[:, lo:lo + width], preferred_element_type=f32)

    def rms(t, g):
        return t * lax.rsqrt(jnp.mean(t * t, axis=-1, keepdims=True) + RMS_EPS) * g

    mcos, msin = mcos_ref[...], msin_ref[...]
    dcos, dsin = dcos_ref[...], dsin_ref[...]
    mla_scale = (MLA_NOPE_DIM + MLA_ROPE_DIM) ** -0.5 * LOG2_E
    dil_scale = DIL_HEAD_DIM ** -0.5 * LOG2_E

    unit = 2 * LANES
    latent = {}

    def slabs(t):
        return [t[:, j * LANES:(j + 1) * LANES] for j in range(t.shape[1] // LANES)]

    def dilated(which, lo, first_pair):
        def epilogue(t):
            for j, val in enumerate(slabs(t)):
                if which < 2:
                    val = _rope_lanes(val, dcos, dsin)
                if which == 0:
                    val = val * dil_scale
                emit_residues(which, first_pair + j, val)
        return (lambda: seg(lo, unit)), epilogue

    def gate(out_ref, lo, first_pair):
        def epilogue(t):
            for j, val in enumerate(slabs(jax.nn.silu(t))):
                out_ref[first_pair + j] = val.astype(bf16)
        return (lambda: seg(lo, unit)), epilogue

    def mla_q(first_head):
        def epilogue(t):
            for j, val in enumerate(slabs(t)):
                val = _rope_lanes(val, mcos, msin) * mla_scale
                qm_ref[first_head + j] = val.astype(bf16)
        lo = first_head * LANES
        return (lambda: jnp.dot(latent["cq"], wuq_ref[:, lo:lo + unit], preferred_element_type=f32)), epilogue

    def mla_k(first_head):
        def epilogue(t):
            for j, val in enumerate(slabs(t)):
                km_ref[first_head + j] = (val + latent["kpe"]).astype(bf16)
        lo = first_head * LANES
        return (lambda: jnp.dot(latent["ckv"], wuk_ref[:, lo:lo + unit], preferred_element_type=f32)), epilogue

    def store_vt(t):
        for c in range(vt_ref.shape[0]):
            vt_ref[c] = t[:, c * MLA_KEY_TILE:(c + 1) * MLA_KEY_TILE].astype(bf16)

    split = unit
    units = [
        (lambda: jnp.dot(xb, wlat_ref[:, :split], preferred_element_type=f32),
         lambda t: latent.update(cq_head=t)),
        (lambda: jnp.dot(xb, wlat_ref[:, split:Q_LORA_RANK], preferred_element_type=f32),
         lambda t: latent.update(cq=rms(jnp.concatenate([latent["cq_head"], t], axis=1), qg_ref[...]).astype(bf16))),
        (lambda: jnp.dot(xb, wlat_ref[:, Q_LORA_RANK:], preferred_element_type=f32),
         lambda t: latent.update(ckv=rms(t, kvg_ref[...]).astype(bf16))),
        dilated(0, _OFF_QB, 0), mla_q(0), dilated(0, _OFF_QB + unit, 2), mla_q(2),
        dilated(1, _OFF_KB, 0), mla_q(4), dilated(1, _OFF_KB + unit, 2), mla_q(6),
        (lambda: jnp.dot(xb, wkpe_ref[...], preferred_element_type=f32),
         lambda t: latent.update(kpe=_rope_lanes(t, mcos, msin))),
        gate(ga_ref, _OFF_GA, 0), mla_k(0), dilated(2, _OFF_VB, 0), mla_k(2),
        gate(ga_ref, _OFF_GA + unit, 2), mla_k(4), dilated(2, _OFF_VB + unit, 2), mla_k(6),
        gate(gb_ref, _OFF_GB, 0), gate(gb_ref, _OFF_GB + unit, 2),
        (lambda: lax.dot_general(wuvt_ref[...], latent["ckv"], (((1,), (1,)), ((), ())),
                                 preferred_element_type=f32), store_vt),
    ]
    pending = None
    for matmul, epilogue in units:
        result = matmul()
        if pending is not None:
            pending[0](pending[1])
        pending = (epilogue, result)
    pending[0](pending[1])


def _mla_kernel(q_ref, k_ref, vt_ref, g_ref, o_ref, accs_ref, s_ref):
    f32 = jnp.float32
    bf16 = jnp.bfloat16
    tq, tk = MLA_TILE, MLA_KEY_TILE
    n_chunks = tq // tk
    assert n_chunks % 2 == 0
    nq = q_ref.shape[1] // tq
    units = [(h, c) for c in range(n_chunks) for h in range(2)]
    ones = jnp.ones((MLA_ONES_ROWS, tk), bf16)
    key = lax.broadcasted_iota(jnp.int32, (tk, tk), 0)
    qry = lax.broadcasted_iota(jnp.int32, (tk, tk), 1)
    lower = key <= qry

    def lanes(c):
        return slice(c * tk, (c + 1) * tk)

    def scores(qi, j, slot, h, c):
        q = q_ref[h, pl.ds(qi * tq + c * tk, tk), :]
        k = k_ref[h, pl.ds(j * tk if isinstance(j, int) else pl.multiple_of(j * tk, tk), tk), :]
        s = lax.dot_general(k, q, (((1,), (1,)), ((), ())), preferred_element_type=f32)
        s_ref[slot, h, :, lanes(c)] = s
        return jnp.max(s, axis=0, keepdims=True)

    def absorb(acc_ref, j, slot, h, c, m_tile, m_old, masked=False):
        s = s_ref[slot, h, :, lanes(c)]
        if masked:
            s = jnp.where(lower, s, NEG)
            m_tile = jnp.max(s, axis=0, keepdims=True)
        m_new = jnp.maximum(m_old, m_tile)
        alpha = jnp.exp2(m_old - m_new)
        p = jnp.exp2(s - m_new).astype(bf16)
        vt = jnp.concatenate([vt_ref[j, h * MLA_V_DIM:(h + 1) * MLA_V_DIM, :], ones], axis=0)
        acc_ref[h, :, lanes(c)] = alpha * acc_ref[h, :, lanes(c)] + jnp.dot(vt, p, preferred_element_type=f32)
        return m_new

    def per_query_tile(qi, acc_ref):
        rows = pl.ds(qi * tq, tq)
        acc_ref[...] = jnp.zeros_like(acc_ref)

        def step(j, slot, tile_max, m_run):
            next_max, m_new = {}, {}
            for u in units:
                next_max[u] = scores(qi, j + 1, 1 - slot, *u)
                m_new[u] = absorb(acc_ref, j, slot, *u, tile_max[u], m_run[u])
            return next_max, m_new

        def body(jj, carry):
            tile_max, m_run = carry
            for i in range(n_chunks):
                tile_max, m_run = step(n_chunks * jj + i, i % 2, tile_max, m_run)
            return tile_max, m_run

        start = {u: jnp.full((1, tk), NEG, f32) for u in units}
        first = {u: scores(qi, 0, 0, *u) for u in units}
        tile_max, m_run = lax.fori_loop(0, qi, body, (first, start), unroll=qi <= MLA_UNROLL_TRIPS)

        base = qi * n_chunks
        for d in range(n_chunks):
            next_max = {}
            for h, c in units:
                if c > d:
                    next_max[h, c] = scores(qi, base + d + 1, (d + 1) % 2, h, c)
                if c >= d:
                    m_run[h, c] = absorb(acc_ref, base + d, d % 2, h, c, tile_max[h, c], m_run[h, c],
                                         masked=(c == d))
            tile_max = next_max

        yt = jnp.concatenate([acc_ref[h, :MLA_V_DIM, :] / acc_ref[h, MLA_V_DIM:MLA_V_DIM + 1, :] for h in range(2)],
                             axis=0)
        o_ref[rows, :] = (yt.T * g_ref[0, rows, :].astype(f32)).astype(o_ref.dtype)

    for qi in range(nq):
        per_query_tile(qi, accs_ref.at[qi % 2])


def _dilated_kernel(q1_ref, k1_ref, v1_ref, q4_ref, k4_ref, v4_ref, q16_ref, k16_ref, v16_ref,
                    g_ref, o_ref, num_ref, max_ref, den_ref, hop_ref, s_ref, p_ref, bias_ref):
    f32 = jnp.float32
    seq = num_ref.shape[1]

    lane = lax.broadcasted_iota(jnp.int32, (BLOCK, LANES), 1)
    low_half = lane < DIL_HEAD_DIM
    first_head = lane % (LANES // 2) < LANES // 4
    ones = jnp.ones((2 * BLOCK, LANES), jnp.bfloat16)
    qi = lax.broadcasted_iota(jnp.int32, (2 * BLOCK, 2 * BLOCK), 0) % BLOCK
    kj = lax.broadcasted_iota(jnp.int32, (2 * BLOCK, 2 * BLOCK), 1)
    dist_first = qi - kj
    dist_later = dist_first + BLOCK
    for kind, dist in enumerate((dist_first, dist_later)):
        bias_ref[kind] = jnp.where((dist >= 0) & (dist <= BLOCK), 0.0, NEG).astype(f32)

    n_groups = seq // BLOCK // DIL_GROUP
    assert n_groups % 2 == 0 and n_groups >= 4

    def branch(idx, dil, q_ref, k_ref, v_ref):
        per_residue = q_ref.shape[3] // BLOCK

        def locate(b):
            return b // per_residue, b % per_residue

        def key_rows(n):
            return pl.ds(pl.multiple_of(jnp.maximum(n - 1, 0) * BLOCK, BLOCK), 2 * BLOCK)

        def out_rows(r, n):
            if dil == 1:
                return pl.ds(pl.multiple_of(n * BLOCK, BLOCK), BLOCK)
            return pl.ds(r + n * (BLOCK * dil), BLOCK, stride=dil)

        two_hops = dil > SCATTER_STRIDE
        inner = dil // SCATTER_STRIDE

        def scatter(kind, stat_ref, r, n, val):
            if two_hops:
                rows = pl.ds(r // SCATTER_STRIDE + n * (BLOCK * inner), BLOCK, stride=inner)
                hop_ref[kind, r % SCATTER_STRIDE, rows, :] = val
            else:
                stat_ref[idx, out_rows(r, n), :] = val

        def finish():
            if not two_hops:
                return
            for kind, stat_ref in enumerate((num_ref, max_ref, den_ref)):
                for a in range(SCATTER_STRIDE):
                    for c in range(hop_ref.shape[2] // BLOCK):
                        rows = pl.ds(a + c * (BLOCK * SCATTER_STRIDE), BLOCK, stride=SCATTER_STRIDE)
                        stat_ref[idx, rows, :] = hop_ref[kind, a, c * BLOCK:(c + 1) * BLOCK, :]

        def scores(g, slot, i):
            r, n = locate(g * DIL_GROUP + i)
            q = q_ref[0, 0, r, pl.ds(pl.multiple_of(n * BLOCK, BLOCK), BLOCK), :]
            k = k_ref[0, 0, r, key_rows(n), :]
            zero = jnp.zeros_like(q)
            q2 = jnp.concatenate([jnp.where(first_head, q, zero), jnp.where(first_head, zero, q)], axis=0)
            s_ref[slot, i] = lax.dot_general(q2, k, (((1,), (1,)), ((), ())), preferred_element_type=f32)

        def softmax(g, slot, i):
            r, n = locate(g * DIL_GROUP + i)
            s = s_ref[slot, i] + bias_ref[jnp.minimum(n, 1)]
            m = jnp.max(s, axis=1, keepdims=True)
            p_ref[slot, i] = jnp.exp2(s - m).astype(jnp.bfloat16)
            scatter(1, max_ref, r, n, jnp.where(low_half, m[:BLOCK], m[BLOCK:]))

        def values(g, slot, i):
            r, n = locate(g * DIL_GROUP + i)
            v = jnp.concatenate([v_ref[0, 0, r, key_rows(n), :], ones], axis=1)
            both = jnp.dot(p_ref[slot, i], v, preferred_element_type=f32)
            num, den = both[:, :LANES], both[:, LANES:]
            num = jnp.where(low_half, num[:BLOCK], num[BLOCK:])
            den = jnp.where(low_half, den[:BLOCK], den[BLOCK:])
            if dil != 1:
                scatter(0, num_ref, r, n, num)
                scatter(2, den_ref, r, n, den)
                return
            rows = out_rows(r, n)
            others = [i for i in range(len(DILATIONS)) if i != idx]
            ms = [max_ref[idx, rows, :]] + [max_ref[i, rows, :] for i in others]
            m_all = functools.reduce(jnp.maximum, ms)
            w = jnp.exp2(ms[0] - m_all)
            num, den = w * num, w * den
            for m, i in zip(ms[1:], others):
                w = jnp.exp2(m - m_all)
                num = num + w * num_ref[i, rows, :]
                den = den + w * den_ref[i, rows, :]
            o_ref[rows, :] = (num / den * g_ref[0, rows, :].astype(f32)).astype(o_ref.dtype)

        return scores, softmax, values, finish

    assert DILATIONS[0] == 1
    stages = [branch(2, DILATIONS[2], q16_ref, k16_ref, v16_ref),
              branch(1, DILATIONS[1], q4_ref, k4_ref, v4_ref),
              branch(0, DILATIONS[0], q1_ref, k1_ref, v1_ref)]
    last = n_groups - 1

    def trip(*work):
        for stage, g, slot in work:
            for i in range(DIL_GROUP):
                stage(g, slot, i)

    for b, (scores, softmax, values, _) in enumerate(stages):
        if b == 0:
            trip((scores, 0, 0))
            trip((scores, 1, 1), (softmax, 0, 0))
        else:
            trip((stages[b - 1][2], last, 1), (scores, 1, 1), (softmax, 0, 0))
            stages[b - 1][3]()

        def two_trips(t, carry, scores=scores, softmax=softmax, values=values):
            trip((values, 2 * t, 0), (scores, 2 * t + 2, 0), (softmax, 2 * t + 1, 1))
            trip((values, 2 * t + 1, 1), (scores, 2 * t + 3, 1), (softmax, 2 * t + 2, 0))
            return carry

        for t in range((n_groups - 2) // 2):
            two_trips(t, 0)
        if b + 1 < len(stages):
            trip((values, last - 1, 0), (stages[b + 1][0], 0, 0), (softmax, last, 1))
        else:
            trip((values, last - 1, 0), (softmax, last, 1))
    trip((stages[-1][2], last, 1))


def _out_kernel(x_ref, ya_ref, yb_ref, wa_ref, wb_ref, g_ref, b_ref, o_ref):
    f32 = jnp.float32
    chunks = [pl.ds(r, OUT_CHUNK) for r in range(0, x_ref.shape[0], OUT_CHUNK)]

    def project(rows):
        return (jnp.dot(ya_ref[rows, :], wa_ref[...], preferred_element_type=f32)
                + jnp.dot(yb_ref[rows, :], wb_ref[...], preferred_element_type=f32))

    def normalise(rows, mixed):
        h = DEEPNORM_ALPHA * x_ref[rows, :] + mixed
        mu = jnp.mean(h, axis=-1, keepdims=True)
        c = h - mu
        var = jnp.mean(c * c, axis=-1, keepdims=True)
        o_ref[rows, :] = c * lax.rsqrt(var + LN_EPS) * g_ref[...] + b_ref[...]

    mixed = project(chunks[0])
    for i, rows in enumerate(chunks):
        ahead = project(chunks[i + 1]) if i + 1 < len(chunks) else None
        normalise(rows, mixed)
        mixed = ahead


_HALF_ROPE = MLA_ROPE_DIM // 2
_NOPE_LOW = LANES // 2 - _HALF_ROPE
_MLA_Q_LANES = ((MLA_NOPE_DIM, MLA_NOPE_DIM + _HALF_ROPE), (0, _NOPE_LOW),
                (MLA_NOPE_DIM + _HALF_ROPE, MLA_NOPE_DIM + MLA_ROPE_DIM), (_NOPE_LOW, MLA_NOPE_DIM),
                ("zero", LANES // 2 - _HALF_ROPE - (MLA_NOPE_DIM - _NOPE_LOW)))
_MLA_K_LANES = (("zero", _HALF_ROPE), (0, _NOPE_LOW), ("zero", _HALF_ROPE), (_NOPE_LOW, MLA_NOPE_DIM),
                ("zero", LANES // 2 - _HALF_ROPE - (MLA_NOPE_DIM - _NOPE_LOW)))
_MLA_KPE_LANES = ((0, _HALF_ROPE), ("zero", LANES // 2 - _HALF_ROPE), (_HALF_ROPE, MLA_ROPE_DIM),
                  ("zero", LANES // 2 - _HALF_ROPE))
_DIL_HALF_ROT = DIL_ROT_DIM // 2
_DIL_QUARTER = LANES // 4
_DIL_PLAIN_LOW = _DIL_QUARTER - _DIL_HALF_ROT


def _dil_pair_lanes():
    runs = []
    for upper in (False, True):
        for head in range(2):
            base = head * DIL_HEAD_DIM
            rot = base + (_DIL_HALF_ROT if upper else 0)
            plain = base + DIL_ROT_DIM + (_DIL_PLAIN_LOW if upper else 0)
            runs += [(rot, rot + _DIL_HALF_ROT), (plain, plain + _DIL_PLAIN_LOW)]
    return tuple(runs)


def _place_lanes(w, runs):
    parts = [jnp.zeros((w.shape[0], run[1]), w.dtype) if run[0] == "zero" else w[:, run[0]:run[1]] for run in runs]
    return jnp.concatenate(parts, axis=1)


def _rope_tables(seq, rot_dim, rot_lanes):
    half = rot_dim // 2
    inv_freq = ROPE_THETA ** (-np.arange(0, rot_dim, 2, dtype=np.float64) / rot_dim)
    ang = np.arange(seq, dtype=np.float64)[:, None] * inv_freq[None, :]
    cos = np.ones((seq, LANES), np.float32)
    sin = np.zeros((seq, LANES), np.float32)
    for lane in rot_lanes:
        for first, sign in ((lane, -1.0), (lane + LANES // 2, 1.0)):
            cos[:, first:first + half] = np.cos(ang)
            sin[:, first:first + half] = sign * np.sin(ang)
    return jnp.asarray(cos), jnp.asarray(sin)


def _params(*semantics, flags=None):
    return pltpu.CompilerParams(dimension_semantics=semantics, vmem_limit_bytes=VMEM_LIMIT_BYTES, flags=flags)


def kernel(x, w_in, q_norm_g, kv_norm_g, w_uq, w_ukv, w_out, ln_g, ln_b):
    f32, bf16 = jnp.float32, jnp.bfloat16
    batch, seq, _ = x.shape
    rows = batch * seq
    x2 = x.reshape(rows, D_MODEL)

    rope_end = _LATENT_WIDTH + MLA_ROPE_DIM
    w_lat = w_in[:, :_LATENT_WIDTH].astype(bf16)
    w_kpe = _place_lanes(w_in[:, _LATENT_WIDTH:rope_end].astype(bf16), _MLA_KPE_LANES)
    w_rest = w_in[:, rope_end:].astype(bf16)
    assert w_rest.shape[1] == _REST_WIDTH
    pair_lanes = _dil_pair_lanes()
    pieces = [w_rest[:, :_OFF_QB]]
    for lo in range(_OFF_QB, _OFF_VB, LANES):
        pieces.append(_place_lanes(w_rest[:, lo:lo + LANES], pair_lanes))
    w_rest = jnp.concatenate(pieces + [w_rest[:, _OFF_VB:]], axis=1)
    dk = MLA_NOPE_DIM + MLA_ROPE_DIM
    wuq = jnp.concatenate([_place_lanes(w_uq[:, h * dk:(h + 1) * dk], _MLA_Q_LANES) for h in range(MLA_HEADS)],
                          axis=1).astype(bf16)
    wukv = w_ukv.reshape(KV_LORA_RANK, MLA_HEADS, MLA_NOPE_DIM + MLA_V_DIM)
    wuk = jnp.concatenate([_place_lanes(wukv[:, h, :MLA_NOPE_DIM], _MLA_K_LANES) for h in range(MLA_HEADS)],
                          axis=1).astype(bf16)
    wuvt = wukv[:, :, MLA_NOPE_DIM:].reshape(KV_LORA_RANK, MLA_WIDTH).T.astype(bf16)
    wa = w_out[:MLA_WIDTH].astype(bf16)
    wb = w_out[MLA_WIDTH:].astype(bf16)

    mla_tabs = _rope_tables(seq, MLA_ROPE_DIM, (0,))
    dil_tabs = _rope_tables(seq, DIL_ROT_DIM, (0, _DIL_QUARTER))

    tm = PROJ_ROWS
    seq_tiles = seq // tm
    full = lambda shape: pl.BlockSpec(shape, lambda i: (0,) * len(shape))
    tab = pl.BlockSpec((tm, LANES), lambda i: (i % seq_tiles, 0))
    slab = lambda n: pl.BlockSpec((n, tm, LANES), lambda i: (0, i, 0))
    slab_shape = lambda n: jax.ShapeDtypeStruct((n, rows, LANES), bf16)
    tk = MLA_KEY_TILE
    vt_spec = pl.BlockSpec((tm // tk, MLA_WIDTH, tk), lambda i: (i, 0, 0))
    vt_shape = jax.ShapeDtypeStruct((rows // tk, MLA_WIDTH, tk), bf16)
    res_specs, res_shapes = [], []
    for dil in DILATIONS:
        spec = pl.BlockSpec((HEAD_PAIRS, 1, dil, tm // dil, LANES),
                            lambda i: (0, i // seq_tiles, 0, i % seq_tiles, 0))
        shape = jax.ShapeDtypeStruct((HEAD_PAIRS, batch, dil, seq // dil, LANES), bf16)
        res_specs += [spec] * 3
        res_shapes += [shape] * 3
    qm, km, vt, ga, gb, *dil_in = pl.pallas_call(
        _proj_kernel,
        grid=(rows // tm,),
        in_specs=[pl.BlockSpec((tm, D_MODEL), lambda i: (i, 0)),
                  full(w_lat.shape), full(w_kpe.shape), full(w_rest.shape),
                  full(wuq.shape), full(wuk.shape), full(wuvt.shape),
                  full((1, Q_LORA_RANK)), full((1, KV_LORA_RANK)),
                  tab, tab, tab, tab],
        out_specs=[slab(MLA_HEADS), slab(MLA_HEADS), vt_spec, slab(HEAD_PAIRS), slab(HEAD_PAIRS)] + res_specs,
        out_shape=[slab_shape(MLA_HEADS), slab_shape(MLA_HEADS), vt_shape, slab_shape(HEAD_PAIRS),
                   slab_shape(HEAD_PAIRS)] + res_shapes,
        scratch_shapes=[pltpu.VMEM((3 * HEAD_PAIRS, tm, LANES), f32),
                        pltpu.VMEM((3 * HEAD_PAIRS, 4, tm // 4, LANES), f32)],
        compiler_params=_params("parallel"),
        name="proj",
    )(x2, w_lat, w_kpe, w_rest, wuq, wuk, wuvt, q_norm_g.reshape(1, -1), kv_norm_g.reshape(1, -1),
      *mla_tabs, *dil_tabs)

    t = MLA_TILE
    ya = pl.pallas_call(
        _mla_kernel,
        grid=(batch, HEAD_PAIRS),
        in_specs=[pl.BlockSpec((2, seq, LANES), lambda b, p: (p, b, 0)),
                  pl.BlockSpec((2, seq, LANES), lambda b, p: (p, b, 0)),
                  pl.BlockSpec((seq // tk, LANES, tk), lambda b, p: (b, p, 0)),
                  pl.BlockSpec((1, seq, LANES), lambda b, p: (p, b, 0))],
        out_specs=pl.BlockSpec((seq, LANES), lambda b, p: (b, p)),
        out_shape=jax.ShapeDtypeStruct((rows, MLA_WIDTH), bf16),
        scratch_shapes=[pltpu.VMEM((2, 2, MLA_V_DIM + MLA_ONES_ROWS, t), f32), pltpu.VMEM((2, 2, tk, t), f32)],
        compiler_params=_params("parallel", "parallel"),
        name="mla",
    )(qm, km, vt, ga)

    dil_specs = [pl.BlockSpec((1, 1, dil, seq // dil, LANES), lambda b, p: (p, b, 0, 0, 0))
                 for dil in DILATIONS for _ in range(3)]
    yb = pl.pallas_call(
        _dilated_kernel,
        grid=(batch, HEAD_PAIRS),
        in_specs=dil_specs + [pl.BlockSpec((1, seq, LANES), lambda b, p: (p, b, 0))],
        out_specs=pl.BlockSpec((seq, LANES), lambda b, p: (b, p)),
        out_shape=jax.ShapeDtypeStruct((rows, DIL_WIDTH), bf16),
        scratch_shapes=[pltpu.VMEM((3, seq, LANES), f32)] * 3 + [
            pltpu.VMEM((3, SCATTER_STRIDE, seq // SCATTER_STRIDE, LANES), f32),
            pltpu.VMEM((2, DIL_GROUP, 2 * BLOCK, 2 * BLOCK), f32),
            pltpu.VMEM((2, DIL_GROUP, 2 * BLOCK, 2 * BLOCK), bf16),
            pltpu.VMEM((2, 2 * BLOCK, 2 * BLOCK), f32)],
        compiler_params=_params("parallel", "parallel"),
        name="dilated",
    )(*dil_in, gb)

    to = OUT_ROWS
    const = lambda shape: pl.BlockSpec(shape, lambda i: (0,) * len(shape))
    out = pl.pallas_call(
        _out_kernel,
        grid=(rows // to,),
        in_specs=[pl.BlockSpec((to, D_MODEL), lambda i: (i, 0)),
                  pl.BlockSpec((to, MLA_WIDTH), lambda i: (i, 0)),
                  pl.BlockSpec((to, DIL_WIDTH), lambda i: (i, 0)),
                  const((MLA_WIDTH, D_MODEL)), const((DIL_WIDTH, D_MODEL)),
                  const((1, D_MODEL)), const((1, D_MODEL))],
        out_specs=pl.BlockSpec((to, D_MODEL), lambda i: (i, 0)),
        out_shape=jax.ShapeDtypeStruct((rows, D_MODEL), f32),
        compiler_params=_params("parallel"),
        name="out",
    )(x2, ya, yb, wa, wb, ln_g.reshape(1, -1), ln_b.reshape(1, -1))
    return out.reshape(batch, seq, D_MODEL)
```

```python
import functools

import jax
import jax.numpy as jnp
import numpy as np
from jax import lax
from jax.experimental import pallas as pl
from jax.experimental.pallas import tpu as pltpu

D_MODEL = 1024
ROPE_THETA = 500000.0
BLOCK = 128
NEG = -1e30
RMS_EPS = 1e-6
LN_EPS = 1e-5

MLA_HEADS = 8
MLA_NOPE_DIM = 64
MLA_ROPE_DIM = 32
MLA_V_DIM = 64
Q_LORA_RANK = 384
KV_LORA_RANK = 256
MLA_WIDTH = MLA_HEADS * MLA_V_DIM

DIL_HEADS = 8
DIL_HEAD_DIM = 64
DIL_ROT_DIM = DIL_HEAD_DIM // 4
DIL_WIDTH = DIL_HEADS * DIL_HEAD_DIM
DILATIONS = (1, 4, 16)

DEPTH = 1
DEEPNORM_ALPHA = (2.0 * DEPTH) ** 0.25
LOG2_E = 1.4426950408889634

LANES = 128
HEAD_PAIRS = MLA_HEADS // 2
VMEM_LIMIT_BYTES = 56 * 1024 * 1024

_LATENT_WIDTH = Q_LORA_RANK + KV_LORA_RANK
_OFF_GA = 0
_OFF_QB = _OFF_GA + MLA_WIDTH
_OFF_KB = _OFF_QB + DIL_WIDTH
_OFF_VB = _OFF_KB + DIL_WIDTH
_OFF_GB = _OFF_VB + DIL_WIDTH
_REST_WIDTH = _OFF_GB + DIL_WIDTH

PROJ_ROWS = 512
MLA_KEY_TILE = 256
MLA_TILE = 1024
MLA_UNROLL_TRIPS = 2
MLA_ONES_ROWS = 16
OUT_ROWS = 2048
OUT_CHUNK = 256
DIL_GROUP = 4
SCATTER_STRIDE = 4


def _rope_lanes(x, cos, sin):
    return x * cos + pltpu.roll(x, LANES // 2, 1) * sin


def _proj_kernel(x_ref, wlat_ref, wkpe_ref, w_ref, wuq_ref, wuk_ref, wuvt_ref, qg_ref, kvg_ref,
                 mcos_ref, msin_ref, dcos_ref, dsin_ref,
                 qm_ref, km_ref, vt_ref, ga_ref, gb_ref, *dil_and_scratch):
    f32 = jnp.float32
    bf16 = jnp.bfloat16
    xb = x_ref[...].astype(bf16)
    dil_refs = [dil_and_scratch[3 * i:3 * i + 3] for i in range(len(DILATIONS))]
    stage_ref, stage4_ref = dil_and_scratch[-2:]
    tm = x_ref.shape[0]
    assert DILATIONS == (1, 4, 16)

    def emit_residues(which, p, val):
        slab = which * HEAD_PAIRS + p
        q1, q4, q16 = (refs[which] for refs in dil_refs)
        q1[p, 0, 0] = val.astype(bf16)
        stage_ref[slab] = val
        for a in range(4):
            rows4 = stage_ref[slab, pl.ds(a, tm // 4, stride=4), :]
            q4[p, 0, a] = rows4.astype(bf16)
            stage4_ref[slab, a] = rows4
        for a in range(4):
            for b in range(4):
                rows16 = stage4_ref[slab, a, pl.ds(b, tm // 16, stride=4), :]
                q16[p, 0, a + 4 * b] = rows16.astype(bf16)

    def seg(lo, width):
        return jnp.dot(xb, w_ref[:, lo:lo + width], preferred_element_type=f32)

    def rms(t, g):
        return t * lax.rsqrt(jnp.mean(t * t, axis=-1, keepdims=True) + RMS_EPS) * g

    mcos, msin = mcos_ref[...], msin_ref[...]
    dcos, dsin = dcos_ref[...], dsin_ref[...]
    mla_scale = (MLA_NOPE_DIM + MLA_ROPE_DIM) ** -0.5 * LOG2_E
    dil_scale = DIL_HEAD_DIM ** -0.5 * LOG2_E

    unit = 2 * LANES
    latent = {}

    def slabs(t):
        return [t[:, j * LANES:(j + 1) * LANES] for j in range(t.shape[1] // LANES)]

    def dilated(which, lo, first_pair):
        def epilogue(t):
            for j, val in enumerate(slabs(t)):
                if which < 2:
                    val = _rope_lanes(val, dcos, dsin)
                if which == 0:
                    val = val * dil_scale
                emit_residues(which, first_pair + j, val)
        return (lambda: seg(lo, unit)), epilogue

    def gate(out_ref, lo, first_pair):
        def epilogue(t):
            for j, val in enumerate(slabs(jax.nn.silu(t))):
                out_ref[first_pair + j] = val.astype(bf16)
        return (lambda: seg(lo, unit)), epilogue

    def mla_q(first_head):
        def epilogue(t):
            for j, val in enumerate(slabs(t)):
                val = _rope_lanes(val, mcos, msin) * mla_scale
                qm_ref[first_head + j] = val.astype(bf16)
        lo = first_head * LANES
        return (lambda: jnp.dot(latent["cq"], wuq_ref[:, lo:lo + unit], preferred_element_type=f32)), epilogue

    def mla_k(first_head):
        def epilogue(t):
            for j, val in enumerate(slabs(t)):
                km_ref[first_head + j] = (val + latent["kpe"]).astype(bf16)
        lo = first_head * LANES
        return (lambda: jnp.dot(latent["ckv"], wuk_ref[:, lo:lo + unit], preferred_element_type=f32)), epilogue

    def store_vt(t):
        for c in range(vt_ref.shape[0]):
            vt_ref[c] = t[:, c * MLA_KEY_TILE:(c + 1) * MLA_KEY_TILE].astype(bf16)

    split = unit
    units = [
        (lambda: jnp.dot(xb, wlat_ref[:, :split], preferred_element_type=f32),
         lambda t: latent.update(cq_head=t)),
        (lambda: jnp.dot(xb, wlat_ref[:, split:Q_LORA_RANK], preferred_element_type=f32),
         lambda t: latent.update(cq=rms(jnp.concatenate([latent["cq_head"], t], axis=1), qg_ref[...]).astype(bf16))),
        (lambda: jnp.dot(xb, wlat_ref[:, Q_LORA_RANK:], preferred_element_type=f32),
         lambda t: latent.update(ckv=rms(t, kvg_ref[...]).astype(bf16))),
        dilated(0, _OFF_QB, 0), mla_q(0), dilated(0, _OFF_QB + unit, 2), mla_q(2),
        dilated(1, _OFF_KB, 0), mla_q(4), dilated(1, _OFF_KB + unit, 2), mla_q(6),
        (lambda: jnp.dot(xb, wkpe_ref[...], preferred_element_type=f32),
         lambda t: latent.update(kpe=_rope_lanes(t, mcos, msin))),
        gate(ga_ref, _OFF_GA, 0), mla_k(0), dilated(2, _OFF_VB, 0), mla_k(2),
        gate(ga_ref, _OFF_GA + unit, 2), mla_k(4), dilated(2, _OFF_VB + unit, 2), mla_k(6),
        gate(gb_ref, _OFF_GB, 0), gate(gb_ref, _OFF_GB + unit, 2),
        (lambda: lax.dot_general(wuvt_ref[...], latent["ckv"], (((1,), (1,)), ((), ())),
                                 preferred_element_type=f32), store_vt),
    ]
    pending = None
    for matmul, epilogue in units:
        result = matmul()
        if pending is not None:
            pending[0](pending[1])
        pending = (epilogue, result)
    pending[0](pending[1])


def _mla_kernel(q_ref, k_ref, vt_ref, g_ref, o_ref, accs_ref, s_ref):
    f32 = jnp.float32
    bf16 = jnp.bfloat16
    tq, tk = MLA_TILE, MLA_KEY_TILE
    n_chunks = tq // tk
    assert n_chunks % 2 == 0
    nq = q_ref.shape[1] // tq
    units = [(h, c) for c in range(n_chunks) for h in range(2)]
    ones = jnp.ones((MLA_ONES_ROWS, tk), bf16)
    key = lax.broadcasted_iota(jnp.int32, (tk, tk), 0)
    qry = lax.broadcasted_iota(jnp.int32, (tk, tk), 1)
    lower = key <= qry

    def lanes(c):
        return slice(c * tk, (c + 1) * tk)

    def scores(qi, j, slot, h, c):
        q = q_ref[h, pl.ds(qi * tq + c * tk, tk), :]
        k = k_ref[h, pl.ds(j * tk if isinstance(j, int) else pl.multiple_of(j * tk, tk), tk), :]
        s = lax.dot_general(k, q, (((1,), (1,)), ((), ())), preferred_element_type=f32)
        s_ref[slot, h, :, lanes(c)] = s
        return jnp.max(s, axis=0, keepdims=True)

    def absorb(acc_ref, j, slot, h, c, m_tile, m_old, masked=False):
        s = s_ref[slot, h, :, lanes(c)]
        if masked:
            s = jnp.where(lower, s, NEG)
            m_tile = jnp.max(s, axis=0, keepdims=True)
        m_new = jnp.maximum(m_old, m_tile)
        alpha = jnp.exp2(m_old - m_new)
        p = jnp.exp2(s - m_new).astype(bf16)
        vt = jnp.concatenate([vt_ref[j, h * MLA_V_DIM:(h + 1) * MLA_V_DIM, :], ones], axis=0)
        acc_ref[h, :, lanes(c)] = alpha * acc_ref[h, :, lanes(c)] + jnp.dot(vt, p, preferred_element_type=f32)
        return m_new

    def per_query_tile(qi, acc_ref):
        rows = pl.ds(qi * tq, tq)
        acc_ref[...] = jnp.zeros_like(acc_ref)

        def step(j, slot, tile_max, m_run):
            next_max, m_new = {}, {}
            for u in units:
                next_max[u] = scores(qi, j + 1, 1 - slot, *u)
                m_new[u] = absorb(acc_ref, j, slot, *u, tile_max[u], m_run[u])
            return next_max, m_new

        def body(jj, carry):
            tile_max, m_run = carry
            for i in range(n_chunks):
                tile_max, m_run = step(n_chunks * jj + i, i % 2, tile_max, m_run)
            return tile_max, m_run

        start = {u: jnp.full((1, tk), NEG, f32) for u in units}
        first = {u: scores(qi, 0, 0, *u) for u in units}
        tile_max, m_run = lax.fori_loop(0, qi, body, (first, start), unroll=qi <= MLA_UNROLL_TRIPS)

        base = qi * n_chunks
        for d in range(n_chunks):
            next_max = {}
            for h, c in units:
                if c > d:
                    next_max[h, c] = scores(qi, base + d + 1, (d + 1) % 2, h, c)
                if c >= d:
                    m_run[h, c] = absorb(acc_ref, base + d, d % 2, h, c, tile_max[h, c], m_run[h, c],
                                         masked=(c == d))
            tile_max = next_max

        yt = jnp.concatenate([acc_ref[h, :MLA_V_DIM, :] / acc_ref[h, MLA_V_DIM:MLA_V_DIM + 1, :] for h in range(2)],
                             axis=0)
        o_ref[rows, :] = (yt.T * g_ref[0, rows, :].astype(f32)).astype(o_ref.dtype)

    for qi in range(nq):
        per_query_tile(qi, accs_ref.at[qi % 2])


def _dilated_kernel(q1_ref, k1_ref, v1_ref, q4_ref, k4_ref, v4_ref, q16_ref, k16_ref, v16_ref,
                    g_ref, o_ref, num_ref, max_ref, den_ref, hop_ref, s_ref, p_ref, bias_ref):
    f32 = jnp.float32
    seq = num_ref.shape[1]

    lane = lax.broadcasted_iota(jnp.int32, (BLOCK, LANES), 1)
    low_half = lane < DIL_HEAD_DIM
    first_head = lane % (LANES // 2) < LANES // 4
    ones = jnp.ones((2 * BLOCK, LANES), jnp.bfloat16)
    qi = lax.broadcasted_iota(jnp.int32, (2 * BLOCK, 2 * BLOCK), 0) % BLOCK
    kj = lax.broadcasted_iota(jnp.int32, (2 * BLOCK, 2 * BLOCK), 1)
    dist_first = qi - kj
    dist_later = dist_first + BLOCK
    for kind, dist in enumerate((dist_first, dist_later)):
        bias_ref[kind] = jnp.where((dist >= 0) & (dist <= BLOCK), 0.0, NEG).astype(f32)

    n_groups = seq // BLOCK // DIL_GROUP
    assert n_groups % 2 == 0 and n_groups >= 4

    def branch(idx, dil, q_ref, k_ref, v_ref):
        per_residue = q_ref.shape[3] // BLOCK

        def locate(b):
            return b // per_residue, b % per_residue

        def key_rows(n):
            return pl.ds(pl.multiple_of(jnp.maximum(n - 1, 0) * BLOCK, BLOCK), 2 * BLOCK)

        def out_rows(r, n):
            if dil == 1:
                return pl.ds(pl.multiple_of(n * BLOCK, BLOCK), BLOCK)
            return pl.ds(r + n * (BLOCK * dil), BLOCK, stride=dil)

        two_hops = dil > SCATTER_STRIDE
        inner = dil // SCATTER_STRIDE

        def scatter(kind, stat_ref, r, n, val):
            if two_hops:
                rows = pl.ds(r // SCATTER_STRIDE + n * (BLOCK * inner), BLOCK, stride=inner)
                hop_ref[kind, r % SCATTER_STRIDE, rows, :] = val
            else:
                stat_ref[idx, out_rows(r, n), :] = val

        def finish():
            if not two_hops:
                return
            for kind, stat_ref in enumerate((num_ref, max_ref, den_ref)):
                for a in range(SCATTER_STRIDE):
                    for c in range(hop_ref.shape[2] // BLOCK):
                        rows = pl.ds(a + c * (BLOCK * SCATTER_STRIDE), BLOCK, stride=SCATTER_STRIDE)
                        stat_ref[idx, rows, :] = hop_ref[kind, a, c * BLOCK:(c + 1) * BLOCK, :]

        def scores(g, slot, i):
            r, n = locate(g * DIL_GROUP + i)
            q = q_ref[0, 0, r, pl.ds(pl.multiple_of(n * BLOCK, BLOCK), BLOCK), :]
            k = k_ref[0, 0, r, key_rows(n), :]
            zero = jnp.zeros_like(q)
            q2 = jnp.concatenate([jnp.where(first_head, q, zero), jnp.where(first_head, zero, q)], axis=0)
            s_ref[slot, i] = lax.dot_general(q2, k, (((1,), (1,)), ((), ())), preferred_element_type=f32)

        def softmax(g, slot, i):
            r, n = locate(g * DIL_GROUP + i)
            s = s_ref[slot, i] + bias_ref[jnp.minimum(n, 1)]
            m = jnp.max(s, axis=1, keepdims=True)
            p_ref[slot, i] = jnp.exp2(s - m).astype(jnp.bfloat16)
            scatter(1, max_ref, r, n, jnp.where(low_half, m[:BLOCK], m[BLOCK:]))

        def values(g, slot, i):
            r, n = locate(g * DIL_GROUP + i)
            v = jnp.concatenate([v_ref[0, 0, r, key_rows(n), :], ones], axis=1)
            both = jnp.dot(p_ref[slot, i], v, preferred_element_type=f32)
            num, den = both[:, :LANES], both[:, LANES:]
            num = jnp.where(low_half, num[:BLOCK], num[BLOCK:])
            den = jnp.where(low_half, den[:BLOCK], den[BLOCK:])
            if dil != 1:
                scatter(0, num_ref, r, n, num)
                scatter(2, den_ref, r, n, den)
                return
            rows = out_rows(r, n)
            others = [i for i in range(len(DILATIONS)) if i != idx]
            ms = [max_ref[idx, rows, :]] + [max_ref[i, rows, :] for i in others]
            m_all = functools.reduce(jnp.maximum, ms)
            w = jnp.exp2(ms[0] - m_all)
            num, den = w * num, w * den
            for m, i in zip(ms[1:], others):
                w = jnp.exp2(m - m_all)
                num = num + w * num_ref[i, rows, :]
                den = den + w * den_ref[i, rows, :]
            o_ref[rows, :] = (num / den * g_ref[0, rows, :].astype(f32)).astype(o_ref.dtype)

        return scores, softmax, values, finish

    assert DILATIONS[0] == 1
    stages = [branch(2, DILATIONS[2], q16_ref, k16_ref, v16_ref),
              branch(1, DILATIONS[1], q4_ref, k4_ref, v4_ref),
              branch(0, DILATIONS[0], q1_ref, k1_ref, v1_ref)]
    last = n_groups - 1

    def trip(*work):
        for stage, g, slot in work:
            for i in range(DIL_GROUP):
                stage(g, slot, i)

    for b, (scores, softmax, values, _) in enumerate(stages):
        if b == 0:
            trip((scores, 0, 0))
            trip((scores, 1, 1), (softmax, 0, 0))
        else:
            trip((stages[b - 1][2], last, 1), (scores, 1, 1), (softmax, 0, 0))
            stages[b - 1][3]()

        def two_trips(t, carry, scores=scores, softmax=softmax, values=values):
            trip((values, 2 * t, 0), (scores, 2 * t + 2, 0), (softmax, 2 * t + 1, 1))
            trip((values, 2 * t + 1, 1), (scores, 2 * t + 3, 1), (softmax, 2 * t + 2, 0))
            return carry

        for t in range((n_groups - 2) // 2):
            two_trips(t, 0)
        if b + 1 < len(stages):
            trip((values, last - 1, 0), (stages[b + 1][0], 0, 0), (softmax, last, 1))
        else:
            trip((values, last - 1, 0), (softmax, last, 1))
    trip((stages[-1][2], last, 1))


def _out_kernel(x_ref, ya_ref, yb_ref, wa_ref, wb_ref, g_ref, b_ref, o_ref):
    f32 = jnp.float32
    chunks = [pl.ds(r, OUT_CHUNK) for r in range(0, x_ref.shape[0], OUT_CHUNK)]

    def project(rows):
        return (jnp.dot(ya_ref[rows, :], wa_ref[...], preferred_element_type=f32)
                + jnp.dot(yb_ref[rows, :], wb_ref[...], preferred_element_type=f32))

    def normalise(rows, mixed):
        h = DEEPNORM_ALPHA * x_ref[rows, :] + mixed
        mu = jnp.mean(h, axis=-1, keepdims=True)
        c = h - mu
        var = jnp.mean(c * c, axis=-1, keepdims=True)
        o_ref[rows, :] = c * lax.rsqrt(var + LN_EPS) * g_ref[...] + b_ref[...]

    mixed = project(chunks[0])
    for i, rows in enumerate(chunks):
        ahead = project(chunks[i + 1]) if i + 1 < len(chunks) else None
        normalise(rows, mixed)
        mixed = ahead


_HALF_ROPE = MLA_ROPE_DIM // 2
_NOPE_LOW = LANES // 2 - _HALF_ROPE
_MLA_Q_LANES = ((MLA_NOPE_DIM, MLA_NOPE_DIM + _HALF_ROPE), (0, _NOPE_LOW),
                (MLA_NOPE_DIM + _HALF_ROPE, MLA_NOPE_DIM + MLA_ROPE_DIM), (_NOPE_LOW, MLA_NOPE_DIM),
                ("zero", LANES // 2 - _HALF_ROPE - (MLA_NOPE_DIM - _NOPE_LOW)))
_MLA_K_LANES = (("zero", _HALF_ROPE), (0, _NOPE_LOW), ("zero", _HALF_ROPE), (_NOPE_LOW, MLA_NOPE_DIM),
                ("zero", LANES // 2 - _HALF_ROPE - (MLA_NOPE_DIM - _NOPE_LOW)))
_MLA_KPE_LANES = ((0, _HALF_ROPE), ("zero", LANES // 2 - _HALF_ROPE), (_HALF_ROPE, MLA_ROPE_DIM),
                  ("zero", LANES // 2 - _HALF_ROPE))
_DIL_HALF_ROT = DIL_ROT_DIM // 2
_DIL_QUARTER = LANES // 4
_DIL_PLAIN_LOW = _DIL_QUARTER - _DIL_HALF_ROT


def _dil_pair_lanes():
    runs = []
    for upper in (False, True):
        for head in range(2):
            base = head * DIL_HEAD_DIM
            rot = base + (_DIL_HALF_ROT if upper else 0)
            plain = base + DIL_ROT_DIM + (_DIL_PLAIN_LOW if upper else 0)
            runs += [(rot, rot + _DIL_HALF_ROT), (plain, plain + _DIL_PLAIN_LOW)]
    return tuple(runs)


def _place_lanes(w, runs):
    n_cols = w.shape[-1]
    select = np.zeros((n_cols, LANES), np.float32)
    lane = 0
    for run in runs:
        if run[0] == "zero":
            lane += run[1]
        else:
            for col in range(run[0], run[1]):
                select[col, lane] = 1.0
                lane += 1
    assert lane == LANES
    placed = jnp.einsum("rgc,cl->rgl", w, jnp.asarray(select, w.dtype), preferred_element_type=w.dtype)
    return placed.reshape(w.shape[0], -1)


def _rope_tables(seq, rot_dim, rot_lanes):
    half = rot_dim // 2
    inv_freq = ROPE_THETA ** (-np.arange(0, rot_dim, 2, dtype=np.float64) / rot_dim)
    ang = np.arange(seq, dtype=np.float64)[:, None] * inv_freq[None, :]
    cos = np.ones((seq, LANES), np.float32)
    sin = np.zeros((seq, LANES), np.float32)
    for lane in rot_lanes:
        for first, sign in ((lane, -1.0), (lane + LANES // 2, 1.0)):
            cos[:, first:first + half] = np.cos(ang)
            sin[:, first:first + half] = sign * np.sin(ang)
    return jnp.asarray(cos), jnp.asarray(sin)


def _params(*semantics, flags=None):
    return pltpu.CompilerParams(dimension_semantics=semantics, vmem_limit_bytes=VMEM_LIMIT_BYTES, flags=flags)


def kernel(x, w_in, q_norm_g, kv_norm_g, w_uq, w_ukv, w_out, ln_g, ln_b):
    f32, bf16 = jnp.float32, jnp.bfloat16
    batch, seq, _ = x.shape
    rows = batch * seq
    x2 = x.reshape(rows, D_MODEL)

    rope_end = _LATENT_WIDTH + MLA_ROPE_DIM
    w_lat = w_in[:, :_LATENT_WIDTH].astype(bf16)
    w_kpe = _place_lanes(w_in[:, None, _LATENT_WIDTH:rope_end].astype(bf16), _MLA_KPE_LANES)
    w_rest = w_in[:, rope_end:].astype(bf16)
    assert w_rest.shape[1] == _REST_WIDTH
    w_qk = _place_lanes(w_rest[:, _OFF_QB:_OFF_VB].reshape(D_MODEL, -1, LANES), _dil_pair_lanes())
    w_rest = jnp.concatenate([w_rest[:, :_OFF_QB], w_qk, w_rest[:, _OFF_VB:]], axis=1)
    dk = MLA_NOPE_DIM + MLA_ROPE_DIM
    wuq = _place_lanes(w_uq.astype(bf16).reshape(Q_LORA_RANK, MLA_HEADS, dk), _MLA_Q_LANES)
    wukv = w_ukv.astype(bf16).reshape(KV_LORA_RANK, MLA_HEADS, MLA_NOPE_DIM + MLA_V_DIM)
    wuk = _place_lanes(wukv[:, :, :MLA_NOPE_DIM], _MLA_K_LANES)
    wuvt = wukv[:, :, MLA_NOPE_DIM:].reshape(KV_LORA_RANK, MLA_WIDTH).T
    wa = w_out[:MLA_WIDTH].astype(bf16)
    wb = w_out[MLA_WIDTH:].astype(bf16)

    mla_tabs = _rope_tables(seq, MLA_ROPE_DIM, (0,))
    dil_tabs = _rope_tables(seq, DIL_ROT_DIM, (0, _DIL_QUARTER))

    tm = PROJ_ROWS
    seq_tiles = seq // tm
    full = lambda shape: pl.BlockSpec(shape, lambda i: (0,) * len(shape))
    tab = pl.BlockSpec((tm, LANES), lambda i: (i % seq_tiles, 0))
    slab = lambda n: pl.BlockSpec((n, tm, LANES), lambda i: (0, i, 0))
    slab_shape = lambda n: jax.ShapeDtypeStruct((n, rows, LANES), bf16)
    tk = MLA_KEY_TILE
    vt_spec = pl.BlockSpec((tm // tk, MLA_WIDTH, tk), lambda i: (i, 0, 0))
    vt_shape = jax.ShapeDtypeStruct((rows // tk, MLA_WIDTH, tk), bf16)
    res_specs, res_shapes = [], []
    for dil in DILATIONS:
        spec = pl.BlockSpec((HEAD_PAIRS, 1, dil, tm // dil, LANES),
                            lambda i: (0, i // seq_tiles, 0, i % seq_tiles, 0))
        shape = jax.ShapeDtypeStruct((HEAD_PAIRS, batch, dil, seq // dil, LANES), bf16)
        res_specs += [spec] * 3
        res_shapes += [shape] * 3
    qm, km, vt, ga, gb, *dil_in = pl.pallas_call(
        _proj_kernel,
        grid=(rows // tm,),
        in_specs=[pl.BlockSpec((tm, D_MODEL), lambda i: (i, 0)),
                  full(w_lat.shape), full(w_kpe.shape), full(w_rest.shape),
                  full(wuq.shape), full(wuk.shape), full(wuvt.shape),
                  full((1, Q_LORA_RANK)), full((1, KV_LORA_RANK)),
                  tab, tab, tab, tab],
        out_specs=[slab(MLA_HEADS), slab(MLA_HEADS), vt_spec, slab(HEAD_PAIRS), slab(HEAD_PAIRS)] + res_specs,
        out_shape=[slab_shape(MLA_HEADS), slab_shape(MLA_HEADS), vt_shape, slab_shape(HEAD_PAIRS),
                   slab_shape(HEAD_PAIRS)] + res_shapes,
        scratch_shapes=[pltpu.VMEM((3 * HEAD_PAIRS, tm, LANES), f32),
                        pltpu.VMEM((3 * HEAD_PAIRS, 4, tm // 4, LANES), f32)],
        compiler_params=_params("parallel"),
        name="proj",
    )(x2, w_lat, w_kpe, w_rest, wuq, wuk, wuvt, q_norm_g.reshape(1, -1), kv_norm_g.reshape(1, -1),
      *mla_tabs, *dil_tabs)

    t = MLA_TILE
    ya = pl.pallas_call(
        _mla_kernel,
        grid=(batch, HEAD_PAIRS),
        in_specs=[pl.BlockSpec((2, seq, LANES), lambda b, p: (p, b, 0)),
                  pl.BlockSpec((2, seq, LANES), lambda b, p: (p, b, 0)),
                  pl.BlockSpec((seq // tk, LANES, tk), lambda b, p: (b, p, 0)),
                  pl.BlockSpec((1, seq, LANES), lambda b, p: (p, b, 0))],
        out_specs=pl.BlockSpec((seq, LANES), lambda b, p: (b, p)),
        out_shape=jax.ShapeDtypeStruct((rows, MLA_WIDTH), bf16),
        scratch_shapes=[pltpu.VMEM((2, 2, MLA_V_DIM + MLA_ONES_ROWS, t), f32), pltpu.VMEM((2, 2, tk, t), f32)],
        compiler_params=_params("parallel", "parallel"),
        name="mla",
    )(qm, km, vt, ga)

    dil_specs = [pl.BlockSpec((1, 1, dil, seq // dil, LANES), lambda b, p: (p, b, 0, 0, 0))
                 for dil in DILATIONS for _ in range(3)]
    yb = pl.pallas_call(
        _dilated_kernel,
        grid=(batch, HEAD_PAIRS),
        in_specs=dil_specs + [pl.BlockSpec((1, seq, LANES), lambda b, p: (p, b, 0))],
        out_specs=pl.BlockSpec((seq, LANES), lambda b, p: (b, p)),
        out_shape=jax.ShapeDtypeStruct((rows, DIL_WIDTH), bf16),
        scratch_shapes=[pltpu.VMEM((3, seq, LANES), f32)] * 3 + [
            pltpu.VMEM((3, SCATTER_STRIDE, seq // SCATTER_STRIDE, LANES), f32),
            pltpu.VMEM((2, DIL_GROUP, 2 * BLOCK, 2 * BLOCK), f32),
            pltpu.VMEM((2, DIL_GROUP, 2 * BLOCK, 2 * BLOCK), bf16),
            pltpu.VMEM((2, 2 * BLOCK, 2 * BLOCK), f32)],
        compiler_params=_params("parallel", "parallel"),
        name="dilated",
    )(*dil_in, gb)

    to = OUT_ROWS
    const = lambda shape: pl.BlockSpec(shape, lambda i: (0,) * len(shape))
    out = pl.pallas_call(
        _out_kernel,
        grid=(rows // to,),
        in_specs=[pl.BlockSpec((to, D_MODEL), lambda i: (i, 0)),
                  pl.BlockSpec((to, MLA_WIDTH), lambda i: (i, 0)),
                  pl.BlockSpec((to, DIL_WIDTH), lambda i: (i, 0)),
                  const((MLA_WIDTH, D_MODEL)), const((DIL_WIDTH, D_MODEL)),
                  const((1, D_MODEL)), const((1, D_MODEL))],
        out_specs=pl.BlockSpec((to, D_MODEL), lambda i: (i, 0)),
        out_shape=jax.ShapeDtypeStruct((rows, D_MODEL), f32),
        compiler_params=_params("parallel"),
        name="out",
    )(x2, ya, yb, wa, wb, ln_g.reshape(1, -1), ln_b.reshape(1, -1))
    return out.reshape(batch, seq, D_MODEL)
```

```python
import functools

import jax
import jax.numpy as jnp
import numpy as np
from jax import lax
from jax.experimental import pallas as pl
from jax.experimental.pallas import tpu as pltpu

D_MODEL = 1024
ROPE_THETA = 500000.0
BLOCK = 128
NEG = -1e30
RMS_EPS = 1e-6
LN_EPS = 1e-5

MLA_HEADS = 8
MLA_NOPE_DIM = 64
MLA_ROPE_DIM = 32
MLA_V_DIM = 64
Q_LORA_RANK = 384
KV_LORA_RANK = 256
MLA_WIDTH = MLA_HEADS * MLA_V_DIM

DIL_HEADS = 8
DIL_HEAD_DIM = 64
DIL_ROT_DIM = DIL_HEAD_DIM // 4
DIL_WIDTH = DIL_HEADS * DIL_HEAD_DIM
DILATIONS = (1, 4, 16)

DEPTH = 1
DEEPNORM_ALPHA = (2.0 * DEPTH) ** 0.25
LOG2_E = 1.4426950408889634

LANES = 128
HEAD_PAIRS = MLA_HEADS // 2
VMEM_LIMIT_BYTES = 56 * 1024 * 1024

_LATENT_WIDTH = Q_LORA_RANK + KV_LORA_RANK
_OFF_GA = 0
_OFF_QB = _OFF_GA + MLA_WIDTH
_OFF_KB = _OFF_QB + DIL_WIDTH
_OFF_VB = _OFF_KB + DIL_WIDTH
_OFF_GB = _OFF_VB + DIL_WIDTH
_REST_WIDTH = _OFF_GB + DIL_WIDTH

PROJ_ROWS = 512
MLA_KEY_TILE = 256
MLA_TILE = 1024
MLA_UNROLL_TRIPS = 2
MLA_ONES_ROWS = 16
OUT_ROWS = 2048
OUT_CHUNK = 256
DIL_GROUP = 4
SCATTER_STRIDE = 4


def _rope_lanes(x, cos, sin_fwd, sin_bwd, half):
    fwd = pltpu.roll(x, half, 1)
    bwd = pltpu.roll(x, LANES - half, 1)
    return x * cos + fwd * sin_fwd + bwd * sin_bwd


def _proj_kernel(x_ref, wlat_ref, wkpe_ref, w_ref, wuq_ref, wuk_ref, wuvt_ref, qg_ref, kvg_ref,
                 mcos_ref, msf_ref, msb_ref, dcos_ref, dsf_ref, dsb_ref,
                 qm_ref, km_ref, vt_ref, ga_ref, gb_ref, *dil_and_scratch):
    f32 = jnp.float32
    bf16 = jnp.bfloat16
    xb = x_ref[...].astype(bf16)
    dil_refs = [dil_and_scratch[3 * i:3 * i + 3] for i in range(len(DILATIONS))]
    stage_ref, stage4_ref = dil_and_scratch[-2:]
    tm = x_ref.shape[0]
    unit = 2 * LANES
    assert DILATIONS == (1, 4, 16)

    def emit_residues(which, p, val):
        slab = which * HEAD_PAIRS + p
        q1, q4, q16 = (refs[which] for refs in dil_refs)
        q1[p, 0, 0] = val.astype(bf16)
        stage_ref[slab] = val
        for a in range(4):
            rows4 = stage_ref[slab, pl.ds(a, tm // 4, stride=4), :]
            q4[p, 0, a] = rows4.astype(bf16)
            stage4_ref[slab, a] = rows4
        for a in range(4):
            for b in range(4):
                rows16 = stage4_ref[slab, a, pl.ds(b, tm // 16, stride=4), :]
                q16[p, 0, a + 4 * b] = rows16.astype(bf16)

    def seg(lo):
        return jnp.dot(xb, w_ref[:, lo:lo + unit], preferred_element_type=f32)

    def rms(t, g):
        return t * lax.rsqrt(jnp.mean(t * t, axis=-1, keepdims=True) + RMS_EPS) * g

    mcos, msf, msb = mcos_ref[...], msf_ref[...], msb_ref[...]
    dcos, dsf, dsb = dcos_ref[...], dsf_ref[...], dsb_ref[...]
    mla_scale = (MLA_NOPE_DIM + MLA_ROPE_DIM) ** -0.5 * LOG2_E
    dil_scale = DIL_HEAD_DIM ** -0.5 * LOG2_E

    latent = {}

    def slabs(t):
        return [t[:, j * LANES:(j + 1) * LANES] for j in range(t.shape[1] // LANES)]

    def dilated(which, lo, first_pair):
        def epilogue(t):
            for j, val in enumerate(slabs(t)):
                if which < 2:
                    val = _rope_lanes(val, dcos, dsf, dsb, DIL_ROT_DIM // 2)
                if which == 0:
                    val = val * dil_scale
                emit_residues(which, first_pair + j, val)
        return (lambda: seg(lo)), epilogue

    def gate(out_ref, lo, first_pair):
        def epilogue(t):
            for j, val in enumerate(slabs(jax.nn.silu(t))):
                out_ref[first_pair + j] = val.astype(bf16)
        return (lambda: seg(lo)), epilogue

    def mla_q(first_head):
        def epilogue(t):
            for j, val in enumerate(slabs(t)):
                val = _rope_lanes(val, mcos, msf, msb, MLA_ROPE_DIM // 2) * mla_scale
                qm_ref[first_head + j] = val.astype(bf16)
        lo = first_head * LANES
        return (lambda: jnp.dot(latent["cq"], wuq_ref[:, lo:lo + unit], preferred_element_type=f32)), epilogue

    def mla_k(first_head):
        def epilogue(t):
            for j, val in enumerate(slabs(t)):
                km_ref[first_head + j] = (val + latent["kpe"]).astype(bf16)
        lo = first_head * LANES
        return (lambda: jnp.dot(latent["ckv"], wuk_ref[:, lo:lo + unit], preferred_element_type=f32)), epilogue

    def store_vt(t):
        for c in range(vt_ref.shape[0]):
            vt_ref[c] = t[:, c * MLA_KEY_TILE:(c + 1) * MLA_KEY_TILE].astype(bf16)

    split = unit
    units = [
        (lambda: jnp.dot(xb, wlat_ref[:, :split], preferred_element_type=f32),
         lambda t: latent.update(cq_head=t)),
        (lambda: jnp.dot(xb, wlat_ref[:, split:Q_LORA_RANK], preferred_element_type=f32),
         lambda t: latent.update(cq=rms(jnp.concatenate([latent["cq_head"], t], axis=1), qg_ref[...]).astype(bf16))),
        (lambda: jnp.dot(xb, wlat_ref[:, Q_LORA_RANK:], preferred_element_type=f32),
         lambda t: latent.update(ckv=rms(t, kvg_ref[...]).astype(bf16))),
        dilated(0, _OFF_QB, 0), mla_q(0), dilated(0, _OFF_QB + unit, 2), mla_q(2),
        dilated(1, _OFF_KB, 0), mla_q(4), dilated(1, _OFF_KB + unit, 2), mla_q(6),
        (lambda: jnp.dot(xb, wkpe_ref[...], preferred_element_type=f32),
         lambda t: latent.update(kpe=_rope_lanes(t, mcos, msf, msb, MLA_ROPE_DIM // 2))),
        gate(ga_ref, _OFF_GA, 0), mla_k(0), dilated(2, _OFF_VB, 0), mla_k(2),
        gate(ga_ref, _OFF_GA + unit, 2), mla_k(4), dilated(2, _OFF_VB + unit, 2), mla_k(6),
        gate(gb_ref, _OFF_GB, 0), gate(gb_ref, _OFF_GB + unit, 2),
        (lambda: lax.dot_general(wuvt_ref[...], latent["ckv"], (((1,), (1,)), ((), ())),
                                 preferred_element_type=f32), store_vt),
    ]
    pending = None
    for matmul, epilogue in units:
        result = matmul()
        if pending is not None:
            pending[0](pending[1])
        pending = (epilogue, result)
    pending[0](pending[1])


def _mla_kernel(q_ref, k_ref, vt_ref, g_ref, o_ref, accs_ref, s_ref):
    f32 = jnp.float32
    bf16 = jnp.bfloat16
    tq, tk = MLA_TILE, MLA_KEY_TILE
    n_chunks = tq // tk
    assert n_chunks % 2 == 0
    nq = q_ref.shape[1] // tq
    units = [(h, c) for c in range(n_chunks) for h in range(2)]
    ones = jnp.ones((MLA_ONES_ROWS, tk), bf16)
    key = lax.broadcasted_iota(jnp.int32, (tk, tk), 0)
    qry = lax.broadcasted_iota(jnp.int32, (tk, tk), 1)
    lower = key <= qry

    def lanes(c):
        return slice(c * tk, (c + 1) * tk)

    def scores(qi, j, slot, h, c):
        q = q_ref[h, pl.ds(qi * tq + c * tk, tk), :]
        k = k_ref[h, pl.ds(j * tk if isinstance(j, int) else pl.multiple_of(j * tk, tk), tk), :]
        s = lax.dot_general(k, q, (((1,), (1,)), ((), ())), preferred_element_type=f32)
        s_ref[slot, h, :, lanes(c)] = s
        return jnp.max(s, axis=0, keepdims=True)

    def absorb(acc_ref, j, slot, h, c, m_tile, m_old, masked=False):
        s = s_ref[slot, h, :, lanes(c)]
        if masked:
            s = jnp.where(lower, s, NEG)
            m_tile = jnp.max(s, axis=0, keepdims=True)
        m_new = jnp.maximum(m_old, m_tile)
        alpha = jnp.exp2(m_old - m_new)
        p = jnp.exp2(s - m_new).astype(bf16)
        vt = jnp.concatenate([vt_ref[j, h * MLA_V_DIM:(h + 1) * MLA_V_DIM, :], ones], axis=0)
        acc_ref[h, :, lanes(c)] = alpha * acc_ref[h, :, lanes(c)] + jnp.dot(vt, p, preferred_element_type=f32)
        return m_new

    def per_query_tile(qi, acc_ref):
        rows = pl.ds(qi * tq, tq)
        acc_ref[...] = jnp.zeros_like(acc_ref)

        def step(j, slot, tile_max, m_run):
            next_max, m_new = {}, {}
            for u in units:
                next_max[u] = scores(qi, j + 1, 1 - slot, *u)
                m_new[u] = absorb(acc_ref, j, slot, *u, tile_max[u], m_run[u])
            return next_max, m_new

        def body(jj, carry):
            tile_max, m_run = carry
            for i in range(n_chunks):
                tile_max, m_run = step(n_chunks * jj + i, i % 2, tile_max, m_run)
            return tile_max, m_run

        start = {u: jnp.full((1, tk), NEG, f32) for u in units}
        first = {u: scores(qi, 0, 0, *u) for u in units}
        tile_max, m_run = lax.fori_loop(0, qi, body, (first, start), unroll=qi <= MLA_UNROLL_TRIPS)

        base = qi * n_chunks
        for d in range(n_chunks):
            next_max = {}
            for h, c in units:
                if c > d:
                    next_max[h, c] = scores(qi, base + d + 1, (d + 1) % 2, h, c)
                if c >= d:
                    m_run[h, c] = absorb(acc_ref, base + d, d % 2, h, c, tile_max[h, c], m_run[h, c],
                                         masked=(c == d))
            tile_max = next_max

        yt = jnp.concatenate([acc_ref[h, :MLA_V_DIM, :] / acc_ref[h, MLA_V_DIM:MLA_V_DIM + 1, :] for h in range(2)],
                             axis=0)
        o_ref[rows, :] = (yt.T * g_ref[0, rows, :].astype(f32)).astype(o_ref.dtype)

    for qi in range(nq):
        per_query_tile(qi, accs_ref.at[qi % 2])


def _dilated_kernel(q1_ref, k1_ref, v1_ref, q4_ref, k4_ref, v4_ref, q16_ref, k16_ref, v16_ref,
                    g_ref, o_ref, num_ref, max_ref, den_ref, hop_ref, s_ref, p_ref, bias_ref):
    f32 = jnp.float32
    seq = num_ref.shape[1]

    lane = lax.broadcasted_iota(jnp.int32, (BLOCK, LANES), 1)
    low_half = lane < DIL_HEAD_DIM
    ones = jnp.ones((2 * BLOCK, LANES), jnp.bfloat16)
    qi = lax.broadcasted_iota(jnp.int32, (2 * BLOCK, 2 * BLOCK), 0) % BLOCK
    kj = lax.broadcasted_iota(jnp.int32, (2 * BLOCK, 2 * BLOCK), 1)
    dist_first = qi - kj
    dist_later = dist_first + BLOCK
    for kind, dist in enumerate((dist_first, dist_later)):
        bias_ref[kind] = jnp.where((dist >= 0) & (dist <= BLOCK), 0.0, NEG).astype(f32)

    n_groups = seq // BLOCK // DIL_GROUP
    assert n_groups % 2 == 0 and n_groups >= 4

    def branch(idx, dil, q_ref, k_ref, v_ref):
        per_residue = q_ref.shape[3] // BLOCK

        def locate(b):
            return b // per_residue, b % per_residue

        def key_rows(n):
            return pl.ds(pl.multiple_of(jnp.maximum(n - 1, 0) * BLOCK, BLOCK), 2 * BLOCK)

        def out_rows(r, n):
            if dil == 1:
                return pl.ds(pl.multiple_of(n * BLOCK, BLOCK), BLOCK)
            return pl.ds(r + n * (BLOCK * dil), BLOCK, stride=dil)

        two_hops = dil > SCATTER_STRIDE
        inner = dil // SCATTER_STRIDE

        def scatter(kind, stat_ref, r, n, val):
            if two_hops:
                rows = pl.ds(r // SCATTER_STRIDE + n * (BLOCK * inner), BLOCK, stride=inner)
                hop_ref[kind, r % SCATTER_STRIDE, rows, :] = val
            else:
                stat_ref[idx, out_rows(r, n), :] = val

        def finish():
            if not two_hops:
                return
            for kind, stat_ref in enumerate((num_ref, max_ref, den_ref)):
                for a in range(SCATTER_STRIDE):
                    for c in range(hop_ref.shape[2] // BLOCK):
                        rows = pl.ds(a + c * (BLOCK * SCATTER_STRIDE), BLOCK, stride=SCATTER_STRIDE)
                        stat_ref[idx, rows, :] = hop_ref[kind, a, c * BLOCK:(c + 1) * BLOCK, :]

        def scores(g, slot, i):
            r, n = locate(g * DIL_GROUP + i)
            q = q_ref[0, 0, r, pl.ds(pl.multiple_of(n * BLOCK, BLOCK), BLOCK), :]
            k = k_ref[0, 0, r, key_rows(n), :]
            zero = jnp.zeros_like(q)
            q2 = jnp.concatenate([jnp.where(low_half, q, zero), jnp.where(low_half, zero, q)], axis=0)
            s_ref[slot, i] = lax.dot_general(q2, k, (((1,), (1,)), ((), ())), preferred_element_type=f32)

        def softmax(g, slot, i):
            r, n = locate(g * DIL_GROUP + i)
            s = s_ref[slot, i] + bias_ref[jnp.minimum(n, 1)]
            m = jnp.max(s, axis=1, keepdims=True)
            p_ref[slot, i] = jnp.exp2(s - m).astype(jnp.bfloat16)
            scatter(1, max_ref, r, n, jnp.where(low_half, m[:BLOCK], m[BLOCK:]))

        def values(g, slot, i):
            r, n = locate(g * DIL_GROUP + i)
            v = jnp.concatenate([v_ref[0, 0, r, key_rows(n), :], ones], axis=1)
            both = jnp.dot(p_ref[slot, i], v, preferred_element_type=f32)
            num, den = both[:, :LANES], both[:, LANES:]
            num = jnp.where(low_half, num[:BLOCK], num[BLOCK:])
            den = jnp.where(low_half, den[:BLOCK], den[BLOCK:])
            if dil != 1:
                scatter(0, num_ref, r, n, num)
                scatter(2, den_ref, r, n, den)
                return
            rows = out_rows(r, n)
            others = [i for i in range(len(DILATIONS)) if i != idx]
            ms = [max_ref[idx, rows, :]] + [max_ref[i, rows, :] for i in others]
            m_all = functools.reduce(jnp.maximum, ms)
            w = jnp.exp2(ms[0] - m_all)
            num, den = w * num, w * den
            for m, i in zip(ms[1:], others):
                w = jnp.exp2(m - m_all)
                num = num + w * num_ref[i, rows, :]
                den = den + w * den_ref[i, rows, :]
            o_ref[rows, :] = (num / den * g_ref[0, rows, :].astype(f32)).astype(o_ref.dtype)

        return scores, softmax, values, finish

    assert DILATIONS[0] == 1
    stages = [branch(2, DILATIONS[2], q16_ref, k16_ref, v16_ref),
              branch(1, DILATIONS[1], q4_ref, k4_ref, v4_ref),
              branch(0, DILATIONS[0], q1_ref, k1_ref, v1_ref)]
    last = n_groups - 1

    def trip(*work):
        for stage, g, slot in work:
            for i in range(DIL_GROUP):
                stage(g, slot, i)

    for b, (scores, softmax, values, _) in enumerate(stages):
        if b == 0:
            trip((scores, 0, 0))
            trip((scores, 1, 1), (softmax, 0, 0))
        else:
            trip((stages[b - 1][2], last, 1), (scores, 1, 1), (softmax, 0, 0))
            stages[b - 1][3]()

        for t in range((n_groups - 2) // 2):
            trip((values, 2 * t, 0), (scores, 2 * t + 2, 0), (softmax, 2 * t + 1, 1))
            trip((values, 2 * t + 1, 1), (scores, 2 * t + 3, 1), (softmax, 2 * t + 2, 0))
        if b + 1 < len(stages):
            trip((values, last - 1, 0), (stages[b + 1][0], 0, 0), (softmax, last, 1))
        else:
            trip((values, last - 1, 0), (softmax, last, 1))
    trip((stages[-1][2], last, 1))


def _out_kernel(x_ref, ya_ref, yb_ref, wa_ref, wb_ref, g_ref, b_ref, o_ref):
    f32 = jnp.float32
    chunks = [pl.ds(r, OUT_CHUNK) for r in range(0, x_ref.shape[0], OUT_CHUNK)]

    def project(rows):
        return (jnp.dot(ya_ref[rows, :], wa_ref[...], preferred_element_type=f32)
                + jnp.dot(yb_ref[rows, :], wb_ref[...], preferred_element_type=f32))

    def normalise(rows, mixed):
        h = DEEPNORM_ALPHA * x_ref[rows, :] + mixed
        mu = jnp.mean(h, axis=-1, keepdims=True)
        c = h - mu
        var = jnp.mean(c * c, axis=-1, keepdims=True)
        o_ref[rows, :] = c * lax.rsqrt(var + LN_EPS) * g_ref[...] + b_ref[...]

    mixed = project(chunks[0])
    for i, rows in enumerate(chunks):
        ahead = project(chunks[i + 1]) if i + 1 < len(chunks) else None
        normalise(rows, mixed)
        mixed = ahead


def _rope_tables(seq, rot_dim, period, rot_offset, pass_rest):
    half = rot_dim // 2
    f32 = np.float32
    inv_freq = ROPE_THETA ** (-np.arange(0, rot_dim, 2, dtype=np.float64) / rot_dim)
    ang = np.arange(seq, dtype=np.float64)[:, None] * inv_freq[None, :]
    cos, sin = np.cos(ang).astype(f32), np.sin(ang).astype(f32)
    zeros = np.zeros((seq, half), f32)
    rest = period - rot_offset - rot_dim
    fill = np.ones if pass_rest else np.zeros
    group_cos = np.concatenate([np.ones((seq, rot_offset), f32), cos, cos, fill((seq, rest), f32)], axis=1)
    group_fwd = np.concatenate([np.zeros((seq, rot_offset), f32), zeros, sin, np.zeros((seq, rest), f32)], axis=1)
    group_bwd = np.concatenate([np.zeros((seq, rot_offset), f32), -sin, zeros, np.zeros((seq, rest), f32)], axis=1)
    reps = LANES // period
    return tuple(jnp.asarray(np.tile(t, (1, reps))) for t in (group_cos, group_fwd, group_bwd))


def _params(*semantics):
    return pltpu.CompilerParams(dimension_semantics=semantics, vmem_limit_bytes=VMEM_LIMIT_BYTES)


def kernel(x, w_in, q_norm_g, kv_norm_g, w_uq, w_ukv, w_out, ln_g, ln_b):
    f32, bf16 = jnp.float32, jnp.bfloat16
    batch, seq, _ = x.shape
    rows = batch * seq
    x2 = x.reshape(rows, D_MODEL)

    rope_end = _LATENT_WIDTH + MLA_ROPE_DIM
    w_lat = w_in[:, :_LATENT_WIDTH].astype(bf16)
    w_kpe = jnp.pad(w_in[:, _LATENT_WIDTH:rope_end].astype(bf16),
                    ((0, 0), (MLA_NOPE_DIM, LANES - MLA_NOPE_DIM - MLA_ROPE_DIM)))
    w_rest = w_in[:, rope_end:].astype(bf16)
    assert w_rest.shape[1] == _REST_WIDTH
    dk = MLA_NOPE_DIM + MLA_ROPE_DIM
    wuq = jnp.pad(w_uq.reshape(Q_LORA_RANK, MLA_HEADS, dk), ((0, 0), (0, 0), (0, LANES - dk)))
    wuq = wuq.reshape(Q_LORA_RANK, MLA_HEADS * LANES).astype(bf16)
    wukv = w_ukv.reshape(KV_LORA_RANK, MLA_HEADS, MLA_NOPE_DIM + MLA_V_DIM)
    wuk = jnp.pad(wukv[:, :, :MLA_NOPE_DIM], ((0, 0), (0, 0), (0, LANES - MLA_NOPE_DIM)))
    wuk = wuk.reshape(KV_LORA_RANK, MLA_HEADS * LANES).astype(bf16)
    wuvt = wukv[:, :, MLA_NOPE_DIM:].reshape(KV_LORA_RANK, MLA_WIDTH).T.astype(bf16)
    wa = w_out[:MLA_WIDTH].astype(bf16)
    wb = w_out[MLA_WIDTH:].astype(bf16)

    mla_tabs = _rope_tables(seq, MLA_ROPE_DIM, LANES, MLA_NOPE_DIM, pass_rest=False)
    dil_tabs = _rope_tables(seq, DIL_ROT_DIM, DIL_HEAD_DIM, 0, pass_rest=True)

    tm = PROJ_ROWS
    seq_tiles = seq // tm
    full = lambda shape: pl.BlockSpec(shape, lambda i: (0,) * len(shape))
    tab = pl.BlockSpec((tm, LANES), lambda i: (i % seq_tiles, 0))
    slab = lambda n: pl.BlockSpec((n, tm, LANES), lambda i: (0, i, 0))
    slab_shape = lambda n: jax.ShapeDtypeStruct((n, rows, LANES), bf16)
    tk = MLA_KEY_TILE
    vt_spec = pl.BlockSpec((tm // tk, MLA_WIDTH, tk), lambda i: (i, 0, 0))
    vt_shape = jax.ShapeDtypeStruct((rows // tk, MLA_WIDTH, tk), bf16)
    res_specs, res_shapes = [], []
    for dil in DILATIONS:
        spec = pl.BlockSpec((HEAD_PAIRS, 1, dil, tm // dil, LANES),
                            lambda i: (0, i // seq_tiles, 0, i % seq_tiles, 0))
        shape = jax.ShapeDtypeStruct((HEAD_PAIRS, batch, dil, seq // dil, LANES), bf16)
        res_specs += [spec] * 3
        res_shapes += [shape] * 3
    qm, km, vt, ga, gb, *dil_in = pl.pallas_call(
        _proj_kernel,
        grid=(rows // tm,),
        in_specs=[pl.BlockSpec((tm, D_MODEL), lambda i: (i, 0)),
                  full(w_lat.shape), full(w_kpe.shape), full(w_rest.shape),
                  full(wuq.shape), full(wuk.shape), full(wuvt.shape),
                  full((1, Q_LORA_RANK)), full((1, KV_LORA_RANK)),
                  tab, tab, tab, tab, tab, tab],
        out_specs=[slab(MLA_HEADS), slab(MLA_HEADS), vt_spec, slab(HEAD_PAIRS), slab(HEAD_PAIRS)] + res_specs,
        out_shape=[slab_shape(MLA_HEADS), slab_shape(MLA_HEADS), vt_shape, slab_shape(HEAD_PAIRS),
                   slab_shape(HEAD_PAIRS)] + res_shapes,
        scratch_shapes=[pltpu.VMEM((3 * HEAD_PAIRS, tm, LANES), f32),
                        pltpu.VMEM((3 * HEAD_PAIRS, 4, tm // 4, LANES), f32)],
        compiler_params=_params("parallel"),
        name="proj",
    )(x2, w_lat, w_kpe, w_rest, wuq, wuk, wuvt, q_norm_g.reshape(1, -1), kv_norm_g.reshape(1, -1),
      *mla_tabs, *dil_tabs)

    t = MLA_TILE
    ya = pl.pallas_call(
        _mla_kernel,
        grid=(batch, HEAD_PAIRS),
        in_specs=[pl.BlockSpec((2, seq, LANES), lambda b, p: (p, b, 0)),
                  pl.BlockSpec((2, seq, LANES), lambda b, p: (p, b, 0)),
                  pl.BlockSpec((seq // tk, LANES, tk), lambda b, p: (b, p, 0)),
                  pl.BlockSpec((1, seq, LANES), lambda b, p: (p, b, 0))],
        out_specs=pl.BlockSpec((seq, LANES), lambda b, p: (b, p)),
        out_shape=jax.ShapeDtypeStruct((rows, MLA_WIDTH), bf16),
        scratch_shapes=[pltpu.VMEM((2, 2, MLA_V_DIM + MLA_ONES_ROWS, t), f32), pltpu.VMEM((2, 2, tk, t), f32)],
        compiler_params=_params("parallel", "parallel"),
        name="mla",
    )(qm, km, vt, ga)

    dil_specs = [pl.BlockSpec((1, 1, dil, seq // dil, LANES), lambda b, p: (p, b, 0, 0, 0))
                 for dil in DILATIONS for _ in range(3)]
    yb = pl.pallas_call(
        _dilated_kernel,
        grid=(batch, HEAD_PAIRS),
        in_specs=dil_specs + [pl.BlockSpec((1, seq, LANES), lambda b, p: (p, b, 0))],
        out_specs=pl.BlockSpec((seq, LANES), lambda b, p: (b, p)),
        out_shape=jax.ShapeDtypeStruct((rows, DIL_WIDTH), bf16),
        scratch_shapes=[pltpu.VMEM((3, seq, LANES), f32)] * 3 + [
            pltpu.VMEM((3, SCATTER_STRIDE, seq // SCATTER_STRIDE, LANES), f32),
            pltpu.VMEM((2, DIL_GROUP, 2 * BLOCK, 2 * BLOCK), f32),
            pltpu.VMEM((2, DIL_GROUP, 2 * BLOCK, 2 * BLOCK), bf16),
            pltpu.VMEM((2, 2 * BLOCK, 2 * BLOCK), f32)],
        compiler_params=_params("parallel", "parallel"),
        name="dilated",
    )(*dil_in, gb)

    to = OUT_ROWS
    const = lambda shape: pl.BlockSpec(shape, lambda i: (0,) * len(shape))
    out = pl.pallas_call(
        _out_kernel,
        grid=(rows // to,),
        in_specs=[pl.BlockSpec((to, D_MODEL), lambda i: (i, 0)),
                  pl.BlockSpec((to, MLA_WIDTH), lambda i: (i, 0)),
                  pl.BlockSpec((to, DIL_WIDTH), lambda i: (i, 0)),
                  const((MLA_WIDTH, D_MODEL)), const((DIL_WIDTH, D_MODEL)),
                  const((1, D_MODEL)), const((1, D_MODEL))],
        out_specs=pl.BlockSpec((to, D_MODEL), lambda i: (i, 0)),
        out_shape=jax.ShapeDtypeStruct((rows, D_MODEL), f32),
        compiler_params=_params("parallel"),
        name="out",
    )(x2, ya, yb, wa, wb, ln_g.reshape(1, -1), ln_b.reshape(1, -1))
    return out.reshape(batch, seq, D_MODEL)
```

```python
import functools

import jax
import jax.numpy as jnp
import numpy as np
from jax import lax
from jax.experimental import pallas as pl
from jax.experimental.pallas import tpu as pltpu

D_MODEL = 1024
ROPE_THETA = 500000.0
BLOCK = 128
NEG = -1e30
RMS_EPS = 1e-6
LN_EPS = 1e-5

MLA_HEADS = 8
MLA_NOPE_DIM = 64
MLA_ROPE_DIM = 32
MLA_V_DIM = 64
Q_LORA_RANK = 384
KV_LORA_RANK = 256
MLA_WIDTH = MLA_HEADS * MLA_V_DIM

DIL_HEADS = 8
DIL_HEAD_DIM = 64
DIL_ROT_DIM = DIL_HEAD_DIM // 4
DIL_WIDTH = DIL_HEADS * DIL_HEAD_DIM
DILATIONS = (1, 4, 16)

DEPTH = 1
DEEPNORM_ALPHA = (2.0 * DEPTH) ** 0.25
LOG2_E = 1.4426950408889634

LANES = 128
HEAD_PAIRS = MLA_HEADS // 2
VMEM_LIMIT_BYTES = 56 * 1024 * 1024

_LATENT_WIDTH = Q_LORA_RANK + KV_LORA_RANK
_OFF_GA = 0
_OFF_QB = _OFF_GA + MLA_WIDTH
_OFF_KB = _OFF_QB + DIL_WIDTH
_OFF_VB = _OFF_KB + DIL_WIDTH
_OFF_GB = _OFF_VB + DIL_WIDTH
_REST_WIDTH = _OFF_GB + DIL_WIDTH

PROJ_ROWS = 512
MLA_KEY_TILE = 256
MLA_TILE = 1024
MLA_UNROLL_TRIPS = 2
MLA_ONES_ROWS = 16
OUT_ROWS = 2048
OUT_CHUNK = 256
DIL_GROUP = 4
SCATTER_STRIDE = 4


def _rope_lanes(x, cos, sin_fwd, sin_bwd, half):
    fwd = pltpu.roll(x, half, 1)
    bwd = pltpu.roll(x, LANES - half, 1)
    return x * cos + fwd * sin_fwd + bwd * sin_bwd


def _proj_kernel(x_ref, wlat_ref, wkpe_ref, w_ref, wuq_ref, wuk_ref, wuvt_ref, qg_ref, kvg_ref,
                 mcos_ref, msf_ref, msb_ref, dcos_ref, dsf_ref, dsb_ref,
                 qm_ref, km_ref, vt_ref, ga_ref, gb_ref, *dil_and_scratch):
    f32 = jnp.float32
    bf16 = jnp.bfloat16
    xb = x_ref[...].astype(bf16)
    dil_refs = [dil_and_scratch[3 * i:3 * i + 3] for i in range(len(DILATIONS))]
    stage_ref, stage4_ref = dil_and_scratch[-2:]
    tm = x_ref.shape[0]
    unit = 2 * LANES
    assert DILATIONS == (1, 4, 16)

    def emit_residues(which, p, val):
        slab = which * HEAD_PAIRS + p
        q1, q4, q16 = (refs[which] for refs in dil_refs)
        q1[p, 0, 0] = val.astype(bf16)
        stage_ref[slab] = val
        for a in range(4):
            rows4 = stage_ref[slab, pl.ds(a, tm // 4, stride=4), :]
            q4[p, 0, a] = rows4.astype(bf16)
            stage4_ref[slab, a] = rows4
        for a in range(4):
            for b in range(4):
                rows16 = stage4_ref[slab, a, pl.ds(b, tm // 16, stride=4), :]
                q16[p, 0, a + 4 * b] = rows16.astype(bf16)

    def seg(lo):
        return jnp.dot(xb, w_ref[:, lo:lo + unit], preferred_element_type=f32)

    def rms(t, g):
        return t * lax.rsqrt(jnp.mean(t * t, axis=-1, keepdims=True) + RMS_EPS) * g

    mcos, msf, msb = mcos_ref[...], msf_ref[...], msb_ref[...]
    dcos, dsf, dsb = dcos_ref[...], dsf_ref[...], dsb_ref[...]
    mla_scale = (MLA_NOPE_DIM + MLA_ROPE_DIM) ** -0.5 * LOG2_E
    dil_scale = DIL_HEAD_DIM ** -0.5 * LOG2_E

    latent = {}

    def slabs(t):
        return [t[:, j * LANES:(j + 1) * LANES] for j in range(t.shape[1] // LANES)]

    def dilated(which, lo, first_pair):
        def epilogue(t):
            for j, val in enumerate(slabs(t)):
                if which < 2:
                    val = _rope_lanes(val, dcos, dsf, dsb, DIL_ROT_DIM // 2)
                if which == 0:
                    val = val * dil_scale
                emit_residues(which, first_pair + j, val)
        return (lambda: seg(lo)), epilogue

    def gate(out_ref, lo, first_pair):
        def epilogue(t):
            for j, val in enumerate(slabs(jax.nn.silu(t))):
                out_ref[first_pair + j] = val.astype(bf16)
        return (lambda: seg(lo)), epilogue

    def mla_q(first_head):
        def epilogue(t):
            for j, val in enumerate(slabs(t)):
                val = _rope_lanes(val, mcos, msf, msb, MLA_ROPE_DIM // 2) * mla_scale
                qm_ref[first_head + j] = val.astype(bf16)
        lo = first_head * LANES
        return (lambda: jnp.dot(latent["cq"], wuq_ref[:, lo:lo + unit], preferred_element_type=f32)), epilogue

    def mla_k(first_head):
        def epilogue(t):
            for j, val in enumerate(slabs(t)):
                km_ref[first_head + j] = (val + latent["kpe"]).astype(bf16)
        lo = first_head * LANES
        return (lambda: jnp.dot(latent["ckv"], wuk_ref[:, lo:lo + unit], preferred_element_type=f32)), epilogue

    def store_vt(t):
        for c in range(vt_ref.shape[0]):
            vt_ref[c] = t[:, c * MLA_KEY_TILE:(c + 1) * MLA_KEY_TILE].astype(bf16)

    split = unit
    units = [
        (lambda: jnp.dot(xb, wlat_ref[:, :split], preferred_element_type=f32),
         lambda t: latent.update(cq_head=t)),
        (lambda: jnp.dot(xb, wlat_ref[:, split:Q_LORA_RANK], preferred_element_type=f32),
         lambda t: latent.update(cq=rms(jnp.concatenate([latent["cq_head"], t], axis=1), qg_ref[...]).astype(bf16))),
        (lambda: jnp.dot(xb, wlat_ref[:, Q_LORA_RANK:], preferred_element_type=f32),
         lambda t: latent.update(ckv=rms(t, kvg_ref[...]).astype(bf16))),
        dilated(0, _OFF_QB, 0), mla_q(0), dilated(0, _OFF_QB + unit, 2), mla_q(2),
        dilated(1, _OFF_KB, 0), mla_q(4), dilated(1, _OFF_KB + unit, 2), mla_q(6),
        (lambda: jnp.dot(xb, wkpe_ref[...], preferred_element_type=f32),
         lambda t: latent.update(kpe=_rope_lanes(t, mcos, msf, msb, MLA_ROPE_DIM // 2))),
        gate(ga_ref, _OFF_GA, 0), mla_k(0), dilated(2, _OFF_VB, 0), mla_k(2),
        gate(ga_ref, _OFF_GA + unit, 2), mla_k(4), dilated(2, _OFF_VB + unit, 2), mla_k(6),
        gate(gb_ref, _OFF_GB, 0), gate(gb_ref, _OFF_GB + unit, 2),
        (lambda: lax.dot_general(wuvt_ref[...], latent["ckv"], (((1,), (1,)), ((), ())),
                                 preferred_element_type=f32), store_vt),
    ]
    pending = None
    for matmul, epilogue in units:
        result = matmul()
        if pending is not None:
            pending[0](pending[1])
        pending = (epilogue, result)
    pending[0](pending[1])


def _mla_kernel(q_ref, k_ref, vt_ref, g_ref, o_ref, accs_ref, s_ref):
    f32 = jnp.float32
    bf16 = jnp.bfloat16
    tq, tk = MLA_TILE, MLA_KEY_TILE
    n_chunks = tq // tk
    assert n_chunks % 2 == 0
    nq = q_ref.shape[1] // tq
    units = [(h, c) for h in range(2) for c in range(n_chunks)]
    ones = jnp.ones((MLA_ONES_ROWS, tk), bf16)
    key = lax.broadcasted_iota(jnp.int32, (tk, tk), 0)
    qry = lax.broadcasted_iota(jnp.int32, (tk, tk), 1)
    lower = key <= qry

    def lanes(c):
        return slice(c * tk, (c + 1) * tk)

    def scores(qi, j, slot, h, c):
        q = q_ref[h, pl.ds(qi * tq + c * tk, tk), :]
        k = k_ref[h, pl.ds(j * tk if isinstance(j, int) else pl.multiple_of(j * tk, tk), tk), :]
        s = lax.dot_general(k, q, (((1,), (1,)), ((), ())), preferred_element_type=f32)
        s_ref[slot, h, :, lanes(c)] = s
        return jnp.max(s, axis=0, keepdims=True)

    def absorb(acc_ref, j, slot, h, c, m_tile, m_old, masked=False):
        s = s_ref[slot, h, :, lanes(c)]
        if masked:
            s = jnp.where(lower, s, NEG)
            m_tile = jnp.max(s, axis=0, keepdims=True)
        m_new = jnp.maximum(m_old, m_tile)
        alpha = jnp.exp2(m_old - m_new)
        p = jnp.exp2(s - m_new).astype(bf16)
        vt = jnp.concatenate([vt_ref[j, h * MLA_V_DIM:(h + 1) * MLA_V_DIM, :], ones], axis=0)
        acc_ref[h, :, lanes(c)] = alpha * acc_ref[h, :, lanes(c)] + jnp.dot(vt, p, preferred_element_type=f32)
        return m_new

    def per_query_tile(qi, acc_ref):
        rows = pl.ds(qi * tq, tq)
        acc_ref[...] = jnp.zeros_like(acc_ref)

        def step(j, slot, tile_max, m_run):
            next_max, m_new = {}, {}
            for u in units:
                next_max[u] = scores(qi, j + 1, 1 - slot, *u)
                m_new[u] = absorb(acc_ref, j, slot, *u, tile_max[u], m_run[u])
            return next_max, m_new

        def body(jj, carry):
            tile_max, m_run = carry
            for i in range(n_chunks):
                tile_max, m_run = step(n_chunks * jj + i, i % 2, tile_max, m_run)
            return tile_max, m_run

        start = {u: jnp.full((1, tk), NEG, f32) for u in units}
        first = {u: scores(qi, 0, 0, *u) for u in units}
        tile_max, m_run = lax.fori_loop(0, qi, body, (first, start), unroll=qi <= MLA_UNROLL_TRIPS)

        base = qi * n_chunks
        for d in range(n_chunks):
            next_max = {}
            for h, c in units:
                if c > d:
                    next_max[h, c] = scores(qi, base + d + 1, (d + 1) % 2, h, c)
                if c >= d:
                    m_run[h, c] = absorb(acc_ref, base + d, d % 2, h, c, tile_max[h, c], m_run[h, c],
                                         masked=(c == d))
            tile_max = next_max

        yt = jnp.concatenate([acc_ref[h, :MLA_V_DIM, :] / acc_ref[h, MLA_V_DIM:MLA_V_DIM + 1, :] for h in range(2)],
                             axis=0)
        o_ref[rows, :] = (yt.T * g_ref[0, rows, :].astype(f32)).astype(o_ref.dtype)

    for qi in range(nq):
        per_query_tile(qi, accs_ref.at[qi % 2])


def _dilated_kernel(q1_ref, k1_ref, v1_ref, q4_ref, k4_ref, v4_ref, q16_ref, k16_ref, v16_ref,
                    g_ref, o_ref, num_ref, max_ref, den_ref, hop_ref, s_ref, p_ref, bias_ref):
    f32 = jnp.float32
    seq = num_ref.shape[1]

    lane = lax.broadcasted_iota(jnp.int32, (BLOCK, LANES), 1)
    low_half = lane < DIL_HEAD_DIM
    ones = jnp.ones((2 * BLOCK, LANES), jnp.bfloat16)
    qi = lax.broadcasted_iota(jnp.int32, (2 * BLOCK, 2 * BLOCK), 0) % BLOCK
    kj = lax.broadcasted_iota(jnp.int32, (2 * BLOCK, 2 * BLOCK), 1)
    dist_first = qi - kj
    dist_later = dist_first + BLOCK
    for kind, dist in enumerate((dist_first, dist_later)):
        bias_ref[kind] = jnp.where((dist >= 0) & (dist <= BLOCK), 0.0, NEG).astype(f32)

    n_groups = seq // BLOCK // DIL_GROUP
    assert n_groups % 2 == 0 and n_groups >= 4

    def branch(idx, dil, q_ref, k_ref, v_ref):
        per_residue = q_ref.shape[3] // BLOCK

        def locate(b):
            return b // per_residue, b % per_residue

        def key_rows(n):
            return pl.ds(pl.multiple_of(jnp.maximum(n - 1, 0) * BLOCK, BLOCK), 2 * BLOCK)

        def out_rows(r, n):
            if dil == 1:
                return pl.ds(pl.multiple_of(n * BLOCK, BLOCK), BLOCK)
            return pl.ds(r + n * (BLOCK * dil), BLOCK, stride=dil)

        two_hops = dil > SCATTER_STRIDE
        inner = dil // SCATTER_STRIDE

        def scatter(kind, stat_ref, r, n, val):
            if two_hops:
                rows = pl.ds(r // SCATTER_STRIDE + n * (BLOCK * inner), BLOCK, stride=inner)
                hop_ref[kind, r % SCATTER_STRIDE, rows, :] = val
            else:
                stat_ref[idx, out_rows(r, n), :] = val

        def finish():
            if not two_hops:
                return
            for kind, stat_ref in enumerate((num_ref, max_ref, den_ref)):
                for a in range(SCATTER_STRIDE):
                    for c in range(hop_ref.shape[2] // BLOCK):
                        rows = pl.ds(a + c * (BLOCK * SCATTER_STRIDE), BLOCK, stride=SCATTER_STRIDE)
                        stat_ref[idx, rows, :] = hop_ref[kind, a, c * BLOCK:(c + 1) * BLOCK, :]

        def scores(g, slot, i):
            r, n = locate(g * DIL_GROUP + i)
            q = q_ref[0, 0, r, pl.ds(pl.multiple_of(n * BLOCK, BLOCK), BLOCK), :]
            k = k_ref[0, 0, r, key_rows(n), :]
            zero = jnp.zeros_like(q)
            q2 = jnp.concatenate([jnp.where(low_half, q, zero), jnp.where(low_half, zero, q)], axis=0)
            s_ref[slot, i] = lax.dot_general(q2, k, (((1,), (1,)), ((), ())), preferred_element_type=f32)

        def softmax(g, slot, i):
            r, n = locate(g * DIL_GROUP + i)
            s = s_ref[slot, i] + bias_ref[jnp.minimum(n, 1)]
            m = jnp.max(s, axis=1, keepdims=True)
            p_ref[slot, i] = jnp.exp2(s - m).astype(jnp.bfloat16)
            scatter(1, max_ref, r, n, jnp.where(low_half, m[:BLOCK], m[BLOCK:]))

        def values(g, slot, i):
            r, n = locate(g * DIL_GROUP + i)
            v = jnp.concatenate([v_ref[0, 0, r, key_rows(n), :], ones], axis=1)
            both = jnp.dot(p_ref[slot, i], v, preferred_element_type=f32)
            num, den = both[:, :LANES], both[:, LANES:]
            num = jnp.where(low_half, num[:BLOCK], num[BLOCK:])
            den = jnp.where(low_half, den[:BLOCK], den[BLOCK:])
            if dil != 1:
                scatter(0, num_ref, r, n, num)
                scatter(2, den_ref, r, n, den)
                return
            rows = out_rows(r, n)
            others = [i for i in range(len(DILATIONS)) if i != idx]
            ms = [max_ref[idx, rows, :]] + [max_ref[i, rows, :] for i in others]
            m_all = functools.reduce(jnp.maximum, ms)
            w = jnp.exp2(ms[0] - m_all)
            num, den = w * num, w * den
            for m, i in zip(ms[1:], others):
                w = jnp.exp2(m - m_all)
                num = num + w * num_ref[i, rows, :]
                den = den + w * den_ref[i, rows, :]
            o_ref[rows, :] = (num / den * g_ref[0, rows, :].astype(f32)).astype(o_ref.dtype)

        return scores, softmax, values, finish

    assert DILATIONS[0] == 1
    stages = [branch(2, DILATIONS[2], q16_ref, k16_ref, v16_ref),
              branch(1, DILATIONS[1], q4_ref, k4_ref, v4_ref),
              branch(0, DILATIONS[0], q1_ref, k1_ref, v1_ref)]
    last = n_groups - 1

    def trip(*work):
        for stage, g, slot in work:
            for i in range(DIL_GROUP):
                stage(g, slot, i)

    for b, (scores, softmax, values, _) in enumerate(stages):
        if b == 0:
            trip((scores, 0, 0))
            trip((scores, 1, 1), (softmax, 0, 0))
        else:
            trip((stages[b - 1][2], last, 1), (scores, 1, 1), (softmax, 0, 0))
            stages[b - 1][3]()

        for t in range((n_groups - 2) // 2):
            trip((values, 2 * t, 0), (scores, 2 * t + 2, 0), (softmax, 2 * t + 1, 1))
            trip((values, 2 * t + 1, 1), (scores, 2 * t + 3, 1), (softmax, 2 * t + 2, 0))
        if b + 1 < len(stages):
            trip((values, last - 1, 0), (stages[b + 1][0], 0, 0), (softmax, last, 1))
        else:
            trip((values, last - 1, 0), (softmax, last, 1))
    trip((stages[-1][2], last, 1))


def _out_kernel(x_ref, ya_ref, yb_ref, wa_ref, wb_ref, g_ref, b_ref, o_ref):
    f32 = jnp.float32
    chunks = [pl.ds(r, OUT_CHUNK) for r in range(0, x_ref.shape[0], OUT_CHUNK)]

    def project(rows):
        return (jnp.dot(ya_ref[rows, :], wa_ref[...], preferred_element_type=f32)
                + jnp.dot(yb_ref[rows, :], wb_ref[...], preferred_element_type=f32))

    def normalise(rows, mixed):
        h = DEEPNORM_ALPHA * x_ref[rows, :] + mixed
        mu = jnp.mean(h, axis=-1, keepdims=True)
        c = h - mu
        var = jnp.mean(c * c, axis=-1, keepdims=True)
        o_ref[rows, :] = c * lax.rsqrt(var + LN_EPS) * g_ref[...] + b_ref[...]

    mixed = project(chunks[0])
    for i, rows in enumerate(chunks):
        ahead = project(chunks[i + 1]) if i + 1 < len(chunks) else None
        normalise(rows, mixed)
        mixed = ahead


def _rope_tables(seq, rot_dim, period, rot_offset, pass_rest):
    half = rot_dim // 2
    f32 = np.float32
    inv_freq = ROPE_THETA ** (-np.arange(0, rot_dim, 2, dtype=np.float64) / rot_dim)
    ang = np.arange(seq, dtype=np.float64)[:, None] * inv_freq[None, :]
    cos, sin = np.cos(ang).astype(f32), np.sin(ang).astype(f32)
    zeros = np.zeros((seq, half), f32)
    rest = period - rot_offset - rot_dim
    fill = np.ones if pass_rest else np.zeros
    group_cos = np.concatenate([np.ones((seq, rot_offset), f32), cos, cos, fill((seq, rest), f32)], axis=1)
    group_fwd = np.concatenate([np.zeros((seq, rot_offset), f32), zeros, sin, np.zeros((seq, rest), f32)], axis=1)
    group_bwd = np.concatenate([np.zeros((seq, rot_offset), f32), -sin, zeros, np.zeros((seq, rest), f32)], axis=1)
    reps = LANES // period
    return tuple(jnp.asarray(np.tile(t, (1, reps))) for t in (group_cos, group_fwd, group_bwd))


def _params(*semantics):
    return pltpu.CompilerParams(dimension_semantics=semantics, vmem_limit_bytes=VMEM_LIMIT_BYTES)


def kernel(x, w_in, q_norm_g, kv_norm_g, w_uq, w_ukv, w_out, ln_g, ln_b):
    f32, bf16 = jnp.float32, jnp.bfloat16
    batch, seq, _ = x.shape
    rows = batch * seq
    x2 = x.reshape(rows, D_MODEL)

    rope_end = _LATENT_WIDTH + MLA_ROPE_DIM
    w_lat = w_in[:, :_LATENT_WIDTH].astype(bf16)
    w_kpe = jnp.pad(w_in[:, _LATENT_WIDTH:rope_end].astype(bf16),
                    ((0, 0), (MLA_NOPE_DIM, LANES - MLA_NOPE_DIM - MLA_ROPE_DIM)))
    w_rest = w_in[:, rope_end:].astype(bf16)
    assert w_rest.shape[1] == _REST_WIDTH
    dk = MLA_NOPE_DIM + MLA_ROPE_DIM
    wuq = jnp.pad(w_uq.reshape(Q_LORA_RANK, MLA_HEADS, dk), ((0, 0), (0, 0), (0, LANES - dk)))
    wuq = wuq.reshape(Q_LORA_RANK, MLA_HEADS * LANES).astype(bf16)
    wukv = w_ukv.reshape(KV_LORA_RANK, MLA_HEADS, MLA_NOPE_DIM + MLA_V_DIM)
    wuk = jnp.pad(wukv[:, :, :MLA_NOPE_DIM], ((0, 0), (0, 0), (0, LANES - MLA_NOPE_DIM)))
    wuk = wuk.reshape(KV_LORA_RANK, MLA_HEADS * LANES).astype(bf16)
    wuvt = wukv[:, :, MLA_NOPE_DIM:].reshape(KV_LORA_RANK, MLA_WIDTH).T.astype(bf16)
    wa = w_out[:MLA_WIDTH].astype(bf16)
    wb = w_out[MLA_WIDTH:].astype(bf16)

    mla_tabs = _rope_tables(seq, MLA_ROPE_DIM, LANES, MLA_NOPE_DIM, pass_rest=False)
    dil_tabs = _rope_tables(seq, DIL_ROT_DIM, DIL_HEAD_DIM, 0, pass_rest=True)

    tm = PROJ_ROWS
    seq_tiles = seq // tm
    full = lambda shape: pl.BlockSpec(shape, lambda i: (0,) * len(shape))
    tab = pl.BlockSpec((tm, LANES), lambda i: (i % seq_tiles, 0))
    slab = lambda n: pl.BlockSpec((n, tm, LANES), lambda i: (0, i, 0))
    slab_shape = lambda n: jax.ShapeDtypeStruct((n, rows, LANES), bf16)
    tk = MLA_KEY_TILE
    vt_spec = pl.BlockSpec((tm // tk, MLA_WIDTH, tk), lambda i: (i, 0, 0))
    vt_shape = jax.ShapeDtypeStruct((rows // tk, MLA_WIDTH, tk), bf16)
    res_specs, res_shapes = [], []
    for dil in DILATIONS:
        spec = pl.BlockSpec((HEAD_PAIRS, 1, dil, tm // dil, LANES),
                            lambda i: (0, i // seq_tiles, 0, i % seq_tiles, 0))
        shape = jax.ShapeDtypeStruct((HEAD_PAIRS, batch, dil, seq // dil, LANES), bf16)
        res_specs += [spec] * 3
        res_shapes += [shape] * 3
    qm, km, vt, ga, gb, *dil_in = pl.pallas_call(
        _proj_kernel,
        grid=(rows // tm,),
        in_specs=[pl.BlockSpec((tm, D_MODEL), lambda i: (i, 0)),
                  full(w_lat.shape), full(w_kpe.shape), full(w_rest.shape),
                  full(wuq.shape), full(wuk.shape), full(wuvt.shape),
                  full((1, Q_LORA_RANK)), full((1, KV_LORA_RANK)),
                  tab, tab, tab, tab, tab, tab],
        out_specs=[slab(MLA_HEADS), slab(MLA_HEADS), vt_spec, slab(HEAD_PAIRS), slab(HEAD_PAIRS)] + res_specs,
        out_shape=[slab_shape(MLA_HEADS), slab_shape(MLA_HEADS), vt_shape, slab_shape(HEAD_PAIRS),
                   slab_shape(HEAD_PAIRS)] + res_shapes,
        scratch_shapes=[pltpu.VMEM((3 * HEAD_PAIRS, tm, LANES), f32),
                        pltpu.VMEM((3 * HEAD_PAIRS, 4, tm // 4, LANES), f32)],
        compiler_params=_params("parallel"),
        name="proj",
    )(x2, w_lat, w_kpe, w_rest, wuq, wuk, wuvt, q_norm_g.reshape(1, -1), kv_norm_g.reshape(1, -1),
      *mla_tabs, *dil_tabs)

    t = MLA_TILE
    ya = pl.pallas_call(
        _mla_kernel,
        grid=(batch, HEAD_PAIRS),
        in_specs=[pl.BlockSpec((2, seq, LANES), lambda b, p: (p, b, 0)),
                  pl.BlockSpec((2, seq, LANES), lambda b, p: (p, b, 0)),
                  pl.BlockSpec((seq // tk, LANES, tk), lambda b, p: (b, p, 0)),
                  pl.BlockSpec((1, seq, LANES), lambda b, p: (p, b, 0))],
        out_specs=pl.BlockSpec((seq, LANES), lambda b, p: (b, p)),
        out_shape=jax.ShapeDtypeStruct((rows, MLA_WIDTH), bf16),
        scratch_shapes=[pltpu.VMEM((2, 2, MLA_V_DIM + MLA_ONES_ROWS, t), f32), pltpu.VMEM((2, 2, tk, t), f32)],
        compiler_params=_params("parallel", "parallel"),
        name="mla",
    )(qm, km, vt, ga)

    dil_specs = [pl.BlockSpec((1, 1, dil, seq // dil, LANES), lambda b, p: (p, b, 0, 0, 0))
                 for dil in DILATIONS for _ in range(3)]
    yb = pl.pallas_call(
        _dilated_kernel,
        grid=(batch, HEAD_PAIRS),
        in_specs=dil_specs + [pl.BlockSpec((1, seq, LANES), lambda b, p: (p, b, 0))],
        out_specs=pl.BlockSpec((seq, LANES), lambda b, p: (b, p)),
        out_shape=jax.ShapeDtypeStruct((rows, DIL_WIDTH), bf16),
        scratch_shapes=[pltpu.VMEM((3, seq, LANES), f32)] * 3 + [
            pltpu.VMEM((3, SCATTER_STRIDE, seq // SCATTER_STRIDE, LANES), f32),
            pltpu.VMEM((2, DIL_GROUP, 2 * BLOCK, 2 * BLOCK), f32),
            pltpu.VMEM((2, DIL_GROUP, 2 * BLOCK, 2 * BLOCK), bf16),
            pltpu.VMEM((2, 2 * BLOCK, 2 * BLOCK), f32)],
        compiler_params=_params("parallel", "parallel"),
        name="dilated",
    )(*dil_in, gb)

    to = OUT_ROWS
    const = lambda shape: pl.BlockSpec(shape, lambda i: (0,) * len(shape))
    out = pl.pallas_call(
        _out_kernel,
        grid=(rows // to,),
        in_specs=[pl.BlockSpec((to, D_MODEL), lambda i: (i, 0)),
                  pl.BlockSpec((to, MLA_WIDTH), lambda i: (i, 0)),
                  pl.BlockSpec((to, DIL_WIDTH), lambda i: (i, 0)),
                  const((MLA_WIDTH, D_MODEL)), const((DIL_WIDTH, D_MODEL)),
                  const((1, D_MODEL)), const((1, D_MODEL))],
        out_specs=pl.BlockSpec((to, D_MODEL), lambda i: (i, 0)),
        out_shape=jax.ShapeDtypeStruct((rows, D_MODEL), f32),
        compiler_params=_params("parallel"),
        name="out",
    )(x2, ya, yb, wa, wb, ln_g.reshape(1, -1), ln_b.reshape(1, -1))
    return out.reshape(batch, seq, D_MODEL)
```

```python
import functools

import jax
import jax.numpy as jnp
import numpy as np
from jax import lax
from jax.experimental import pallas as pl
from jax.experimental.pallas import tpu as pltpu

D_MODEL = 1024
ROPE_THETA = 500000.0
BLOCK = 128
NEG = -1e30
RMS_EPS = 1e-6
LN_EPS = 1e-5

MLA_HEADS = 8
MLA_NOPE_DIM = 64
MLA_ROPE_DIM = 32
MLA_V_DIM = 64
Q_LORA_RANK = 384
KV_LORA_RANK = 256
MLA_WIDTH = MLA_HEADS * MLA_V_DIM

DIL_HEADS = 8
DIL_HEAD_DIM = 64
DIL_ROT_DIM = DIL_HEAD_DIM // 4
DIL_WIDTH = DIL_HEADS * DIL_HEAD_DIM
DILATIONS = (1, 4, 16)

DEPTH = 1
DEEPNORM_ALPHA = (2.0 * DEPTH) ** 0.25
LOG2_E = 1.4426950408889634

LANES = 128
HEAD_PAIRS = MLA_HEADS // 2
VMEM_LIMIT_BYTES = 56 * 1024 * 1024

_LATENT_WIDTH = Q_LORA_RANK + KV_LORA_RANK
_OFF_GA = 0
_OFF_QB = _OFF_GA + MLA_WIDTH
_OFF_KB = _OFF_QB + DIL_WIDTH
_OFF_VB = _OFF_KB + DIL_WIDTH
_OFF_GB = _OFF_VB + DIL_WIDTH
_REST_WIDTH = _OFF_GB + DIL_WIDTH

PROJ_ROWS = 512
MLA_KEY_TILE = 256
MLA_TILE = 1024
MLA_CHUNK_ORDER = (0, 2, 1, 3)
MLA_UNROLL_TRIPS = 2
MLA_ONES_ROWS = 16
OUT_ROWS = 2048
OUT_CHUNK = 256
DIL_GROUP = 4
SCATTER_STRIDE = 4


def _rope_lanes(x, cos, sin_fwd, sin_bwd, half):
    fwd = pltpu.roll(x, half, 1)
    bwd = pltpu.roll(x, LANES - half, 1)
    return x * cos + fwd * sin_fwd + bwd * sin_bwd


def _proj_kernel(x_ref, wlat_ref, wkpe_ref, w_ref, wuq_ref, wuk_ref, wuvt_ref, qg_ref, kvg_ref,
                 mcos_ref, msf_ref, msb_ref, dcos_ref, dsf_ref, dsb_ref,
                 qm_ref, km_ref, vt_ref, ga_ref, gb_ref, *dil_and_scratch):
    f32 = jnp.float32
    bf16 = jnp.bfloat16
    xb = x_ref[...].astype(bf16)
    dil_refs = [dil_and_scratch[3 * i:3 * i + 3] for i in range(len(DILATIONS))]
    stage_ref, stage4_ref = dil_and_scratch[-2:]
    tm = x_ref.shape[0]
    unit = 2 * LANES
    assert DILATIONS == (1, 4, 16)

    def emit_residues(which, p, val):
        slab = which * HEAD_PAIRS + p
        q1, q4, q16 = (refs[which] for refs in dil_refs)
        q1[p, 0, 0] = val.astype(bf16)
        stage_ref[slab] = val
        for a in range(4):
            rows4 = stage_ref[slab, pl.ds(a, tm // 4, stride=4), :]
            q4[p, 0, a] = rows4.astype(bf16)
            stage4_ref[slab, a] = rows4
        for a in range(4):
            for b in range(4):
                rows16 = stage4_ref[slab, a, pl.ds(b, tm // 16, stride=4), :]
                q16[p, 0, a + 4 * b] = rows16.astype(bf16)

    def seg(lo):
        return jnp.dot(xb, w_ref[:, lo:lo + unit], preferred_element_type=f32)

    def rms(t, g):
        return t * lax.rsqrt(jnp.mean(t * t, axis=-1, keepdims=True) + RMS_EPS) * g

    mcos, msf, msb = mcos_ref[...], msf_ref[...], msb_ref[...]
    dcos, dsf, dsb = dcos_ref[...], dsf_ref[...], dsb_ref[...]
    mla_scale = (MLA_NOPE_DIM + MLA_ROPE_DIM) ** -0.5 * LOG2_E
    dil_scale = DIL_HEAD_DIM ** -0.5 * LOG2_E

    latent = {}

    def slabs(t):
        return [t[:, j * LANES:(j + 1) * LANES] for j in range(t.shape[1] // LANES)]

    def dilated(which, lo, first_pair):
        def epilogue(t):
            for j, val in enumerate(slabs(t)):
                if which < 2:
                    val = _rope_lanes(val, dcos, dsf, dsb, DIL_ROT_DIM // 2)
                if which == 0:
                    val = val * dil_scale
                emit_residues(which, first_pair + j, val)
        return (lambda: seg(lo)), epilogue

    def gate(out_ref, lo, first_pair):
        def epilogue(t):
            for j, val in enumerate(slabs(jax.nn.silu(t))):
                out_ref[first_pair + j] = val.astype(bf16)
        return (lambda: seg(lo)), epilogue

    def mla_q(first_head):
        def epilogue(t):
            for j, val in enumerate(slabs(t)):
                val = _rope_lanes(val, mcos, msf, msb, MLA_ROPE_DIM // 2) * mla_scale
                qm_ref[first_head + j] = val.astype(bf16)
        lo = first_head * LANES
        return (lambda: jnp.dot(latent["cq"], wuq_ref[:, lo:lo + unit], preferred_element_type=f32)), epilogue

    def mla_k(first_head):
        def epilogue(t):
            for j, val in enumerate(slabs(t)):
                km_ref[first_head + j] = (val + latent["kpe"]).astype(bf16)
        lo = first_head * LANES
        return (lambda: jnp.dot(latent["ckv"], wuk_ref[:, lo:lo + unit], preferred_element_type=f32)), epilogue

    def store_vt(t):
        for c in range(vt_ref.shape[0]):
            vt_ref[c] = t[:, c * MLA_KEY_TILE:(c + 1) * MLA_KEY_TILE].astype(bf16)

    split = unit
    units = [
        (lambda: jnp.dot(xb, wlat_ref[:, :split], preferred_element_type=f32),
         lambda t: latent.update(cq_head=t)),
        (lambda: jnp.dot(xb, wlat_ref[:, split:Q_LORA_RANK], preferred_element_type=f32),
         lambda t: latent.update(cq=rms(jnp.concatenate([latent["cq_head"], t], axis=1), qg_ref[...]).astype(bf16))),
        (lambda: jnp.dot(xb, wlat_ref[:, Q_LORA_RANK:], preferred_element_type=f32),
         lambda t: latent.update(ckv=rms(t, kvg_ref[...]).astype(bf16))),
        dilated(0, _OFF_QB, 0), mla_q(0), gate(ga_ref, _OFF_GA, 0), dilated(0, _OFF_QB + unit, 2), mla_q(2),
        gate(ga_ref, _OFF_GA + unit, 2), dilated(1, _OFF_KB, 0), mla_q(4), gate(gb_ref, _OFF_GB, 0),
        dilated(1, _OFF_KB + unit, 2), mla_q(6), gate(gb_ref, _OFF_GB + unit, 2),
        (lambda: jnp.dot(xb, wkpe_ref[...], preferred_element_type=f32),
         lambda t: latent.update(kpe=_rope_lanes(t, mcos, msf, msb, MLA_ROPE_DIM // 2))),
        mla_k(0), dilated(2, _OFF_VB, 0), mla_k(2), mla_k(4), dilated(2, _OFF_VB + unit, 2), mla_k(6),
        (lambda: lax.dot_general(wuvt_ref[...], latent["ckv"], (((1,), (1,)), ((), ())),
                                 preferred_element_type=f32), store_vt),
    ]
    pending = None
    for matmul, epilogue in units:
        result = matmul()
        if pending is not None:
            pending[0](pending[1])
        pending = (epilogue, result)
    pending[0](pending[1])


def _mla_kernel(q_ref, k_ref, vt_ref, g_ref, o_ref, accs_ref, s_ref):
    f32 = jnp.float32
    bf16 = jnp.bfloat16
    tq, tk = MLA_TILE, MLA_KEY_TILE
    n_chunks = tq // tk
    assert n_chunks % 2 == 0
    nq = q_ref.shape[1] // tq
    assert sorted(MLA_CHUNK_ORDER) == list(range(n_chunks))
    units = [(h, c) for c in MLA_CHUNK_ORDER for h in range(2)]
    ones = jnp.ones((MLA_ONES_ROWS, tk), bf16)
    key = lax.broadcasted_iota(jnp.int32, (tk, tk), 0)
    qry = lax.broadcasted_iota(jnp.int32, (tk, tk), 1)
    lower = key <= qry

    def lanes(c):
        return slice(c * tk, (c + 1) * tk)

    def scores(qi, j, slot, h, c):
        q = q_ref[h, pl.ds(qi * tq + c * tk, tk), :]
        k = k_ref[h, pl.ds(j * tk if isinstance(j, int) else pl.multiple_of(j * tk, tk), tk), :]
        s = lax.dot_general(k, q, (((1,), (1,)), ((), ())), preferred_element_type=f32)
        s_ref[slot, h, :, lanes(c)] = s
        return jnp.max(s, axis=0, keepdims=True)

    def absorb(acc_ref, j, slot, h, c, m_tile, m_old, masked=False):
        s = s_ref[slot, h, :, lanes(c)]
        if masked:
            s = jnp.where(lower, s, NEG)
            m_tile = jnp.max(s, axis=0, keepdims=True)
        m_new = jnp.maximum(m_old, m_tile)
        alpha = jnp.exp2(m_old - m_new)
        p = jnp.exp2(s - m_new).astype(bf16)
        vt = jnp.concatenate([vt_ref[j, h * MLA_V_DIM:(h + 1) * MLA_V_DIM, :], ones], axis=0)
        acc_ref[h, :, lanes(c)] = alpha * acc_ref[h, :, lanes(c)] + jnp.dot(vt, p, preferred_element_type=f32)
        return m_new

    def per_query_tile(qi, acc_ref):
        rows = pl.ds(qi * tq, tq)
        acc_ref[...] = jnp.zeros_like(acc_ref)

        def step(j, slot, tile_max, m_run):
            next_max, m_new = {}, {}
            for u in units:
                next_max[u] = scores(qi, j + 1, 1 - slot, *u)
                m_new[u] = absorb(acc_ref, j, slot, *u, tile_max[u], m_run[u])
            return next_max, m_new

        def body(jj, carry):
            tile_max, m_run = carry
            for i in range(n_chunks):
                tile_max, m_run = step(n_chunks * jj + i, i % 2, tile_max, m_run)
            return tile_max, m_run

        start = {u: jnp.full((1, tk), NEG, f32) for u in units}
        first = {u: scores(qi, 0, 0, *u) for u in units}
        tile_max, m_run = lax.fori_loop(0, qi, body, (first, start), unroll=qi <= MLA_UNROLL_TRIPS)

        base = qi * n_chunks
        for d in range(n_chunks):
            next_max = {}
            for h, c in units:
                if c > d:
                    next_max[h, c] = scores(qi, base + d + 1, (d + 1) % 2, h, c)
                if c >= d:
                    m_run[h, c] = absorb(acc_ref, base + d, d % 2, h, c, tile_max[h, c], m_run[h, c],
                                         masked=(c == d))
            tile_max = next_max

        yt = jnp.concatenate([acc_ref[h, :MLA_V_DIM, :] / acc_ref[h, MLA_V_DIM:MLA_V_DIM + 1, :] for h in range(2)],
                             axis=0)
        o_ref[rows, :] = (yt.T * g_ref[0, rows, :].astype(f32)).astype(o_ref.dtype)

    for qi in reversed(range(nq)):
        per_query_tile(qi, accs_ref.at[qi % 2])


def _dilated_kernel(q1_ref, k1_ref, v1_ref, q4_ref, k4_ref, v4_ref, q16_ref, k16_ref, v16_ref,
                    g_ref, o_ref, num_ref, max_ref, den_ref, hop_ref, s_ref, p_ref, bias_ref):
    f32 = jnp.float32
    seq = num_ref.shape[1]

    lane = lax.broadcasted_iota(jnp.int32, (BLOCK, LANES), 1)
    low_half = lane < DIL_HEAD_DIM
    ones = jnp.ones((2 * BLOCK, LANES), jnp.bfloat16)
    qi = lax.broadcasted_iota(jnp.int32, (2 * BLOCK, 2 * BLOCK), 0) % BLOCK
    kj = lax.broadcasted_iota(jnp.int32, (2 * BLOCK, 2 * BLOCK), 1)
    dist_first = qi - kj
    dist_later = dist_first + BLOCK
    for kind, dist in enumerate((dist_first, dist_later)):
        bias_ref[kind] = jnp.where((dist >= 0) & (dist <= BLOCK), 0.0, NEG).astype(f32)

    n_groups = seq // BLOCK // DIL_GROUP
    assert n_groups % 2 == 0 and n_groups >= 4

    def branch(idx, dil, q_ref, k_ref, v_ref):
        per_residue = q_ref.shape[3] // BLOCK

        def locate(b):
            return b // per_residue, b % per_residue

        def key_rows(n):
            return pl.ds(pl.multiple_of(jnp.maximum(n - 1, 0) * BLOCK, BLOCK), 2 * BLOCK)

        def out_rows(r, n):
            if dil == 1:
                return pl.ds(pl.multiple_of(n * BLOCK, BLOCK), BLOCK)
            return pl.ds(r + n * (BLOCK * dil), BLOCK, stride=dil)

        two_hops = dil > SCATTER_STRIDE
        inner = dil // SCATTER_STRIDE

        def scatter(kind, stat_ref, r, n, val):
            if two_hops:
                rows = pl.ds(r // SCATTER_STRIDE + n * (BLOCK * inner), BLOCK, stride=inner)
                hop_ref[kind, r % SCATTER_STRIDE, rows, :] = val
            else:
                stat_ref[idx, out_rows(r, n), :] = val

        def finish():
            if not two_hops:
                return
            for kind, stat_ref in enumerate((num_ref, max_ref, den_ref)):
                for a in range(SCATTER_STRIDE):
                    for c in range(hop_ref.shape[2] // BLOCK):
                        rows = pl.ds(a + c * (BLOCK * SCATTER_STRIDE), BLOCK, stride=SCATTER_STRIDE)
                        stat_ref[idx, rows, :] = hop_ref[kind, a, c * BLOCK:(c + 1) * BLOCK, :]

        def scores(g, slot, i):
            r, n = locate(g * DIL_GROUP + i)
            q = q_ref[0, 0, r, pl.ds(pl.multiple_of(n * BLOCK, BLOCK), BLOCK), :]
            k = k_ref[0, 0, r, key_rows(n), :]
            zero = jnp.zeros_like(q)
            q2 = jnp.concatenate([jnp.where(low_half, q, zero), jnp.where(low_half, zero, q)], axis=0)
            s_ref[slot, i] = lax.dot_general(q2, k, (((1,), (1,)), ((), ())), preferred_element_type=f32)

        def softmax(g, slot, i):
            r, n = locate(g * DIL_GROUP + i)
            s = s_ref[slot, i] + bias_ref[jnp.minimum(n, 1)]
            m = jnp.max(s, axis=1, keepdims=True)
            p_ref[slot, i] = jnp.exp2(s - m).astype(jnp.bfloat16)
            scatter(1, max_ref, r, n, jnp.where(low_half, m[:BLOCK], m[BLOCK:]))

        def values(g, slot, i):
            r, n = locate(g * DIL_GROUP + i)
            v = jnp.concatenate([v_ref[0, 0, r, key_rows(n), :], ones], axis=1)
            both = jnp.dot(p_ref[slot, i], v, preferred_element_type=f32)
            num, den = both[:, :LANES], both[:, LANES:]
            num = jnp.where(low_half, num[:BLOCK], num[BLOCK:])
            den = jnp.where(low_half, den[:BLOCK], den[BLOCK:])
            if dil != 1:
                scatter(0, num_ref, r, n, num)
                scatter(2, den_ref, r, n, den)
                return
            rows = out_rows(r, n)
            others = [i for i in range(len(DILATIONS)) if i != idx]
            ms = [max_ref[idx, rows, :]] + [max_ref[i, rows, :] for i in others]
            m_all = functools.reduce(jnp.maximum, ms)
            w = jnp.exp2(ms[0] - m_all)
            num, den = w * num, w * den
            for m, i in zip(ms[1:], others):
                w = jnp.exp2(m - m_all)
                num = num + w * num_ref[i, rows, :]
                den = den + w * den_ref[i, rows, :]
            o_ref[rows, :] = (num / den * g_ref[0, rows, :].astype(f32)).astype(o_ref.dtype)

        return scores, softmax, values, finish

    assert DILATIONS[0] == 1
    stages = [branch(2, DILATIONS[2], q16_ref, k16_ref, v16_ref),
              branch(1, DILATIONS[1], q4_ref, k4_ref, v4_ref),
              branch(0, DILATIONS[0], q1_ref, k1_ref, v1_ref)]
    last = n_groups - 1

    def trip(*work):
        for stage, g, slot in work:
            for i in range(DIL_GROUP):
                stage(g, slot, i)

    for b, (scores, softmax, values, _) in enumerate(stages):
        if b == 0:
            trip((scores, 0, 0))
            trip((scores, 1, 1), (softmax, 0, 0))
        else:
            trip((stages[b - 1][2], last, 1), (scores, 1, 1), (softmax, 0, 0))
            stages[b - 1][3]()

        for t in range((n_groups - 2) // 2):
            trip((values, 2 * t, 0), (scores, 2 * t + 2, 0), (softmax, 2 * t + 1, 1))
            trip((values, 2 * t + 1, 1), (scores, 2 * t + 3, 1), (softmax, 2 * t + 2, 0))
        if b + 1 < len(stages):
            trip((values, last - 1, 0), (stages[b + 1][0], 0, 0), (softmax, last, 1))
        else:
            trip((values, last - 1, 0), (softmax, last, 1))
    trip((stages[-1][2], last, 1))


def _out_kernel(x_ref, ya_ref, yb_ref, wa_ref, wb_ref, g_ref, b_ref, o_ref):
    f32 = jnp.float32
    chunks = [pl.ds(r, OUT_CHUNK) for r in range(0, x_ref.shape[0], OUT_CHUNK)]

    def project(rows):
        return (jnp.dot(ya_ref[rows, :], wa_ref[...], preferred_element_type=f32)
                + jnp.dot(yb_ref[rows, :], wb_ref[...], preferred_element_type=f32))

    def normalise(rows, mixed):
        h = DEEPNORM_ALPHA * x_ref[rows, :] + mixed
        mu = jnp.mean(h, axis=-1, keepdims=True)
        c = h - mu
        var = jnp.mean(c * c, axis=-1, keepdims=True)
        o_ref[rows, :] = c * lax.rsqrt(var + LN_EPS) * g_ref[...] + b_ref[...]

    mixed = project(chunks[0])
    for i, rows in enumerate(chunks):
        ahead = project(chunks[i + 1]) if i + 1 < len(chunks) else None
        normalise(rows, mixed)
        mixed = ahead


def _rope_tables(seq, rot_dim, period, rot_offset, pass_rest):
    half = rot_dim // 2
    f32 = np.float32
    inv_freq = ROPE_THETA ** (-np.arange(0, rot_dim, 2, dtype=np.float64) / rot_dim)
    ang = np.arange(seq, dtype=np.float64)[:, None] * inv_freq[None, :]
    cos, sin = np.cos(ang).astype(f32), np.sin(ang).astype(f32)
    zeros = np.zeros((seq, half), f32)
    rest = period - rot_offset - rot_dim
    fill = np.ones if pass_rest else np.zeros
    group_cos = np.concatenate([np.ones((seq, rot_offset), f32), cos, cos, fill((seq, rest), f32)], axis=1)
    group_fwd = np.concatenate([np.zeros((seq, rot_offset), f32), zeros, sin, np.zeros((seq, rest), f32)], axis=1)
    group_bwd = np.concatenate([np.zeros((seq, rot_offset), f32), -sin, zeros, np.zeros((seq, rest), f32)], axis=1)
    reps = LANES // period
    return tuple(jnp.asarray(np.tile(t, (1, reps))) for t in (group_cos, group_fwd, group_bwd))


def _params(*semantics):
    return pltpu.CompilerParams(dimension_semantics=semantics, vmem_limit_bytes=VMEM_LIMIT_BYTES)


def kernel(x, w_in, q_norm_g, kv_norm_g, w_uq, w_ukv, w_out, ln_g, ln_b):
    f32, bf16 = jnp.float32, jnp.bfloat16
    batch, seq, _ = x.shape
    rows = batch * seq
    x2 = x.reshape(rows, D_MODEL)

    rope_end = _LATENT_WIDTH + MLA_ROPE_DIM
    w_lat = w_in[:, :_LATENT_WIDTH].astype(bf16)
    w_kpe = jnp.pad(w_in[:, _LATENT_WIDTH:rope_end].astype(bf16),
                    ((0, 0), (MLA_NOPE_DIM, LANES - MLA_NOPE_DIM - MLA_ROPE_DIM)))
    w_rest = w_in[:, rope_end:].astype(bf16)
    assert w_rest.shape[1] == _REST_WIDTH
    dk = MLA_NOPE_DIM + MLA_ROPE_DIM
    wuq = jnp.pad(w_uq.reshape(Q_LORA_RANK, MLA_HEADS, dk), ((0, 0), (0, 0), (0, LANES - dk)))
    wuq = wuq.reshape(Q_LORA_RANK, MLA_HEADS * LANES).astype(bf16)
    wukv = w_ukv.reshape(KV_LORA_RANK, MLA_HEADS, MLA_NOPE_DIM + MLA_V_DIM)
    wuk = jnp.pad(wukv[:, :, :MLA_NOPE_DIM], ((0, 0), (0, 0), (0, LANES - MLA_NOPE_DIM)))
    wuk = wuk.reshape(KV_LORA_RANK, MLA_HEADS * LANES).astype(bf16)
    wuvt = wukv[:, :, MLA_NOPE_DIM:].reshape(KV_LORA_RANK, MLA_WIDTH).T.astype(bf16)
    wa = w_out[:MLA_WIDTH].astype(bf16)
    wb = w_out[MLA_WIDTH:].astype(bf16)

    mla_tabs = _rope_tables(seq, MLA_ROPE_DIM, LANES, MLA_NOPE_DIM, pass_rest=False)
    dil_tabs = _rope_tables(seq, DIL_ROT_DIM, DIL_HEAD_DIM, 0, pass_rest=True)

    tm = PROJ_ROWS
    seq_tiles = seq // tm
    full = lambda shape: pl.BlockSpec(shape, lambda i: (0,) * len(shape))
    tab = pl.BlockSpec((tm, LANES), lambda i: (i % seq_tiles, 0))
    slab = lambda n: pl.BlockSpec((n, tm, LANES), lambda i: (0, i, 0))
    slab_shape = lambda n: jax.ShapeDtypeStruct((n, rows, LANES), bf16)
    tk = MLA_KEY_TILE
    vt_spec = pl.BlockSpec((tm // tk, MLA_WIDTH, tk), lambda i: (i, 0, 0))
    vt_shape = jax.ShapeDtypeStruct((rows // tk, MLA_WIDTH, tk), bf16)
    res_specs, res_shapes = [], []
    for dil in DILATIONS:
        spec = pl.BlockSpec((HEAD_PAIRS, 1, dil, tm // dil, LANES),
                            lambda i: (0, i // seq_tiles, 0, i % seq_tiles, 0))
        shape = jax.ShapeDtypeStruct((HEAD_PAIRS, batch, dil, seq // dil, LANES), bf16)
        res_specs += [spec] * 3
        res_shapes += [shape] * 3
    qm, km, vt, ga, gb, *dil_in = pl.pallas_call(
        _proj_kernel,
        grid=(rows // tm,),
        in_specs=[pl.BlockSpec((tm, D_MODEL), lambda i: (i, 0)),
                  full(w_lat.shape), full(w_kpe.shape), full(w_rest.shape),
                  full(wuq.shape), full(wuk.shape), full(wuvt.shape),
                  full((1, Q_LORA_RANK)), full((1, KV_LORA_RANK)),
                  tab, tab, tab, tab, tab, tab],
        out_specs=[slab(MLA_HEADS), slab(MLA_HEADS), vt_spec, slab(HEAD_PAIRS), slab(HEAD_PAIRS)] + res_specs,
        out_shape=[slab_shape(MLA_HEADS), slab_shape(MLA_HEADS), vt_shape, slab_shape(HEAD_PAIRS),
                   slab_shape(HEAD_PAIRS)] + res_shapes,
        scratch_shapes=[pltpu.VMEM((3 * HEAD_PAIRS, tm, LANES), f32),
                        pltpu.VMEM((3 * HEAD_PAIRS, 4, tm // 4, LANES), f32)],
        compiler_params=_params("parallel"),
        name="proj",
    )(x2, w_lat, w_kpe, w_rest, wuq, wuk, wuvt, q_norm_g.reshape(1, -1), kv_norm_g.reshape(1, -1),
      *mla_tabs, *dil_tabs)

    t = MLA_TILE
    ya = pl.pallas_call(
        _mla_kernel,
        grid=(batch, HEAD_PAIRS),
        in_specs=[pl.BlockSpec((2, seq, LANES), lambda b, p: (p, b, 0)),
                  pl.BlockSpec((2, seq, LANES), lambda b, p: (p, b, 0)),
                  pl.BlockSpec((seq // tk, LANES, tk), lambda b, p: (b, p, 0)),
                  pl.BlockSpec((1, seq, LANES), lambda b, p: (p, b, 0))],
        out_specs=pl.BlockSpec((seq, LANES), lambda b, p: (b, p)),
        out_shape=jax.ShapeDtypeStruct((rows, MLA_WIDTH), bf16),
        scratch_shapes=[pltpu.VMEM((2, 2, MLA_V_DIM + MLA_ONES_ROWS, t), f32), pltpu.VMEM((2, 2, tk, t), f32)],
        compiler_params=_params("parallel", "parallel"),
        name="mla",
    )(qm, km, vt, ga)

    dil_specs = [pl.BlockSpec((1, 1, dil, seq // dil, LANES), lambda b, p: (p, b, 0, 0, 0))
                 for dil in DILATIONS for _ in range(3)]
    yb = pl.pallas_call(
        _dilated_kernel,
        grid=(batch, HEAD_PAIRS),
        in_specs=dil_specs + [pl.BlockSpec((1, seq, LANES), lambda b, p: (p, b, 0))],
        out_specs=pl.BlockSpec((seq, LANES), lambda b, p: (b, p)),
        out_shape=jax.ShapeDtypeStruct((rows, DIL_WIDTH), bf16),
        scratch_shapes=[pltpu.VMEM((3, seq, LANES), f32)] * 3 + [
            pltpu.VMEM((3, SCATTER_STRIDE, seq // SCATTER_STRIDE, LANES), f32),
            pltpu.VMEM((2, DIL_GROUP, 2 * BLOCK, 2 * BLOCK), f32),
            pltpu.VMEM((2, DIL_GROUP, 2 * BLOCK, 2 * BLOCK), bf16),
            pltpu.VMEM((2, 2 * BLOCK, 2 * BLOCK), f32)],
        compiler_params=_params("parallel", "parallel"),
        name="dilated",
    )(*dil_in, gb)

    to = OUT_ROWS
    const = lambda shape: pl.BlockSpec(shape, lambda i: (0,) * len(shape))
    out = pl.pallas_call(
        _out_kernel,
        grid=(rows // to,),
        in_specs=[pl.BlockSpec((to, D_MODEL), lambda i: (i, 0)),
                  pl.BlockSpec((to, MLA_WIDTH), lambda i: (i, 0)),
                  pl.BlockSpec((to, DIL_WIDTH), lambda i: (i, 0)),
                  const((MLA_WIDTH, D_MODEL)), const((DIL_WIDTH, D_MODEL)),
                  const((1, D_MODEL)), const((1, D_MODEL))],
        out_specs=pl.BlockSpec((to, D_MODEL), lambda i: (i, 0)),
        out_shape=jax.ShapeDtypeStruct((rows, D_MODEL), f32),
        compiler_params=_params("parallel"),
        name="out",
    )(x2, ya, yb, wa, wb, ln_g.reshape(1, -1), ln_b.reshape(1, -1))
    return out.reshape(batch, seq, D_MODEL)
```

```python
import functools

import jax
import jax.numpy as jnp
import numpy as np
from jax import lax
from jax.experimental import pallas as pl
from jax.experimental.pallas import tpu as pltpu

D_MODEL = 1024
ROPE_THETA = 500000.0
BLOCK = 128
NEG = -1e30
RMS_EPS = 1e-6
LN_EPS = 1e-5

MLA_HEADS = 8
MLA_NOPE_DIM = 64
MLA_ROPE_DIM = 32
MLA_V_DIM = 64
Q_LORA_RANK = 384
KV_LORA_RANK = 256
MLA_WIDTH = MLA_HEADS * MLA_V_DIM

DIL_HEADS = 8
DIL_HEAD_DIM = 64
DIL_ROT_DIM = DIL_HEAD_DIM // 4
DIL_WIDTH = DIL_HEADS * DIL_HEAD_DIM
DILATIONS = (1, 4, 16)

DEPTH = 1
DEEPNORM_ALPHA = (2.0 * DEPTH) ** 0.25
LOG2_E = 1.4426950408889634

LANES = 128
HEAD_PAIRS = MLA_HEADS // 2
VMEM_LIMIT_BYTES = 56 * 1024 * 1024

_LATENT_WIDTH = Q_LORA_RANK + KV_LORA_RANK
_OFF_GA = 0
_OFF_QB = _OFF_GA + MLA_WIDTH
_OFF_KB = _OFF_QB + DIL_WIDTH
_OFF_VB = _OFF_KB + DIL_WIDTH
_OFF_GB = _OFF_VB + DIL_WIDTH
_REST_WIDTH = _OFF_GB + DIL_WIDTH

PROJ_ROWS = 512
MLA_KEY_TILE = 256
MLA_TILE = 1024
MLA_CHUNK_ORDER = (0, 2, 1, 3)
MLA_UNROLL_TRIPS = 2
MLA_ONES_ROWS = 16
OUT_ROWS = 2048
OUT_CHUNK = 256
DIL_GROUP = 4
SCATTER_STRIDE = 4


def _rope_lanes(x, cos, sin_fwd, sin_bwd, half):
    fwd = pltpu.roll(x, half, 1)
    bwd = pltpu.roll(x, LANES - half, 1)
    return x * cos + fwd * sin_fwd + bwd * sin_bwd


def _proj_kernel(x_ref, wlat_ref, wkpe_ref, w_ref, wuq_ref, wuk_ref, wuvt_ref, qg_ref, kvg_ref,
                 mcos_ref, msf_ref, msb_ref, dcos_ref, dsf_ref, dsb_ref,
                 qm_ref, km_ref, vt_ref, ga_ref, gb_ref, *dil_and_scratch):
    f32 = jnp.float32
    bf16 = jnp.bfloat16
    xb = x_ref[...].astype(bf16)
    dil_refs = [dil_and_scratch[3 * i:3 * i + 3] for i in range(len(DILATIONS))]
    stage_ref, stage4_ref = dil_and_scratch[-2:]
    tm = x_ref.shape[0]
    unit = 2 * LANES
    assert DILATIONS == (1, 4, 16)

    def emit_residues(which, p, val):
        slab = which * HEAD_PAIRS + p
        q1, q4, q16 = (refs[which] for refs in dil_refs)
        q1[p, 0, 0] = val.astype(bf16)
        stage_ref[slab] = val
        for a in range(4):
            rows4 = stage_ref[slab, pl.ds(a, tm // 4, stride=4), :]
            q4[p, 0, a] = rows4.astype(bf16)
            stage4_ref[slab, a] = rows4
        for a in range(4):
            for b in range(4):
                rows16 = stage4_ref[slab, a, pl.ds(b, tm // 16, stride=4), :]
                q16[p, 0, a + 4 * b] = rows16.astype(bf16)

    def seg(lo):
        return jnp.dot(xb, w_ref[:, lo:lo + unit], preferred_element_type=f32)

    def rms(t, g):
        return t * lax.rsqrt(jnp.mean(t * t, axis=-1, keepdims=True) + RMS_EPS) * g

    mcos, msf, msb = mcos_ref[...], msf_ref[...], msb_ref[...]
    dcos, dsf, dsb = dcos_ref[...], dsf_ref[...], dsb_ref[...]
    mla_scale = (MLA_NOPE_DIM + MLA_ROPE_DIM) ** -0.5 * LOG2_E
    dil_scale = DIL_HEAD_DIM ** -0.5 * LOG2_E

    latent = {}

    def slabs(t):
        return [t[:, j * LANES:(j + 1) * LANES] for j in range(t.shape[1] // LANES)]

    def dilated(which, lo, first_pair):
        def epilogue(t):
            for j, val in enumerate(slabs(t)):
                if which < 2:
                    val = _rope_lanes(val, dcos, dsf, dsb, DIL_ROT_DIM // 2)
                if which == 0:
                    val = val * dil_scale
                emit_residues(which, first_pair + j, val)
        return (lambda: seg(lo)), epilogue

    def gate(out_ref, lo, first_pair):
        def epilogue(t):
            for j, val in enumerate(slabs(jax.nn.silu(t))):
                out_ref[first_pair + j] = val.astype(bf16)
        return (lambda: seg(lo)), epilogue

    def mla_q(first_head):
        def epilogue(t):
            for j, val in enumerate(slabs(t)):
                val = _rope_lanes(val, mcos, msf, msb, MLA_ROPE_DIM // 2) * mla_scale
                qm_ref[first_head + j] = val.astype(bf16)
        lo = first_head * LANES
        return (lambda: jnp.dot(latent["cq"], wuq_ref[:, lo:lo + unit], preferred_element_type=f32)), epilogue

    def mla_k(first_head):
        def epilogue(t):
            for j, val in enumerate(slabs(t)):
                km_ref[first_head + j] = (val + latent["kpe"]).astype(bf16)
        lo = first_head * LANES
        return (lambda: jnp.dot(latent["ckv"], wuk_ref[:, lo:lo + unit], preferred_element_type=f32)), epilogue

    def store_vt(t):
        for c in range(vt_ref.shape[0]):
            vt_ref[c] = t[:, c * MLA_KEY_TILE:(c + 1) * MLA_KEY_TILE].astype(bf16)

    split = unit
    units = [
        (lambda: jnp.dot(xb, wlat_ref[:, :split], preferred_element_type=f32),
         lambda t: latent.update(cq_head=t)),
        (lambda: jnp.dot(xb, wlat_ref[:, split:Q_LORA_RANK], preferred_element_type=f32),
         lambda t: latent.update(cq=rms(jnp.concatenate([latent["cq_head"], t], axis=1), qg_ref[...]).astype(bf16))),
        (lambda: jnp.dot(xb, wlat_ref[:, Q_LORA_RANK:], preferred_element_type=f32),
         lambda t: latent.update(ckv=rms(t, kvg_ref[...]).astype(bf16))),
        dilated(0, _OFF_QB, 0), mla_q(0), gate(ga_ref, _OFF_GA, 0), dilated(0, _OFF_QB + unit, 2), mla_q(2),
        gate(ga_ref, _OFF_GA + unit, 2), dilated(1, _OFF_KB, 0), mla_q(4), gate(gb_ref, _OFF_GB, 0),
        dilated(1, _OFF_KB + unit, 2), mla_q(6), gate(gb_ref, _OFF_GB + unit, 2),
        (lambda: jnp.dot(xb, wkpe_ref[...], preferred_element_type=f32),
         lambda t: latent.update(kpe=_rope_lanes(t, mcos, msf, msb, MLA_ROPE_DIM // 2))),
        mla_k(0), dilated(2, _OFF_VB, 0), mla_k(2), mla_k(4), dilated(2, _OFF_VB + unit, 2), mla_k(6),
        (lambda: lax.dot_general(wuvt_ref[...], latent["ckv"], (((1,), (1,)), ((), ())),
                                 preferred_element_type=f32), store_vt),
    ]
    pending = None
    for matmul, epilogue in units:
        result = matmul()
        if pending is not None:
            pending[0](pending[1])
        pending = (epilogue, result)
    pending[0](pending[1])


def _mla_kernel(q_ref, k_ref, vt_ref, g_ref, o_ref, accs_ref, s_ref):
    f32 = jnp.float32
    bf16 = jnp.bfloat16
    tq, tk = MLA_TILE, MLA_KEY_TILE
    n_chunks = tq // tk
    assert n_chunks % 2 == 0
    nq = q_ref.shape[1] // tq
    assert sorted(MLA_CHUNK_ORDER) == list(range(n_chunks))
    units = [(h, c) for c in MLA_CHUNK_ORDER for h in range(2)]
    ones = jnp.ones((MLA_ONES_ROWS, tk), bf16)
    key = lax.broadcasted_iota(jnp.int32, (tk, tk), 0)
    qry = lax.broadcasted_iota(jnp.int32, (tk, tk), 1)
    lower = key <= qry

    def lanes(c):
        return slice(c * tk, (c + 1) * tk)

    def scores(qi, j, slot, h, c):
        q = q_ref[h, pl.ds(qi * tq + c * tk, tk), :]
        k = k_ref[h, pl.ds(j * tk if isinstance(j, int) else pl.multiple_of(j * tk, tk), tk), :]
        s = lax.dot_general(k, q, (((1,), (1,)), ((), ())), preferred_element_type=f32)
        s_ref[slot, h, :, lanes(c)] = s
        return jnp.max(s, axis=0, keepdims=True)

    def absorb(acc_ref, j, slot, h, c, m_tile, m_old, masked=False):
        s = s_ref[slot, h, :, lanes(c)]
        if masked:
            s = jnp.where(lower, s, NEG)
            m_tile = jnp.max(s, axis=0, keepdims=True)
        m_new = jnp.maximum(m_old, m_tile)
        alpha = jnp.exp2(m_old - m_new)
        p = jnp.exp2(s - m_new).astype(bf16)
        vt = jnp.concatenate([vt_ref[j, h * MLA_V_DIM:(h + 1) * MLA_V_DIM, :], ones], axis=0)
        acc_ref[h, :, lanes(c)] = alpha * acc_ref[h, :, lanes(c)] + jnp.dot(vt, p, preferred_element_type=f32)
        return m_new

    def per_query_tile(qi, acc_ref):
        rows = pl.ds(qi * tq, tq)
        acc_ref[...] = jnp.zeros_like(acc_ref)

        def step(j, slot, tile_max, m_run):
            next_max, m_new = {}, {}
            for u in units:
                next_max[u] = scores(qi, j + 1, 1 - slot, *u)
                m_new[u] = absorb(acc_ref, j, slot, *u, tile_max[u], m_run[u])
            return next_max, m_new

        def body(jj, carry):
            tile_max, m_run = carry
            for i in range(n_chunks):
                tile_max, m_run = step(n_chunks * jj + i, i % 2, tile_max, m_run)
            return tile_max, m_run

        start = {u: jnp.full((1, tk), NEG, f32) for u in units}
        first = {u: scores(qi, 0, 0, *u) for u in units}
        tile_max, m_run = lax.fori_loop(0, qi, body, (first, start), unroll=qi <= MLA_UNROLL_TRIPS)

        base = qi * n_chunks
        for d in range(n_chunks):
            next_max = {}
            for h, c in units:
                if c > d:
                    next_max[h, c] = scores(qi, base + d + 1, (d + 1) % 2, h, c)
                if c >= d:
                    m_run[h, c] = absorb(acc_ref, base + d, d % 2, h, c, tile_max[h, c], m_run[h, c],
                                         masked=(c == d))
            tile_max = next_max

        yt = jnp.concatenate([acc_ref[h, :MLA_V_DIM, :] / acc_ref[h, MLA_V_DIM:MLA_V_DIM + 1, :] for h in range(2)],
                             axis=0)
        o_ref[rows, :] = (yt.T * g_ref[0, rows, :].astype(f32)).astype(o_ref.dtype)

    for qi in reversed(range(nq)):
        per_query_tile(qi, accs_ref.at[qi % 2])


def _dilated_kernel(q1_ref, k1_ref, v1_ref, q4_ref, k4_ref, v4_ref, q16_ref, k16_ref, v16_ref,
                    g_ref, o_ref, num_ref, max_ref, den_ref, hop_ref, s_ref, p_ref, bias_ref):
    f32 = jnp.float32
    seq = num_ref.shape[1]

    lane = lax.broadcasted_iota(jnp.int32, (BLOCK, LANES), 1)
    low_half = lane < DIL_HEAD_DIM
    ones = jnp.ones((2 * BLOCK, LANES), jnp.bfloat16)
    qi = lax.broadcasted_iota(jnp.int32, (2 * BLOCK, 2 * BLOCK), 0) % BLOCK
    kj = lax.broadcasted_iota(jnp.int32, (2 * BLOCK, 2 * BLOCK), 1)
    dist_first = qi - kj
    dist_later = dist_first + BLOCK
    for kind, dist in enumerate((dist_first, dist_later)):
        bias_ref[kind] = jnp.where((dist >= 0) & (dist <= BLOCK), 0.0, NEG).astype(f32)

    n_groups = seq // BLOCK // DIL_GROUP
    assert n_groups % 2 == 0 and n_groups >= 4

    def branch(idx, dil, q_ref, k_ref, v_ref):
        per_residue = q_ref.shape[3] // BLOCK

        def locate(b):
            return b // per_residue, b % per_residue

        def key_rows(n):
            return pl.ds(pl.multiple_of(jnp.maximum(n - 1, 0) * BLOCK, BLOCK), 2 * BLOCK)

        def out_rows(r, n):
            if dil == 1:
                return pl.ds(pl.multiple_of(n * BLOCK, BLOCK), BLOCK)
            return pl.ds(r + n * (BLOCK * dil), BLOCK, stride=dil)

        two_hops = dil > SCATTER_STRIDE
        inner = dil // SCATTER_STRIDE

        merges_hop = dil == SCATTER_STRIDE
        merged = len(DILATIONS) - 1

        def own_rows(r, n):
            return pl.ds((r * per_residue + n) * BLOCK, BLOCK)

        def scatter(kind, stat_ref, r, n, val):
            if two_hops:
                rows = pl.ds(r // SCATTER_STRIDE + n * (BLOCK * inner), BLOCK, stride=inner)
                hop_ref[kind, r % SCATTER_STRIDE, rows, :] = val
            elif merges_hop:
                stat_ref[idx, own_rows(r, n), :] = val
            else:
                stat_ref[idx, out_rows(r, n), :] = val

        def merge_into_hop(r, n, num, den):
            rows = pl.ds(n * BLOCK, BLOCK)
            m_own, m_far = max_ref[idx, own_rows(r, n), :], hop_ref[1, r, rows, :]
            m_all = jnp.maximum(m_own, m_far)
            w_own, w_far = jnp.exp2(m_own - m_all), jnp.exp2(m_far - m_all)
            hop_ref[0, r, rows, :] = w_own * num + w_far * hop_ref[0, r, rows, :]
            hop_ref[2, r, rows, :] = w_own * den + w_far * hop_ref[2, r, rows, :]
            hop_ref[1, r, rows, :] = m_all

        def finish():
            if not merges_hop:
                return
            for kind, stat_ref in enumerate((num_ref, max_ref, den_ref)):
                for a in range(SCATTER_STRIDE):
                    for c in range(hop_ref.shape[2] // BLOCK):
                        rows = pl.ds(a + c * (BLOCK * SCATTER_STRIDE), BLOCK, stride=SCATTER_STRIDE)
                        stat_ref[merged, rows, :] = hop_ref[kind, a, c * BLOCK:(c + 1) * BLOCK, :]

        def scores(g, slot, i):
            r, n = locate(g * DIL_GROUP + i)
            q = q_ref[0, 0, r, pl.ds(pl.multiple_of(n * BLOCK, BLOCK), BLOCK), :]
            k = k_ref[0, 0, r, key_rows(n), :]
            zero = jnp.zeros_like(q)
            q2 = jnp.concatenate([jnp.where(low_half, q, zero), jnp.where(low_half, zero, q)], axis=0)
            s_ref[slot, i] = lax.dot_general(q2, k, (((1,), (1,)), ((), ())), preferred_element_type=f32)

        def softmax(g, slot, i):
            r, n = locate(g * DIL_GROUP + i)
            s = s_ref[slot, i] + bias_ref[jnp.minimum(n, 1)]
            m = jnp.max(s, axis=1, keepdims=True)
            p_ref[slot, i] = jnp.exp2(s - m).astype(jnp.bfloat16)
            scatter(1, max_ref, r, n, jnp.where(low_half, m[:BLOCK], m[BLOCK:]))

        def values(g, slot, i):
            r, n = locate(g * DIL_GROUP + i)
            v = jnp.concatenate([v_ref[0, 0, r, key_rows(n), :], ones], axis=1)
            both = jnp.dot(p_ref[slot, i], v, preferred_element_type=f32)
            num, den = both[:, :LANES], both[:, LANES:]
            num = jnp.where(low_half, num[:BLOCK], num[BLOCK:])
            den = jnp.where(low_half, den[:BLOCK], den[BLOCK:])
            if merges_hop:
                merge_into_hop(r, n, num, den)
                return
            if dil != 1:
                scatter(0, num_ref, r, n, num)
                scatter(2, den_ref, r, n, den)
                return
            rows = out_rows(r, n)
            others = [merged]
            ms = [max_ref[idx, rows, :]] + [max_ref[i, rows, :] for i in others]
            m_all = functools.reduce(jnp.maximum, ms)
            w = jnp.exp2(ms[0] - m_all)
            num, den = w * num, w * den
            for m, i in zip(ms[1:], others):
                w = jnp.exp2(m - m_all)
                num = num + w * num_ref[i, rows, :]
                den = den + w * den_ref[i, rows, :]
            o_ref[rows, :] = (num / den * g_ref[0, rows, :].astype(f32)).astype(o_ref.dtype)

        return scores, softmax, values, finish

    assert DILATIONS[0] == 1
    stages = [branch(2, DILATIONS[2], q16_ref, k16_ref, v16_ref),
              branch(1, DILATIONS[1], q4_ref, k4_ref, v4_ref),
              branch(0, DILATIONS[0], q1_ref, k1_ref, v1_ref)]
    last = n_groups - 1

    def trip(*work):
        for stage, g, slot in work:
            for i in range(DIL_GROUP):
                stage(g, slot, i)

    for b, (scores, softmax, values, _) in enumerate(stages):
        if b == 0:
            trip((scores, 0, 0))
            trip((scores, 1, 1), (softmax, 0, 0))
        else:
            trip((stages[b - 1][2], last, 1), (scores, 1, 1), (softmax, 0, 0))
            stages[b - 1][3]()

        for t in range((n_groups - 2) // 2):
            trip((values, 2 * t, 0), (scores, 2 * t + 2, 0), (softmax, 2 * t + 1, 1))
            trip((values, 2 * t + 1, 1), (scores, 2 * t + 3, 1), (softmax, 2 * t + 2, 0))
        if b + 1 < len(stages):
            trip((values, last - 1, 0), (stages[b + 1][0], 0, 0), (softmax, last, 1))
        else:
            trip((values, last - 1, 0), (softmax, last, 1))
    trip((stages[-1][2], last, 1))


def _out_kernel(x_ref, ya_ref, yb_ref, wa_ref, wb_ref, g_ref, b_ref, o_ref):
    f32 = jnp.float32
    chunks = [pl.ds(r, OUT_CHUNK) for r in range(0, x_ref.shape[0], OUT_CHUNK)]

    def project(rows):
        return (jnp.dot(ya_ref[rows, :], wa_ref[...], preferred_element_type=f32)
                + jnp.dot(yb_ref[rows, :], wb_ref[...], preferred_element_type=f32))

    def normalise(rows, mixed):
        h = DEEPNORM_ALPHA * x_ref[rows, :] + mixed
        mu = jnp.mean(h, axis=-1, keepdims=True)
        c = h - mu
        var = jnp.mean(c * c, axis=-1, keepdims=True)
        o_ref[rows, :] = c * lax.rsqrt(var + LN_EPS) * g_ref[...] + b_ref[...]

    mixed = project(chunks[0])
    for i, rows in enumerate(chunks):
        ahead = project(chunks[i + 1]) if i + 1 < len(chunks) else None
        normalise(rows, mixed)
        mixed = ahead


def _rope_tables(seq, rot_dim, period, rot_offset, pass_rest):
    half = rot_dim // 2
    f32 = np.float32
    inv_freq = ROPE_THETA ** (-np.arange(0, rot_dim, 2, dtype=np.float64) / rot_dim)
    ang = np.arange(seq, dtype=np.float64)[:, None] * inv_freq[None, :]
    cos, sin = np.cos(ang).astype(f32), np.sin(ang).astype(f32)
    zeros = np.zeros((seq, half), f32)
    rest = period - rot_offset - rot_dim
    fill = np.ones if pass_rest else np.zeros
    group_cos = np.concatenate([np.ones((seq, rot_offset), f32), cos, cos, fill((seq, rest), f32)], axis=1)
    group_fwd = np.concatenate([np.zeros((seq, rot_offset), f32), zeros, sin, np.zeros((seq, rest), f32)], axis=1)
    group_bwd = np.concatenate([np.zeros((seq, rot_offset), f32), -sin, zeros, np.zeros((seq, rest), f32)], axis=1)
    reps = LANES // period
    return tuple(jnp.asarray(np.tile(t, (1, reps))) for t in (group_cos, group_fwd, group_bwd))


def _params(*semantics):
    return pltpu.CompilerParams(dimension_semantics=semantics, vmem_limit_bytes=VMEM_LIMIT_BYTES)


def kernel(x, w_in, q_norm_g, kv_norm_g, w_uq, w_ukv, w_out, ln_g, ln_b):
    f32, bf16 = jnp.float32, jnp.bfloat16
    batch, seq, _ = x.shape
    rows = batch * seq
    x2 = x.reshape(rows, D_MODEL)

    rope_end = _LATENT_WIDTH + MLA_ROPE_DIM
    w_lat = w_in[:, :_LATENT_WIDTH].astype(bf16)
    w_kpe = jnp.pad(w_in[:, _LATENT_WIDTH:rope_end].astype(bf16),
                    ((0, 0), (MLA_NOPE_DIM, LANES - MLA_NOPE_DIM - MLA_ROPE_DIM)))
    w_rest = w_in[:, rope_end:].astype(bf16)
    assert w_rest.shape[1] == _REST_WIDTH
    dk = MLA_NOPE_DIM + MLA_ROPE_DIM
    wuq = jnp.pad(w_uq.reshape(Q_LORA_RANK, MLA_HEADS, dk), ((0, 0), (0, 0), (0, LANES - dk)))
    wuq = wuq.reshape(Q_LORA_RANK, MLA_HEADS * LANES).astype(bf16)
    wukv = w_ukv.reshape(KV_LORA_RANK, MLA_HEADS, MLA_NOPE_DIM + MLA_V_DIM)
    wuk = jnp.pad(wukv[:, :, :MLA_NOPE_DIM], ((0, 0), (0, 0), (0, LANES - MLA_NOPE_DIM)))
    wuk = wuk.reshape(KV_LORA_RANK, MLA_HEADS * LANES).astype(bf16)
    wuvt = wukv[:, :, MLA_NOPE_DIM:].reshape(KV_LORA_RANK, MLA_WIDTH).T.astype(bf16)
    wa = w_out[:MLA_WIDTH].astype(bf16)
    wb = w_out[MLA_WIDTH:].astype(bf16)

    mla_tabs = _rope_tables(seq, MLA_ROPE_DIM, LANES, MLA_NOPE_DIM, pass_rest=False)
    dil_tabs = _rope_tables(seq, DIL_ROT_DIM, DIL_HEAD_DIM, 0, pass_rest=True)

    tm = PROJ_ROWS
    seq_tiles = seq // tm
    full = lambda shape: pl.BlockSpec(shape, lambda i: (0,) * len(shape))
    tab = pl.BlockSpec((tm, LANES), lambda i: (i % seq_tiles, 0))
    slab = lambda n: pl.BlockSpec((n, tm, LANES), lambda i: (0, i, 0))
    slab_shape = lambda n: jax.ShapeDtypeStruct((n, rows, LANES), bf16)
    tk = MLA_KEY_TILE
    vt_spec = pl.BlockSpec((tm // tk, MLA_WIDTH, tk), lambda i: (i, 0, 0))
    vt_shape = jax.ShapeDtypeStruct((rows // tk, MLA_WIDTH, tk), bf16)
    res_specs, res_shapes = [], []
    for dil in DILATIONS:
        spec = pl.BlockSpec((HEAD_PAIRS, 1, dil, tm // dil, LANES),
                            lambda i: (0, i // seq_tiles, 0, i % seq_tiles, 0))
        shape = jax.ShapeDtypeStruct((HEAD_PAIRS, batch, dil, seq // dil, LANES), bf16)
        res_specs += [spec] * 3
        res_shapes += [shape] * 3
    qm, km, vt, ga, gb, *dil_in = pl.pallas_call(
        _proj_kernel,
        grid=(rows // tm,),
        in_specs=[pl.BlockSpec((tm, D_MODEL), lambda i: (i, 0)),
                  full(w_lat.shape), full(w_kpe.shape), full(w_rest.shape),
                  full(wuq.shape), full(wuk.shape), full(wuvt.shape),
                  full((1, Q_LORA_RANK)), full((1, KV_LORA_RANK)),
                  tab, tab, tab, tab, tab, tab],
        out_specs=[slab(MLA_HEADS), slab(MLA_HEADS), vt_spec, slab(HEAD_PAIRS), slab(HEAD_PAIRS)] + res_specs,
        out_shape=[slab_shape(MLA_HEADS), slab_shape(MLA_HEADS), vt_shape, slab_shape(HEAD_PAIRS),
                   slab_shape(HEAD_PAIRS)] + res_shapes,
        scratch_shapes=[pltpu.VMEM((3 * HEAD_PAIRS, tm, LANES), f32),
                        pltpu.VMEM((3 * HEAD_PAIRS, 4, tm // 4, LANES), f32)],
        compiler_params=_params("parallel"),
        name="proj",
    )(x2, w_lat, w_kpe, w_rest, wuq, wuk, wuvt, q_norm_g.reshape(1, -1), kv_norm_g.reshape(1, -1),
      *mla_tabs, *dil_tabs)

    t = MLA_TILE
    ya = pl.pallas_call(
        _mla_kernel,
        grid=(batch, HEAD_PAIRS),
        in_specs=[pl.BlockSpec((2, seq, LANES), lambda b, p: (p, b, 0)),
                  pl.BlockSpec((2, seq, LANES), lambda b, p: (p, b, 0)),
                  pl.BlockSpec((seq // tk, LANES, tk), lambda b, p: (b, p, 0)),
                  pl.BlockSpec((1, seq, LANES), lambda b, p: (p, b, 0))],
        out_specs=pl.BlockSpec((seq, LANES), lambda b, p: (b, p)),
        out_shape=jax.ShapeDtypeStruct((rows, MLA_WIDTH), bf16),
        scratch_shapes=[pltpu.VMEM((2, 2, MLA_V_DIM + MLA_ONES_ROWS, t), f32), pltpu.VMEM((2, 2, tk, t), f32)],
        compiler_params=_params("parallel", "parallel"),
        name="mla",
    )(qm, km, vt, ga)

    dil_specs = [pl.BlockSpec((1, 1, dil, seq // dil, LANES), lambda b, p: (p, b, 0, 0, 0))
                 for dil in DILATIONS for _ in range(3)]
    yb = pl.pallas_call(
        _dilated_kernel,
        grid=(batch, HEAD_PAIRS),
        in_specs=dil_specs + [pl.BlockSpec((1, seq, LANES), lambda b, p: (p, b, 0))],
        out_specs=pl.BlockSpec((seq, LANES), lambda b, p: (b, p)),
        out_shape=jax.ShapeDtypeStruct((rows, DIL_WIDTH), bf16),
        scratch_shapes=[pltpu.VMEM((3, seq, LANES), f32)] * 3 + [
            pltpu.VMEM((3, SCATTER_STRIDE, seq // SCATTER_STRIDE, LANES), f32),
            pltpu.VMEM((2, DIL_GROUP, 2 * BLOCK, 2 * BLOCK), f32),
            pltpu.VMEM((2, DIL_GROUP, 2 * BLOCK, 2 * BLOCK), bf16),
            pltpu.VMEM((2, 2 * BLOCK, 2 * BLOCK), f32)],
        compiler_params=_params("parallel", "parallel"),
        name="dilated",
    )(*dil_in, gb)

    to = OUT_ROWS
    const = lambda shape: pl.BlockSpec(shape, lambda i: (0,) * len(shape))
    out = pl.pallas_call(
        _out_kernel,
        grid=(rows // to,),
        in_specs=[pl.BlockSpec((to, D_MODEL), lambda i: (i, 0)),
                  pl.BlockSpec((to, MLA_WIDTH), lambda i: (i, 0)),
                  pl.BlockSpec((to, DIL_WIDTH), lambda i: (i, 0)),
                  const((MLA_WIDTH, D_MODEL)), const((DIL_WIDTH, D_MODEL)),
                  const((1, D_MODEL)), const((1, D_MODEL))],
        out_specs=pl.BlockSpec((to, D_MODEL), lambda i: (i, 0)),
        out_shape=jax.ShapeDtypeStruct((rows, D_MODEL), f32),
        compiler_params=_params("parallel"),
        name="out",
    )(x2, ya, yb, wa, wb, ln_g.reshape(1, -1), ln_b.reshape(1, -1))
    return out.reshape(batch, seq, D_MODEL)
```
